```python
import jax, jax.numpy as jnp
from jax import lax
import numpy as np

D_MODEL = 1024
BATCH = 1
SEQ = 16384
DEPTH = 2
DEC_BATCH = 16
DEC_SEQ = 64
PAST_LEN = 4096

CHUNK = 64
N_META = 16
RET_HEADS = 4
RET_DK = D_MODEL // RET_HEADS
RET_DV = 2 * RET_DK
RET_QK = RET_HEADS * RET_DK
RET_V = RET_HEADS * RET_DV
HG_DK = 128
HG_HEADS = D_MODEL // HG_DK
HG_DV = D_MODEL // HG_HEADS
HG_K = HG_HEADS * HG_DK
HG_V = HG_HEADS * HG_DV
PROJ_SIZES = (RET_QK, RET_QK, RET_V, RET_V, HG_K, HG_K, HG_V, HG_V, D_MODEL, D_MODEL)
PROJ_WIDTH = sum(PROJ_SIZES)
N_GROUPS = 4
EXPERTS_PER_GROUP = 8
N_EXPERTS = N_GROUPS * EXPERTS_PER_GROUP
TOP_K = 2
D_EXPERT = D_MODEL // 2
MOE_ROW_BLOCK = 128
ROPE_BASE = 10000.0
EPS = 1e-6
F32 = jnp.float32

kernel_name = 'hybrid_retention_hgrn2_hmoe_stream_step'


def _split_points():
    return [int(s) for s in np.cumsum(PROJ_SIZES)[:-1]]


def rmsnorm(x, gain):
    xf = x.astype(F32)
    y = xf * lax.rsqrt(jnp.mean(xf * xf, axis=-1, keepdims=True) + EPS) * gain.astype(F32)
    return y.astype(x.dtype)


def head_rmsnorm(x, gain=None):
    y = x * lax.rsqrt(jnp.mean(x * x, axis=-1, keepdims=True) + EPS)
    return y if gain is None else y * gain


def rotary(x, pos):
    d = x.shape[-1]
    inv_freq = 1.0 / (ROPE_BASE ** jnp.linspace(0.0, 1.0, d // 2, dtype=F32))
    ang = pos.astype(F32)[:, None] * inv_freq[None, :]
    cos = jnp.cos(ang)[None, :, None, :]
    sin = jnp.sin(ang)[None, :, None, :]
    x1 = x[..., 0::2]
    x2 = x[..., 1::2]
    return jnp.stack([x1 * cos - x2 * sin, x1 * sin + x2 * cos], axis=-1).reshape(x.shape)


def forget_lower_bounds(lb_logits):
    p = jax.nn.softmax(lb_logits.astype(F32), axis=0)
    c = jnp.cumsum(p, axis=0)
    return c - c[0:1]


def _to_chunks(t, chunk):
    b, l = t.shape[0], t.shape[1]
    return jnp.moveaxis(t.reshape(b, l // chunk, chunk, *t.shape[2:]), 1, 0)


def _from_chunks(t):
    n, b, c = t.shape[0], t.shape[1], t.shape[2]
    return jnp.moveaxis(t, 0, 1).reshape(b, n * c, *t.shape[3:])


def retention_scan(q, k, v, state, chunk):
    h = q.shape[2]
    log_gamma = jnp.log1p(-jnp.exp2(-5.0 - jnp.arange(h, dtype=F32)))
    idx = jnp.arange(chunk, dtype=F32)
    rel = idx[:, None] - idx[None, :]
    causal = rel >= 0
    intra_decay = jnp.where(causal, jnp.exp(log_gamma[:, None, None] * jnp.where(causal, rel, 0.0)), 0.0)
    q_decay = jnp.exp(log_gamma[None, :] * (idx[:, None] + 1.0))
    k_decay = jnp.exp(log_gamma[None, :] * (chunk - 1.0 - idx[:, None]))
    chunk_decay = jnp.exp(log_gamma * chunk)

    def step(s, inp):
        qc, kc, vc = inp
        scores = jnp.einsum('bihd,bjhd->bhij', qc, kc) * intra_decay
        o = (jnp.einsum('bhij,bjhe->bihe', scores, vc)
             + jnp.einsum('bihd,bhde->bihe', qc * q_decay[None, :, :, None], s))
        s = chunk_decay[None, :, None, None] * s + jnp.einsum('bjhd,bjhe->bhde', kc * k_decay[None, :, :, None], vc)
        return s, o

    state, o = lax.scan(step, state, (_to_chunks(q, chunk), _to_chunks(k, chunk), _to_chunks(v, chunk)))
    return _from_chunks(o), state


def hgrn2_scan(q, log_f, k, v, state, chunk):
    causal = jnp.tril(jnp.ones((chunk, chunk), dtype=bool))

    def step(s, inp):
        qc, lfc, kc, vc = inp
        b = jnp.cumsum(lfc, axis=1)
        diff = b[:, :, None] - b[:, None, :]
        w = jnp.exp(jnp.where(causal[None, :, :, None, None], diff, -jnp.inf))
        scores = jnp.einsum('bijhd,bjhd->bhij', qc[:, :, None] * w, kc)
        o = (jnp.einsum('bhij,bjhe->bihe', scores, vc)
             + jnp.einsum('bihd,bhde->bihe', qc * jnp.exp(b), s))
        b_last = b[:, -1]
        s = jnp.exp(b_last)[..., None] * s + jnp.einsum('bjhd,bjhe->bhde', kc * jnp.exp(b_last[:, None] - b), vc)
        return s, o

    state, o = lax.scan(step, state, (_to_chunks(q, chunk), _to_chunks(log_f, chunk),
                                      _to_chunks(k, chunk), _to_chunks(v, chunk)))
    return _from_chunks(o), state


def token_mixers(p, pos, lb, hg_gain, s_ret, s_hg, chunk):
    b, l, _ = p.shape
    rq, rk, rv, rg, hq, hf, hi, hgt, _, _ = jnp.split(p, _split_points(), axis=-1)
    heads = lambda t, n: t.astype(F32).reshape(b, l, n, -1)
    q = rotary(heads(rq, RET_HEADS), pos)
    k = rotary(heads(rk, RET_HEADS), pos) * (RET_DK ** -0.5)
    ro, s_ret = retention_scan(q, k, heads(rv, RET_HEADS), s_ret, chunk)
    ro = head_rmsnorm(ro) * jax.nn.silu(heads(rg, RET_HEADS))
    hf = hf.astype(F32)
    log_f = jnp.logaddexp(jnp.log(lb), jnp.log1p(-lb) + jax.nn.log_sigmoid(hf))
    k_in = (1.0 - lb) * jax.nn.sigmoid(-hf)
    ho, s_hg = hgrn2_scan(jax.nn.silu(heads(hq, HG_HEADS)),
                          log_f.reshape(b, l, HG_HEADS, HG_DK),
                          k_in.reshape(b, l, HG_HEADS, HG_DK),
                          heads(hi, HG_HEADS), s_hg, chunk)
    ho = head_rmsnorm(ho, hg_gain.astype(F32)) * jax.nn.silu(heads(hgt, HG_HEADS))
    return ro.reshape(b, l, RET_V), ho.reshape(b, l, HG_V), s_ret, s_hg


def hier_moe(h, router_group, router_expert, w_gate, w_up, w_down):
    t, d = h.shape
    g_prob = jax.nn.softmax((h @ router_group).astype(F32), axis=-1)
    g_w, g_idx = lax.top_k(g_prob, 1)
    e_logits = (h @ router_expert).astype(F32).reshape(t, N_GROUPS, EXPERTS_PER_GROUP)
    e_logits = e_logits[jnp.arange(t), g_idx[:, 0]]
    e_w, e_idx = lax.top_k(jax.nn.softmax(e_logits, axis=-1), TOP_K)
    weights = g_w * e_w / jnp.sum(e_w, axis=-1, keepdims=True)
    expert = g_idx * EXPERTS_PER_GROUP + e_idx
    flat_e = expert.reshape(-1)
    flat_tok = jnp.repeat(jnp.arange(t, dtype=jnp.int32), TOP_K)
    flat_w = weights.reshape(-1)
    order = jnp.argsort(flat_e)
    se, stok, sw = flat_e[order], flat_tok[order], flat_w[order]
    counts = jnp.bincount(flat_e, length=N_EXPERTS)
    padded = (counts + MOE_ROW_BLOCK - 1) // MOE_ROW_BLOCK * MOE_ROW_BLOCK
    pad_end = jnp.cumsum(padded)
    pad_start = pad_end - padded
    start = jnp.cumsum(counts) - counts
    dest = pad_start[se] + jnp.arange(t * TOP_K) - start[se]
    n_blocks = -(-(t * TOP_K) // MOE_ROW_BLOCK) + N_EXPERTS
    slot_tok = jnp.full((n_blocks * MOE_ROW_BLOCK,), t, dtype=jnp.int32).at[dest].set(stok)
    h_pad = jnp.concatenate([h, jnp.zeros((1, d), h.dtype)], axis=0)
    xb = h_pad[slot_tok].reshape(n_blocks, MOE_ROW_BLOCK, d)
    block_e = jnp.minimum(jnp.searchsorted(pad_end, jnp.arange(n_blocks) * MOE_ROW_BLOCK, side='right'), N_EXPERTS - 1)

    def expert_block(args):
        x_blk, e = args
        return (jax.nn.silu(x_blk @ w_gate[e]) * (x_blk @ w_up[e])) @ w_down[e]

    yb = lax.map(expert_block, (xb, block_e)).reshape(-1, d)
    y = jnp.zeros((t, d), F32).at[stok].add(yb[dest].astype(F32) * sw[:, None])
    return y.astype(h.dtype)


def trunk_layer(x, segments, s_ret, s_hg, lb, mix_norm, w_in, hg_norm, w_ret_branch, w_hg_branch, w_out,
                ffn_norm, router_group, router_expert, w_gate, w_up, w_down):
    b, l, d = x.shape
    p = rmsnorm(x, mix_norm) @ w_in
    ro_parts, ho_parts = [], []
    start = 0
    for length, chunk, pos in segments:
        ro, ho, s_ret, s_hg = token_mixers(p[:, start:start + length], pos, lb, hg_norm, s_ret, s_hg, chunk)
        ro_parts.append(ro)
        ho_parts.append(ho)
        start += length
    ro = jnp.concatenate(ro_parts, axis=1).astype(x.dtype)
    ho = jnp.concatenate(ho_parts, axis=1).astype(x.dtype)
    gate_ret = jax.nn.sigmoid(p[..., PROJ_WIDTH - 2 * D_MODEL:PROJ_WIDTH - D_MODEL])
    gate_hg = jax.nn.sigmoid(p[..., PROJ_WIDTH - D_MODEL:])
    merged = gate_ret * (ro @ w_ret_branch) + gate_hg * (ho @ w_hg_branch)
    x = x + merged @ w_out
    x = x + hier_moe(rmsnorm(x, ffn_norm).reshape(b * l, d), router_group, router_expert,
                     w_gate, w_up, w_down).reshape(b, l, d)
    return x, s_ret, s_hg


def setup_inputs(seed: int = 0) -> dict:
    key = jax.random.key(seed)
    ks = jax.random.split(key, 20)
    nrm = lambda k, shape, scale: jax.random.normal(k, shape, F32) * scale
    return {
        'x_prompt': nrm(ks[0], (BATCH, SEQ, D_MODEL), 1.0),
        'x_sample': nrm(ks[1], (DEC_BATCH, DEC_SEQ, D_MODEL), 1.0),
        'state_ret': nrm(ks[2], (DEPTH, DEC_BATCH, RET_HEADS, RET_DK, RET_DV), 0.5),
        'state_hgrn': nrm(ks[3], (DEPTH, DEC_BATCH, HG_HEADS, HG_DK, HG_DV), 0.5),
        'meta_tokens': nrm(ks[4], (N_META, D_MODEL), 1.0),
        'mix_norm': 1.0 + nrm(ks[5], (DEPTH, D_MODEL), 0.02),
        'w_in': nrm(ks[6], (DEPTH, D_MODEL, PROJ_WIDTH), D_MODEL ** -0.5),
        'hg_lb_logits': nrm(ks[7], (DEPTH, HG_K), 1.0),
        'hg_norm': 1.0 + nrm(ks[8], (DEPTH, HG_DV), 0.02),
        'w_ret_branch': nrm(ks[9], (DEPTH, RET_V, D_MODEL), RET_V ** -0.5),
        'w_hg_branch': nrm(ks[10], (DEPTH, HG_V, D_MODEL), HG_V ** -0.5),
        'w_out': nrm(ks[11], (DEPTH, D_MODEL, D_MODEL), D_MODEL ** -0.5),
        'ffn_norm': 1.0 + nrm(ks[12], (DEPTH, D_MODEL), 0.02),
        'router_group': nrm(ks[13], (DEPTH, D_MODEL, N_GROUPS), D_MODEL ** -0.5),
        'router_expert': nrm(ks[14], (DEPTH, D_MODEL, N_EXPERTS), D_MODEL ** -0.5),
        'w_gate': nrm(ks[15], (DEPTH, N_EXPERTS, D_MODEL, D_EXPERT), D_MODEL ** -0.5),
        'w_up': nrm(ks[16], (DEPTH, N_EXPERTS, D_MODEL, D_EXPERT), D_MODEL ** -0.5),
        'w_down': nrm(ks[17], (DEPTH, N_EXPERTS, D_EXPERT, D_MODEL), D_EXPERT ** -0.5),
        'final_norm': 1.0 + nrm(ks[18], (D_MODEL,), 0.02),
    }


def reference(x_prompt, x_sample, state_ret, state_hgrn, meta_tokens, mix_norm, w_in, hg_lb_logits, hg_norm,
              w_ret_branch, w_hg_branch, w_out, ffn_norm, router_group, router_expert, w_gate, w_up, w_down,
              final_norm):
    lb_all = forget_lower_bounds(hg_lb_logits)
    layer_w = lambda l: (mix_norm[l], w_in[l], hg_norm[l], w_ret_branch[l], w_hg_branch[l], w_out[l],
                         ffn_norm[l], router_group[l], router_expert[l], w_gate[l], w_up[l], w_down[l])

    bp, seq, _ = x_prompt.shape
    h = jnp.concatenate([jnp.broadcast_to(meta_tokens.astype(x_prompt.dtype)[None], (bp, N_META, D_MODEL)),
                         x_prompt], axis=1)
    prompt_segments = ((N_META, N_META, jnp.arange(-N_META, 0)), (seq, CHUNK, jnp.arange(seq)))
    ret_p, hg_p = [], []
    for l in range(DEPTH):
        h, s_r, s_h = trunk_layer(h, prompt_segments,
                                  jnp.zeros((bp, RET_HEADS, RET_DK, RET_DV), F32),
                                  jnp.zeros((bp, HG_HEADS, HG_DK, HG_DV), F32),
                                  lb_all[l], *layer_w(l))
        ret_p.append(s_r.astype(x_prompt.dtype))
        hg_p.append(s_h.astype(x_prompt.dtype))
    y_prompt = rmsnorm(h, final_norm)[:, N_META:]

    bs, dec, _ = x_sample.shape
    sample_segments = ((dec, dec, PAST_LEN + jnp.arange(dec)),)
    h = x_sample
    ret_s, hg_s = [], []
    for l in range(DEPTH):
        h, s_r, s_h = trunk_layer(h, sample_segments, state_ret[l].astype(F32), state_hgrn[l].astype(F32),
                                  lb_all[l], *layer_w(l))
        ret_s.append(s_r.astype(x_sample.dtype))
        hg_s.append(s_h.astype(x_sample.dtype))
    y_sample = rmsnorm(h, final_norm)

    ret_prompt = jnp.stack(ret_p)
    hgrn_prompt = jnp.stack(hg_p)
    ret_sample = jnp.stack(ret_s)
    hgrn_sample = jnp.stack(hg_s)
    return (y_prompt, y_sample, ret_prompt, hgrn_prompt, ret_sample, hgrn_sample)
```

```python
import functools

import numpy as np
import jax
import jax.numpy as jnp
from jax import lax
from jax.experimental import pallas as pl
from jax.experimental.pallas import tpu as pltpu

F32 = jnp.float32
BF16 = jnp.bfloat16

D_MODEL = 1024
CHUNK = 64
N_META = 16
RET_HEADS = 4
RET_DK = 256
RET_DV = 512
HG_HEADS = 8
HG_DK = 128
HG_DV = 128
N_GROUPS = 4
EXPERTS_PER_GROUP = 8
N_EXPERTS = 32
D_EXPERT = 512
ROPE_BASE = 10000.0
EPS = 1e-6
PAST_LEN = 4096

C_RQ, C_RK, C_RV, C_RG = 0, 1024, 2048, 4096
C_HQ, C_HF, C_HI, C_HG = 6144, 7168, 8192, 9216
C_GATES = 10240
PROJ_WIDTH = 12288
MIX_WIDTH = C_GATES

TOKEN_TILE = 256
PROJ_TN = 2048
ROW_BLOCK = 128
SUB = 16
LANES = 128
TILE_ROWS = 8
VMEM_LIMIT = 48 * 1024 * 1024


def _sigmoid(x):
    return 1.0 / (1.0 + jnp.exp(-x))


def _dot(a, b):
    return jnp.dot(a, b, preferred_element_type=F32)


def _dot_nt(a, b):
    return lax.dot_general(a, b, (((1,), (1,)), ((), ())), preferred_element_type=F32)


def _dot_tn(a, b):
    return lax.dot_general(a, b, (((0,), (0,)), ((), ())), preferred_element_type=F32)


def _proj_kernel(x_ref, g_ref, wqk_ref, w_ref, o_ref, wbf_ref):
    j = pl.program_id(0)
    i = pl.program_id(1)

    @pl.when((i == 0) & (j == 0))
    def _():
        wbf_ref[...] = wqk_ref[...].astype(BF16)

    @pl.when((i == 0) & (j > 0))
    def _():
        wbf_ref[...] = w_ref[...].astype(BF16)

    x = x_ref[...]
    xn = x * lax.rsqrt(jnp.mean(x * x, axis=-1, keepdims=True) + EPS) * g_ref[...]
    o_ref[...] = _dot(xn.astype(BF16), wbf_ref[...])


def _proj(x, gain, w_qk, w_in):
    t = x.shape[0]
    tm, tn = TOKEN_TILE, PROJ_TN
    return pl.pallas_call(
        _proj_kernel,
        grid=(PROJ_WIDTH // tn, t // tm),
        in_specs=[
            pl.BlockSpec((tm, D_MODEL), lambda j, i: (i, 0)),
            pl.BlockSpec((1, D_MODEL), lambda j, i: (0, 0)),
            pl.BlockSpec((D_MODEL, tn), lambda j, i: (0, 0)),
            pl.BlockSpec((D_MODEL, tn), lambda j, i: (0, j)),
        ],
        out_specs=pl.BlockSpec((tm, tn), lambda j, i: (i, j)),
        out_shape=jax.ShapeDtypeStruct((t, PROJ_WIDTH), F32),
        scratch_shapes=[pltpu.VMEM((D_MODEL, tn), BF16)],
        compiler_params=pltpu.CompilerParams(
            dimension_semantics=("arbitrary", "arbitrary"), vmem_limit_bytes=VMEM_LIMIT),
        name="proj",
    )(x, gain, w_qk, w_in)


def _mixer_kernel(n_prompt_chunks,
                  p_ref, cos_ref, sin_ref, dint_ref, qdec_ref, kdec_ref, cdec_ref,
                  lbt_ref, hgain_ref, sr_in_ref, sh_in_ref,
                  ro_ref, ho_ref, sr_out_ref, sh_out_ref,
                  sr_scr, sh_scr, hq_scr, hb_scr, hk_scr, hv_scr, hg_scr, ho_scr):
    c = pl.program_id(0)
    half = RET_DK // 2

    @pl.when(c == 0)
    def _():
        sr_scr[...] = jnp.zeros_like(sr_scr)
        sh_scr[...] = jnp.zeros_like(sh_scr)

    @pl.when(c >= n_prompt_chunks)
    def _():
        sr_scr[...] = sr_in_ref[0]
        for h in range(HG_HEADS):
            sh_scr[h] = sh_in_ref[0, h].T

    cos = cos_ref[...]
    sin = sin_ref[...]
    scores_v, to_state = [], []
    for h in range(RET_HEADS):
        q = p_ref[:, C_RQ + h * RET_DK:C_RQ + (h + 1) * RET_DK]
        q1, q2 = q[:, :half], q[:, half:]
        qr = jnp.concatenate([q1 * cos - q2 * sin, q1 * sin + q2 * cos], axis=1)
        to_state.append((qr * qdec_ref[:, h * RET_DK:(h + 1) * RET_DK]).astype(BF16))
        scores_v.append(qr.astype(BF16))
    for h in range(RET_HEADS):
        k = p_ref[:, C_RK + h * RET_DK:C_RK + (h + 1) * RET_DK]
        k1, k2 = k[:, :half], k[:, half:]
        kr = jnp.concatenate([k1 * cos - k2 * sin, k1 * sin + k2 * cos], axis=1) * (RET_DK ** -0.5)
        to_state.append((kr * kdec_ref[:, h * RET_DK:(h + 1) * RET_DK]).astype(BF16))
        scores_v[h] = _dot_nt(scores_v[h], kr.astype(BF16)) * dint_ref[h]
    pr = lax.broadcasted_iota(jnp.int32, (RET_DK, RET_DK), 0)
    pc = lax.broadcasted_iota(jnp.int32, (RET_DK, RET_DK), 1)
    src_lane = jnp.where(pr < half, 2 * pr, 2 * (pr - half) + 1)
    perm = jnp.where(pc == src_lane, 1.0, 0.0).astype(BF16)
    natural = _dot(jnp.concatenate(to_state, axis=0), perm).astype(BF16)
    for h in range(RET_HEADS):
        v = p_ref[:, C_RV + h * RET_DV:C_RV + (h + 1) * RET_DV].astype(BF16)
        s = sr_scr[h]
        qd = natural[h * CHUNK:(h + 1) * CHUNK]
        kd = natural[(RET_HEADS + h) * CHUNK:(RET_HEADS + h + 1) * CHUNK]
        o = _dot(scores_v[h].astype(BF16), v) + _dot(qd, s.astype(BF16))
        sr_scr[h] = cdec_ref[h] * s + _dot_tn(kd, v)
        on = o * lax.rsqrt(jnp.mean(o * o, axis=-1, keepdims=True) + EPS)
        g = p_ref[:, C_RG + h * RET_DV:C_RG + (h + 1) * RET_DV]
        ro_ref[:, h * RET_DV:(h + 1) * RET_DV] = (on * (g * _sigmoid(g))).astype(BF16)

    hf = p_ref[:, C_HF:C_HF + D_MODEL]
    log_lb = lbt_ref[0:1, :]
    log1m_lb = lbt_ref[1:2, :]
    one_m_lb = lbt_ref[2:3, :]
    log_sig = jnp.minimum(hf, 0.0) - jnp.log1p(jnp.exp(-jnp.abs(hf)))
    b_ = log1m_lb + log_sig
    log_f = jnp.maximum(log_lb, b_) + jnp.log1p(jnp.exp(-jnp.abs(log_lb - b_)))
    k_in = one_m_lb * _sigmoid(-hf)
    hq = p_ref[:, C_HQ:C_HQ + D_MODEL]
    q_all = hq * _sigmoid(hq)

    ri = lax.broadcasted_iota(jnp.int32, (CHUNK, CHUNK), 0)
    ci = lax.broadcasted_iota(jnp.int32, (CHUNK, CHUNK), 1)
    tri = jnp.where(ri >= ci, 1.0, 0.0).astype(BF16)
    f_hi = log_f.astype(BF16)
    r1 = log_f - f_hi.astype(F32)
    f_mid = r1.astype(BF16)
    f_lo = (r1 - f_mid.astype(F32)).astype(BF16)
    b_all = _dot(tri, f_hi) + _dot(tri, f_mid) + _dot(tri, f_lo)

    for h in range(HG_HEADS):
        sl = slice(h * HG_DK, (h + 1) * HG_DK)
        hq_scr[h] = q_all[:, sl]
        hb_scr[h] = b_all[:, sl]
        hk_scr[h] = k_in[:, sl]
        hv_scr[h] = p_ref[:, C_HI + h * HG_DV:C_HI + (h + 1) * HG_DV]
        hg_scr[h] = p_ref[:, C_HG + h * HG_DV:C_HG + (h + 1) * HG_DV]

    n_sub = CHUNK // SUB
    row = lax.broadcasted_iota(jnp.int32, (CHUNK, HG_DK), 0)
    rowi = lax.broadcasted_iota(jnp.int32, (SUB, 1), 0)
    hgain = hgain_ref[...]

    def head_body(h, carry):
        q = hq_scr[h]
        b = hb_scr[h]
        k = hk_scr[h]
        v = hv_scr[h]
        st = sh_scr[h]
        vb = v.astype(BF16)
        o = _dot_nt((q * jnp.exp(b)).astype(BF16), st.astype(BF16))
        b_last = b[CHUNK - 1:CHUNK, :]
        kd = (k * jnp.exp(b_last - b)).astype(BF16)
        sh_scr[h] = st * jnp.exp(b_last) + _dot_tn(vb, kd)
        b_end = [b[SUB * (jb + 1) - 1:SUB * (jb + 1), :] for jb in range(n_sub)]
        b_end_rows = jnp.concatenate(
            [jnp.broadcast_to(b_end[jb], (SUB, HG_DK)) for jb in range(n_sub)], axis=0)
        k_hat = k * jnp.exp(b_end_rows - b)
        lhs, rhs = [], []
        for jb in range(n_sub - 1):
            q_hat = q * jnp.exp(jnp.minimum(b - b_end[jb], 0.0))
            lhs.append(jnp.where(row >= SUB * (jb + 1), q_hat, 0.0).astype(BF16))
            rhs.append(jnp.where((row >= SUB * jb) & (row < SUB * (jb + 1)), k_hat, 0.0).astype(BF16))
        a_off = _dot_nt(jnp.concatenate(lhs, axis=1), jnp.concatenate(rhs, axis=1))
        o = o + _dot(a_off.astype(BF16), vb)
        diag = []
        for ib in range(n_sub):
            r0 = SUB * ib
            b_i = b[r0:r0 + SUB, :]
            q_i = q[r0:r0 + SUB, :]
            acc = jnp.zeros((SUB, HG_DV), F32)
            for j in range(SUB):
                b_j = b[r0 + j:r0 + j + 1, :]
                k_j = k[r0 + j:r0 + j + 1, :]
                v_j = v[r0 + j:r0 + j + 1, :]
                e = jnp.exp(jnp.minimum(b_i - b_j, 0.0))
                sc = jnp.sum(e * q_i * k_j, axis=-1, keepdims=True)
                acc = acc + jnp.where(rowi >= j, sc, 0.0) * v_j
            diag.append(acc)
        o = o + jnp.concatenate(diag, axis=0)
        on = o * lax.rsqrt(jnp.mean(o * o, axis=-1, keepdims=True) + EPS) * hgain
        g = hg_scr[h]
        ho_scr[h] = on * (g * _sigmoid(g))
        return carry

    lax.fori_loop(0, HG_HEADS, head_body, 0)

    for h in range(HG_HEADS):
        ho_ref[:, h * HG_DV:(h + 1) * HG_DV] = ho_scr[h].astype(BF16)

    @pl.when(c >= n_prompt_chunks - 1)
    def _():
        sr_out_ref[0] = sr_scr[...]
        for h in range(HG_HEADS):
            sh_out_ref[0, h] = sh_scr[h].T


def _mixer(p, tables, lbt, hgain, state_ret, state_hg, n_prompt_chunks):
    t = p.shape[0]
    n_chunks = t // CHUNK
    n_streams = state_ret.shape[0]
    n_seq = n_streams + 1
    npc = n_prompt_chunks
    cos, sin, dint, qdec, kdec, cdec = tables

    def stream_of(c):
        return jnp.maximum(c - npc, 0)

    def seq_of(c):
        return jnp.maximum(c - npc + 1, 0)

    const2 = lambda c: (0, 0)
    return pl.pallas_call(
        functools.partial(_mixer_kernel, npc),
        grid=(n_chunks,),
        in_specs=[
            pl.BlockSpec((CHUNK, MIX_WIDTH), lambda c: (c, 0)),
            pl.BlockSpec((CHUNK, LANES), lambda c: (jnp.minimum(c, npc), 0)),
            pl.BlockSpec((CHUNK, LANES), lambda c: (jnp.minimum(c, npc), 0)),
            pl.BlockSpec((RET_HEADS, CHUNK, CHUNK), lambda c: (0, 0, 0)),
            pl.BlockSpec((CHUNK, RET_HEADS * RET_DK), const2),
            pl.BlockSpec((CHUNK, RET_HEADS * RET_DK), const2),
            pl.BlockSpec(memory_space=pltpu.SMEM),
            pl.BlockSpec((8, D_MODEL), const2),
            pl.BlockSpec((1, HG_DV), const2),
            pl.BlockSpec((1, RET_HEADS, RET_DK, RET_DV), lambda c: (stream_of(c), 0, 0, 0)),
            pl.BlockSpec((1, HG_HEADS, HG_DK, HG_DV), lambda c: (stream_of(c), 0, 0, 0)),
        ],
        out_specs=[
            pl.BlockSpec((CHUNK, RET_HEADS * RET_DV), lambda c: (c, 0)),
            pl.BlockSpec((CHUNK, HG_HEADS * HG_DV), lambda c: (c, 0)),
            pl.BlockSpec((1, RET_HEADS, RET_DK, RET_DV), lambda c: (seq_of(c), 0, 0, 0)),
            pl.BlockSpec((1, HG_HEADS, HG_DK, HG_DV), lambda c: (seq_of(c), 0, 0, 0)),
        ],
        out_shape=[
            jax.ShapeDtypeStruct((t, RET_HEADS * RET_DV), BF16),
            jax.ShapeDtypeStruct((t, HG_HEADS * HG_DV), BF16),
            jax.ShapeDtypeStruct((n_seq, RET_HEADS, RET_DK, RET_DV), F32),
            jax.ShapeDtypeStruct((n_seq, HG_HEADS, HG_DK, HG_DV), F32),
        ],
        scratch_shapes=[
            pltpu.VMEM((RET_HEADS, RET_DK, RET_DV), F32),
            pltpu.VMEM((HG_HEADS, HG_DV, HG_DK), F32),
            pltpu.VMEM((HG_HEADS, CHUNK, HG_DK), F32),
            pltpu.VMEM((HG_HEADS, CHUNK, HG_DK), F32),
            pltpu.VMEM((HG_HEADS, CHUNK, HG_DK), F32),
            pltpu.VMEM((HG_HEADS, CHUNK, HG_DV), F32),
            pltpu.VMEM((HG_HEADS, CHUNK, HG_DV), F32),
            pltpu.VMEM((HG_HEADS, CHUNK, HG_DV), F32),
        ],
        compiler_params=pltpu.CompilerParams(
            dimension_semantics=("arbitrary",), vmem_limit_bytes=VMEM_LIMIT),
        name="mixer",
    )(p, cos, sin, dint, qdec, kdec, cdec, lbt, hgain, state_ret, state_hg)


def _post_kernel(gp_ref, ro_ref, ho_ref, x_ref, wr_ref, wh_ref, wo_ref, fg_ref,
                 rwh_ref, rwl_ref, x1_ref, h2_ref, route_ref, cnt_ref, run_scr):
    i = pl.program_id(0)
    tm = x_ref.shape[0]

    @pl.when(i == 0)
    def _():
        run_scr[...] = jnp.zeros_like(run_scr)

    gate_r = _sigmoid(gp_ref[:, 0:D_MODEL])
    gate_h = _sigmoid(gp_ref[:, D_MODEL:2 * D_MODEL])
    merged = gate_r * _dot(ro_ref[...], wr_ref[...]) + gate_h * _dot(ho_ref[...], wh_ref[...])
    x1 = x_ref[...] + _dot(merged.astype(BF16), wo_ref[...])
    x1_ref[...] = x1
    h2 = x1 * lax.rsqrt(jnp.mean(x1 * x1, axis=-1, keepdims=True) + EPS) * fg_ref[...]
    for kk in range(TILE_ROWS):
        h2_ref[:, kk, :] = h2[:, kk * LANES:(kk + 1) * LANES]

    h_hi = h2.astype(BF16)
    h_lo = (h2 - h_hi.astype(F32)).astype(BF16)
    logits = _dot(h_hi, rwh_ref[...]) + (_dot(h_hi, rwl_ref[...]) + _dot(h_lo, rwh_ref[...]))

    lane = lax.broadcasted_iota(jnp.int32, (tm, LANES), 1)
    lanef = lane.astype(F32)
    neg_inf = jnp.float32(-jnp.inf)
    big = jnp.float32(1e9)
    is_g = (lane >= N_EXPERTS) & (lane < N_EXPERTS + N_GROUPS)
    gl = jnp.where(is_g, logits, neg_inf)
    gmax = jnp.max(gl, axis=-1, keepdims=True)
    gidx = jnp.min(jnp.where(gl == gmax, lanef, big), axis=-1, keepdims=True) - N_EXPERTS
    g_w = 1.0 / jnp.sum(jnp.exp(gl - gmax), axis=-1, keepdims=True)
    lane_group = jnp.floor(lanef * (1.0 / EXPERTS_PER_GROUP))
    in_group = (lane < N_EXPERTS) & (lane_group == gidx)
    el = jnp.where(in_group, logits, neg_inf)
    m1 = jnp.max(el, axis=-1, keepdims=True)
    i1 = jnp.min(jnp.where(el == m1, lanef, big), axis=-1, keepdims=True)
    el2 = jnp.where(lanef == i1, neg_inf, el)
    m2 = jnp.max(el2, axis=-1, keepdims=True)
    i2 = jnp.min(jnp.where(el2 == m2, lanef, big), axis=-1, keepdims=True)
    tt = jnp.exp(m2 - m1)
    w1 = g_w / (1.0 + tt)
    w2 = g_w * tt / (1.0 + tt)

    oh1 = lanef == i1
    oh2 = lanef == i2
    e_cnt = jnp.where(oh1, 1.0, 0.0) + jnp.where(oh2, 1.0, 0.0)
    ri = lax.broadcasted_iota(jnp.int32, (tm, tm), 0)
    ci = lax.broadcasted_iota(jnp.int32, (tm, tm), 1)
    strict = jnp.where(ri > ci, 1.0, 0.0).astype(BF16)
    prefix = _dot(strict, e_cnt.astype(BF16)) + run_scr[0:1, :]
    r1 = jnp.sum(jnp.where(oh1, prefix, 0.0), axis=-1, keepdims=True)
    r2 = jnp.sum(jnp.where(oh2, prefix, 0.0), axis=-1, keepdims=True)
    run_scr[0:1, :] = run_scr[0:1, :] + jnp.sum(e_cnt, axis=0, keepdims=True)
    cnt_ref[...] = run_scr[...]

    route = jnp.where(lane == 0, i1, 0.0)
    route = jnp.where(lane == 1, i2, route)
    route = jnp.where(lane == 2, w1, route)
    route = jnp.where(lane == 3, w2, route)
    route = jnp.where(lane == 4, r1, route)
    route = jnp.where(lane == 5, r2, route)
    route_ref[...] = route


def _post(p, ro, ho, x, wr, wh, wo, fgain, rw_hi, rw_lo):
    t = x.shape[0]
    tm = TOKEN_TILE
    gate_block = C_GATES // (2 * D_MODEL)
    const2 = lambda i: (0, 0)
    return pl.pallas_call(
        _post_kernel,
        grid=(t // tm,),
        in_specs=[
            pl.BlockSpec((tm, 2 * D_MODEL), lambda i: (i, gate_block)),
            pl.BlockSpec((tm, RET_HEADS * RET_DV), lambda i: (i, 0)),
            pl.BlockSpec((tm, HG_HEADS * HG_DV), lambda i: (i, 0)),
            pl.BlockSpec((tm, D_MODEL), lambda i: (i, 0)),
            pl.BlockSpec((RET_HEADS * RET_DV, D_MODEL), const2),
            pl.BlockSpec((HG_HEADS * HG_DV, D_MODEL), const2),
            pl.BlockSpec((D_MODEL, D_MODEL), const2),
            pl.BlockSpec((1, D_MODEL), const2),
            pl.BlockSpec((D_MODEL, LANES), const2),
            pl.BlockSpec((D_MODEL, LANES), const2),
        ],
        out_specs=[
            pl.BlockSpec((tm, D_MODEL), lambda i: (i, 0)),
            pl.BlockSpec((tm, TILE_ROWS, LANES), lambda i: (i, 0, 0)),
            pl.BlockSpec((tm, LANES), lambda i: (i, 0)),
            pl.BlockSpec((8, LANES), const2),
        ],
        out_shape=[
            jax.ShapeDtypeStruct((t, D_MODEL), F32),
            jax.ShapeDtypeStruct((t, TILE_ROWS, LANES), F32),
            jax.ShapeDtypeStruct((t, LANES), F32),
            jax.ShapeDtypeStruct((8, LANES), F32),
        ],
        scratch_shapes=[pltpu.VMEM((8, LANES), F32)],
        compiler_params=pltpu.CompilerParams(
            dimension_semantics=("arbitrary",), vmem_limit_bytes=VMEM_LIMIT),
        name="post",
    )(p, ro, ho, x, wr, wh, wo, fgain, rw_hi, rw_lo)


DISPATCH_WINDOW = 64


def _dispatch_kernel(pairs_per_step, n_blocks,
                     codes_ref, cnt_ref, h2_hbm, xb_in_hbm,
                     xb_hbm, dest_ref, bexp_ref, nused_ref, pstart_scr, sem):
    del xb_in_hbm
    step = pl.program_id(0)
    base = step * pairs_per_step

    @pl.when(step == 0)
    def _():
        def per_expert(e, carry):
            start, last_e = carry
            pstart_scr[e] = start
            nb = (cnt_ref[e] + (ROW_BLOCK - 1)) // ROW_BLOCK
            first = start // ROW_BLOCK

            def fill(bb, _):
                bexp_ref[first + bb] = e
                return 0

            lax.fori_loop(0, nb, fill, 0)
            return start + nb * ROW_BLOCK, jnp.where(nb > 0, e, last_e)

        total, last_e = lax.fori_loop(0, N_EXPERTS, per_expert, (jnp.int32(0), jnp.int32(0)))
        n_used = total // ROW_BLOCK
        nused_ref[0] = n_used

        def fill_tail(bb, _):
            bexp_ref[bb] = last_e
            return 0

        lax.fori_loop(n_used, n_blocks, fill_tail, 0)

    def row_copy(i, d):
        return pltpu.make_async_copy(h2_hbm.at[i >> 1], xb_hbm.at[d], sem)

    def issue(i):
        code = codes_ref[i]
        d = pstart_scr[code >> 16] + (code & 0xFFFF)
        dest_ref[i] = d
        row_copy(i, d).start()

    def prime(i, _):
        issue(base + i)
        return 0

    def steady(i, _):
        row_copy(0, 0).wait()
        issue(base + i)
        return 0

    def drain(i, _):
        row_copy(0, 0).wait()
        return 0

    @pl.when(step == 0)
    def _():
        lax.fori_loop(0, DISPATCH_WINDOW, prime, 0)
        lax.fori_loop(DISPATCH_WINDOW, pairs_per_step, steady, 0)

    @pl.when(step > 0)
    def _():
        lax.fori_loop(0, pairs_per_step, steady, 0)

    @pl.when(step == pl.num_programs(0) - 1)
    def _():
        lax.fori_loop(0, DISPATCH_WINDOW, drain, 0)


def _dispatch(codes, counts, h2, n_blocks):
    n_pairs = codes.shape[0]
    pairs_per_step = 2 * TOKEN_TILE
    assert n_pairs % pairs_per_step == 0 and pairs_per_step >= DISPATCH_WINDOW
    xb_zero = jnp.zeros((n_blocks * ROW_BLOCK, TILE_ROWS, LANES), F32)
    smem = pl.BlockSpec(memory_space=pltpu.SMEM)
    hbm = pl.BlockSpec(memory_space=pl.ANY)
    return pl.pallas_call(
        functools.partial(_dispatch_kernel, pairs_per_step, n_blocks),
        grid=(n_pairs // pairs_per_step,),
        in_specs=[smem, smem, hbm, hbm],
        out_specs=[hbm, smem, smem, smem],
        out_shape=[
            jax.ShapeDtypeStruct(xb_zero.shape, F32),
            jax.ShapeDtypeStruct((n_pairs,), jnp.int32),
            jax.ShapeDtypeStruct((n_blocks,), jnp.int32),
            jax.ShapeDtypeStruct((1,), jnp.int32),
        ],
        scratch_shapes=[pltpu.SMEM((N_EXPERTS,), jnp.int32), pltpu.SemaphoreType.DMA],
        input_output_aliases={3: 0},
        compiler_params=pltpu.CompilerParams(dimension_semantics=("arbitrary",)),
        name="dispatch",
    )(codes, counts, h2, xb_zero)


def _expert_kernel(bexp_ref, nused_ref, xb_ref, wg_ref, wu_ref, wd_ref, yb_ref,
                   wg_s, wu_s, wd_s):
    b = pl.program_id(0)
    prev = bexp_ref[jnp.maximum(b - 1, 0)]

    @pl.when((b == 0) | (bexp_ref[b] != prev))
    def _():
        wg_s[...] = wg_ref[0].astype(BF16)
        wu_s[...] = wu_ref[0].astype(BF16)
        wd_s[...] = wd_ref[0].astype(BF16)

    @pl.when(b < nused_ref[0])
    def _():
        x = jnp.concatenate([xb_ref[:, kk, :] for kk in range(TILE_ROWS)], axis=1).astype(BF16)
        g = _dot(x, wg_s[...])
        u = _dot(x, wu_s[...])
        a = (g * _sigmoid(g) * u).astype(BF16)
        y = _dot(a, wd_s[...])
        for kk in range(TILE_ROWS):
            yb_ref[:, kk, :] = y[:, kk * LANES:(kk + 1) * LANES]

    @pl.when(b >= nused_ref[0])
    def _():
        yb_ref[...] = jnp.zeros_like(yb_ref)


def _experts(bexp, nused, xb, w_gate, w_up, w_down):
    n_blocks = bexp.shape[0]
    wmap = lambda b, be, nu: (be[b], 0, 0)
    return pl.pallas_call(
        _expert_kernel,
        grid_spec=pltpu.PrefetchScalarGridSpec(
            num_scalar_prefetch=2,
            grid=(n_blocks,),
            in_specs=[
                pl.BlockSpec((ROW_BLOCK, TILE_ROWS, LANES), lambda b, be, nu: (b, 0, 0)),
                pl.BlockSpec((1, D_MODEL, D_EXPERT), wmap),
                pl.BlockSpec((1, D_MODEL, D_EXPERT), wmap),
                pl.BlockSpec((1, D_EXPERT, D_MODEL), wmap),
            ],
            out_specs=pl.BlockSpec((ROW_BLOCK, TILE_ROWS, LANES), lambda b, be, nu: (b, 0, 0)),
            scratch_shapes=[
                pltpu.VMEM((D_MODEL, D_EXPERT), BF16),
                pltpu.VMEM((D_MODEL, D_EXPERT), BF16),
                pltpu.VMEM((D_EXPERT, D_MODEL), BF16),
            ],
        ),
        out_shape=jax.ShapeDtypeStruct(xb.shape, F32),
        compiler_params=pltpu.CompilerParams(
            dimension_semantics=("arbitrary",), vmem_limit_bytes=VMEM_LIMIT),
        name="experts",
    )(bexp, nused, xb, w_gate, w_up, w_down)


def _combine_kernel(with_norm, dest_ref, route_ref, x1_ref, yb_hbm, fg_ref, *rest):
    if with_norm:
        x2_ref, yn_ref, buf, sem = rest
    else:
        x2_ref, buf, sem = rest
    i = pl.program_id(0)
    tm = x1_ref.shape[0]
    base = i * tm

    def row_copy(d, slot):
        return pltpu.make_async_copy(yb_hbm.at[d], buf.at[slot], sem)

    def issue(tt, _):
        row_copy(dest_ref[2 * (base + tt)], tt).start()
        row_copy(dest_ref[2 * (base + tt) + 1], tm + tt).start()
        return 0

    def drain(tt, _):
        row_copy(0, 0).wait()
        return 0

    lax.fori_loop(0, tm, issue, 0)
    lax.fori_loop(0, 2 * tm, drain, 0)

    w1 = route_ref[:, 2:3]
    w2 = route_ref[:, 3:4]
    parts = []
    for kk in range(TILE_ROWS):
        y = w1 * buf[0:tm, kk, :] + w2 * buf[tm:2 * tm, kk, :]
        parts.append(x1_ref[:, kk * LANES:(kk + 1) * LANES] + y)
    x2 = jnp.concatenate(parts, axis=1)
    x2_ref[...] = x2
    if with_norm:
        yn_ref[...] = x2 * lax.rsqrt(jnp.mean(x2 * x2, axis=-1, keepdims=True) + EPS) * fg_ref[...]


def _combine(dest, route, x1, yb, fgain, with_norm):
    t = x1.shape[0]
    tm = ROW_BLOCK
    row_spec = pl.BlockSpec((tm, D_MODEL), lambda i, d: (i, 0))
    out_specs = [row_spec, row_spec] if with_norm else [row_spec]
    out_shape = [jax.ShapeDtypeStruct((t, D_MODEL), F32)] * (2 if with_norm else 1)
    return pl.pallas_call(
        functools.partial(_combine_kernel, with_norm),
        grid_spec=pltpu.PrefetchScalarGridSpec(
            num_scalar_prefetch=1,
            grid=(t // tm,),
            in_specs=[
                pl.BlockSpec((tm, LANES), lambda i, d: (i, 0)),
                row_spec,
                pl.BlockSpec(memory_space=pl.ANY),
                pl.BlockSpec((1, D_MODEL), lambda i, d: (0, 0)),
            ],
            out_specs=out_specs,
            scratch_shapes=[
                pltpu.VMEM((2 * tm, TILE_ROWS, LANES), F32),
                pltpu.SemaphoreType.DMA,
            ],
        ),
        out_shape=out_shape,
        compiler_params=pltpu.CompilerParams(
            dimension_semantics=("arbitrary",), vmem_limit_bytes=VMEM_LIMIT),
        name="combine",
    )(dest, route, x1, yb, fgain)


def _mixer_tables(n_prompt_chunks, prompt_pos0):
    half = RET_DK // 2
    inv_freq = 1.0 / (ROPE_BASE ** jnp.linspace(0.0, 1.0, half, dtype=F32))
    pos_prompt = prompt_pos0 + jnp.arange(n_prompt_chunks * CHUNK)
    pos_sample = PAST_LEN + jnp.arange(CHUNK)
    pos = jnp.concatenate([pos_prompt, pos_sample]).astype(F32)
    ang = pos[:, None] * inv_freq[None, :]
    cos, sin = jnp.cos(ang), jnp.sin(ang)
    log_gamma = jnp.log1p(-jnp.exp2(-5.0 - jnp.arange(RET_HEADS, dtype=F32)))
    idx = jnp.arange(CHUNK, dtype=F32)
    rel = idx[:, None] - idx[None, :]
    causal = rel >= 0
    dint = jnp.where(causal, jnp.exp(log_gamma[:, None, None] * jnp.where(causal, rel, 0.0)), 0.0)
    qdec = jnp.exp(log_gamma[None, :] * (idx[:, None] + 1.0))
    kdec = jnp.exp(log_gamma[None, :] * (CHUNK - 1.0 - idx[:, None]))
    cdec = jnp.exp(log_gamma * CHUNK)
    qdec = jnp.repeat(qdec, RET_DK, axis=1)
    kdec = jnp.repeat(kdec, RET_DK, axis=1)
    return cos, sin, dint, qdec, kdec, cdec


def _deinterleave_qk(w_in_l):
    w = w_in_l[:, :C_RV].reshape(D_MODEL, 2 * RET_HEADS, RET_DK // 2, 2)
    return jnp.swapaxes(w, 2, 3).reshape(D_MODEL, C_RV)


def _router_split(router_group_l, router_expert_l):
    rw = jnp.concatenate(
        [router_expert_l, router_group_l,
         jnp.zeros((D_MODEL, LANES - N_EXPERTS - N_GROUPS), F32)], axis=1)
    hi = rw.astype(BF16)
    lo = (rw - hi.astype(F32)).astype(BF16)
    return hi, lo


def kernel(x_prompt, x_sample, state_ret, state_hgrn, meta_tokens, mix_norm, w_in, hg_lb_logits, hg_norm,
           w_ret_branch, w_hg_branch, w_out, ffn_norm, router_group, router_expert, w_gate, w_up, w_down,
           final_norm):
    depth = w_in.shape[0]
    bp, seq, d = x_prompt.shape
    bs, dec, _ = x_sample.shape
    assert bp == 1 and dec == CHUNK and d == D_MODEL and seq % CHUNK == 0

    n_real = N_META + seq
    n_sample = bs * dec
    n_pad = (-(n_real + n_sample)) % TOKEN_TILE
    while (n_pad + n_real) % CHUNK:
        n_pad += TOKEN_TILE
    n_prompt = n_pad + n_real
    t = n_prompt + n_sample
    npc = n_prompt // CHUNK
    n_blocks = -(-(2 * t) // ROW_BLOCK) + N_EXPERTS

    x = jnp.concatenate([
        jnp.zeros((n_pad, d), F32),
        meta_tokens.astype(F32),
        x_prompt[0],
        x_sample.reshape(n_sample, d),
    ], axis=0)

    tables = _mixer_tables(npc, -(n_pad + N_META))

    prob = jax.nn.softmax(hg_lb_logits.astype(F32), axis=0)
    cum = jnp.cumsum(prob, axis=0)
    lb_all = cum - cum[0:1]

    ret_states, hg_states = [], []
    y_final = None
    for l in range(depth):
        lb = lb_all[l]
        lbt = jnp.concatenate(
            [jnp.log(lb)[None], jnp.log1p(-lb)[None], (1.0 - lb)[None], jnp.zeros((5, d), F32)], axis=0)
        p = _proj(x, mix_norm[l][None], _deinterleave_qk(w_in[l]), w_in[l])
        ro, ho, s_ret, s_hg = _mixer(p, tables, lbt, hg_norm[l][None].astype(F32),
                                     state_ret[l].astype(F32), state_hgrn[l].astype(F32), npc)
        ret_states.append(s_ret)
        hg_states.append(s_hg)
        rw_hi, rw_lo = _router_split(router_group[l], router_expert[l])
        x1, h2, route, counts = _post(
            p, ro, ho, x, w_ret_branch[l].astype(BF16), w_hg_branch[l].astype(BF16),
            w_out[l].astype(BF16), ffn_norm[l][None], rw_hi, rw_lo)
        ids = route[:, 0:2].astype(jnp.int32)
        ranks = route[:, 4:6].astype(jnp.int32)
        codes = (ids * 65536 + ranks).reshape(-1)
        xb, dest, bexp, nused = _dispatch(codes, counts[0].astype(jnp.int32), h2, n_blocks)
        yb = _experts(bexp, nused, xb, w_gate[l], w_up[l], w_down[l])
        last = l == depth - 1
        res = _combine(dest, route, x1, yb, final_norm[None], last)
        x = res[0]
        if last:
            y_final = res[1]

    y_prompt = y_final[n_prompt - seq:n_prompt].reshape(bp, seq, d)
    y_sample = y_final[n_prompt:].reshape(bs, dec, d)
    ret_prompt = jnp.stack([s[0:1] for s in ret_states]).astype(x_prompt.dtype)
    hgrn_prompt = jnp.stack([s[0:1] for s in hg_states]).astype(x_prompt.dtype)
    ret_sample = jnp.stack([s[1:] for s in ret_states]).astype(x_sample.dtype)
    hgrn_sample = jnp.stack([s[1:] for s in hg_states]).astype(x_sample.dtype)
    return (y_prompt, y_sample, ret_prompt, hgrn_prompt, ret_sample, hgrn_sample)
```

```python
import functools

import numpy as np
import jax
import jax.numpy as jnp
from jax import lax
from jax.experimental import pallas as pl
from jax.experimental.pallas import tpu as pltpu

F32 = jnp.float32
BF16 = jnp.bfloat16

D_MODEL = 1024
CHUNK = 64
N_META = 16
RET_HEADS = 4
RET_DK = 256
RET_DV = 512
HG_HEADS = 8
HG_DK = 128
HG_DV = 128
N_GROUPS = 4
EXPERTS_PER_GROUP = 8
N_EXPERTS = 32
D_EXPERT = 512
ROPE_BASE = 10000.0
EPS = 1e-6
PAST_LEN = 4096

C_RQ, C_RK, C_RV, C_RG = 0, 1024, 2048, 4096
C_HQ, C_HF, C_HI, C_HG = 6144, 7168, 8192, 9216
C_GATES = 10240
PROJ_WIDTH = 12288
MIX_WIDTH = C_GATES

TOKEN_TILE = 256
PROJ_TN = 2048
PROJ_TM = 768
ROW_BLOCK = 128
SUB = 16
LANES = 128
TILE_ROWS = 8
VMEM_LIMIT = 56 * 1024 * 1024


def _sigmoid(x):
    return 1.0 / (1.0 + jnp.exp(-x))


def _dot(a, b):
    return jnp.dot(a, b, preferred_element_type=F32)


def _dot_nt(a, b):
    return lax.dot_general(a, b, (((1,), (1,)), ((), ())), preferred_element_type=F32)


def _dot_tn(a, b):
    return lax.dot_general(a, b, (((0,), (0,)), ((), ())), preferred_element_type=F32)


def _proj_kernel(x_ref, g_ref, wqk_ref, w_ref, o_ref, wbf_ref):
    j = pl.program_id(0)
    i = pl.program_id(1)

    @pl.when((i == 0) & (j == 0))
    def _():
        wbf_ref[...] = wqk_ref[...].astype(BF16)

    @pl.when((i == 0) & (j > 0))
    def _():
        wbf_ref[...] = w_ref[...].astype(BF16)

    x = x_ref[...]
    xn = x * lax.rsqrt(jnp.mean(x * x, axis=-1, keepdims=True) + EPS) * g_ref[...]
    o_ref[...] = _dot(xn.astype(BF16), wbf_ref[...])


def _proj(x, gain, w_qk, w_in):
    t = x.shape[0]
    tm, tn = PROJ_TM, PROJ_TN
    assert t % tm == 0
    return pl.pallas_call(
        _proj_kernel,
        grid=(PROJ_WIDTH // tn, t // tm),
        in_specs=[
            pl.BlockSpec((tm, D_MODEL), lambda j, i: (i, 0)),
            pl.BlockSpec((1, D_MODEL), lambda j, i: (0, 0)),
            pl.BlockSpec((D_MODEL, tn), lambda j, i: (0, 0)),
            pl.BlockSpec((D_MODEL, tn), lambda j, i: (0, j)),
        ],
        out_specs=pl.BlockSpec((tm, tn), lambda j, i: (i, j)),
        out_shape=jax.ShapeDtypeStruct((t, PROJ_WIDTH), F32),
        scratch_shapes=[pltpu.VMEM((D_MODEL, tn), BF16)],
        compiler_params=pltpu.CompilerParams(
            dimension_semantics=("arbitrary", "arbitrary"), vmem_limit_bytes=VMEM_LIMIT),
        name="proj",
    )(x, gain, w_qk, w_in)


def _mixer_kernel(n_prompt_chunks,
                  p_ref, cos_ref, sin_ref, dint_ref, qdec_ref, kdec_ref, cdec_ref,
                  lbt_ref, hgain_ref, sr_in_ref, sh_in_ref,
                  ro_ref, ho_ref, sr_out_ref, sh_out_ref,
                  sr_scr, sh_scr, hq_scr, hb_scr, hk_scr, hv_scr, hg_scr, ho_scr):
    c = pl.program_id(0)
    half = RET_DK // 2

    @pl.when(c == 0)
    def _():
        sr_scr[...] = jnp.zeros_like(sr_scr)
        sh_scr[...] = jnp.zeros_like(sh_scr)

    @pl.when(c >= n_prompt_chunks)
    def _():
        sr_scr[...] = sr_in_ref[0]
        for h in range(HG_HEADS):
            sh_scr[h] = sh_in_ref[0, h].T

    cos = cos_ref[...]
    sin = sin_ref[...]
    scores_v, to_state = [], []
    for h in range(RET_HEADS):
        q = p_ref[:, C_RQ + h * RET_DK:C_RQ + (h + 1) * RET_DK]
        q1, q2 = q[:, :half], q[:, half:]
        qr = jnp.concatenate([q1 * cos - q2 * sin, q1 * sin + q2 * cos], axis=1)
        to_state.append((qr * qdec_ref[:, h * RET_DK:(h + 1) * RET_DK]).astype(BF16))
        scores_v.append(qr.astype(BF16))
    for h in range(RET_HEADS):
        k = p_ref[:, C_RK + h * RET_DK:C_RK + (h + 1) * RET_DK]
        k1, k2 = k[:, :half], k[:, half:]
        kr = jnp.concatenate([k1 * cos - k2 * sin, k1 * sin + k2 * cos], axis=1) * (RET_DK ** -0.5)
        to_state.append((kr * kdec_ref[:, h * RET_DK:(h + 1) * RET_DK]).astype(BF16))
        scores_v[h] = _dot_nt(scores_v[h], kr.astype(BF16)) * dint_ref[h]
    pr = lax.broadcasted_iota(jnp.int32, (RET_DK, RET_DK), 0)
    pc = lax.broadcasted_iota(jnp.int32, (RET_DK, RET_DK), 1)
    src_lane = jnp.where(pr < half, 2 * pr, 2 * (pr - half) + 1)
    perm = jnp.where(pc == src_lane, 1.0, 0.0).astype(BF16)
    natural = _dot(jnp.concatenate(to_state, axis=0), perm).astype(BF16)
    for h in range(RET_HEADS):
        v = p_ref[:, C_RV + h * RET_DV:C_RV + (h + 1) * RET_DV].astype(BF16)
        s = sr_scr[h]
        qd = natural[h * CHUNK:(h + 1) * CHUNK]
        kd = natural[(RET_HEADS + h) * CHUNK:(RET_HEADS + h + 1) * CHUNK]
        o = _dot(scores_v[h].astype(BF16), v) + _dot(qd, s.astype(BF16))
        sr_scr[h] = cdec_ref[h] * s + _dot_tn(kd, v)
        on = o * lax.rsqrt(jnp.mean(o * o, axis=-1, keepdims=True) + EPS)
        g = p_ref[:, C_RG + h * RET_DV:C_RG + (h + 1) * RET_DV]
        ro_ref[:, h * RET_DV:(h + 1) * RET_DV] = (on * (g * _sigmoid(g))).astype(BF16)

    hf = p_ref[:, C_HF:C_HF + D_MODEL]
    log_lb = lbt_ref[0:1, :]
    log1m_lb = lbt_ref[1:2, :]
    one_m_lb = lbt_ref[2:3, :]
    log_sig = jnp.minimum(hf, 0.0) - jnp.log1p(jnp.exp(-jnp.abs(hf)))
    b_ = log1m_lb + log_sig
    log_f = jnp.maximum(log_lb, b_) + jnp.log1p(jnp.exp(-jnp.abs(log_lb - b_)))
    k_in = one_m_lb * _sigmoid(-hf)
    hq = p_ref[:, C_HQ:C_HQ + D_MODEL]
    q_all = hq * _sigmoid(hq)

    ri = lax.broadcasted_iota(jnp.int32, (CHUNK, CHUNK), 0)
    ci = lax.broadcasted_iota(jnp.int32, (CHUNK, CHUNK), 1)
    tri = jnp.where(ri >= ci, 1.0, 0.0).astype(BF16)
    f_hi = log_f.astype(BF16)
    r1 = log_f - f_hi.astype(F32)
    f_mid = r1.astype(BF16)
    f_lo = (r1 - f_mid.astype(F32)).astype(BF16)
    b_all = _dot(tri, f_hi) + _dot(tri, f_mid) + _dot(tri, f_lo)

    for h in range(HG_HEADS):
        sl = slice(h * HG_DK, (h + 1) * HG_DK)
        hq_scr[h] = q_all[:, sl]
        hb_scr[h] = b_all[:, sl]
        hk_scr[h] = k_in[:, sl]
        hv_scr[h] = p_ref[:, C_HI + h * HG_DV:C_HI + (h + 1) * HG_DV]
        hg_scr[h] = p_ref[:, C_HG + h * HG_DV:C_HG + (h + 1) * HG_DV]

    n_sub = CHUNK // SUB
    row = lax.broadcasted_iota(jnp.int32, (CHUNK, HG_DK), 0)
    rowi = lax.broadcasted_iota(jnp.int32, (SUB, 1), 0)
    hgain = hgain_ref[...]

    def head_body(h, carry):
        q = hq_scr[h]
        b = hb_scr[h]
        k = hk_scr[h]
        v = hv_scr[h]
        st = sh_scr[h]
        vb = v.astype(BF16)
        o = _dot_nt((q * jnp.exp(b)).astype(BF16), st.astype(BF16))
        b_last = b[CHUNK - 1:CHUNK, :]
        kd = (k * jnp.exp(b_last - b)).astype(BF16)
        sh_scr[h] = st * jnp.exp(b_last) + _dot_tn(vb, kd)
        b_end = [b[SUB * (jb + 1) - 1:SUB * (jb + 1), :] for jb in range(n_sub)]
        b_end_rows = jnp.concatenate(
            [jnp.broadcast_to(b_end[jb], (SUB, HG_DK)) for jb in range(n_sub)], axis=0)
        k_hat = k * jnp.exp(b_end_rows - b)
        lhs, rhs = [], []
        for jb in range(n_sub - 1):
            q_hat = q * jnp.exp(jnp.minimum(b - b_end[jb], 0.0))
            lhs.append(jnp.where(row >= SUB * (jb + 1), q_hat, 0.0).astype(BF16))
            rhs.append(jnp.where((row >= SUB * jb) & (row < SUB * (jb + 1)), k_hat, 0.0).astype(BF16))
        a_off = _dot_nt(jnp.concatenate(lhs, axis=1), jnp.concatenate(rhs, axis=1))
        o = o + _dot(a_off.astype(BF16), vb)
        diag = []
        for ib in range(n_sub):
            r0 = SUB * ib
            b_i = b[r0:r0 + SUB, :]
            q_i = q[r0:r0 + SUB, :]
            acc = jnp.zeros((SUB, HG_DV), F32)
            for j in range(SUB):
                b_j = b[r0 + j:r0 + j + 1, :]
                k_j = k[r0 + j:r0 + j + 1, :]
                v_j = v[r0 + j:r0 + j + 1, :]
                e = jnp.exp(jnp.minimum(b_i - b_j, 0.0))
                sc = jnp.sum(e * q_i * k_j, axis=-1, keepdims=True)
                acc = acc + jnp.where(rowi >= j, sc, 0.0) * v_j
            diag.append(acc)
        o = o + jnp.concatenate(diag, axis=0)
        on = o * lax.rsqrt(jnp.mean(o * o, axis=-1, keepdims=True) + EPS) * hgain
        g = hg_scr[h]
        ho_scr[h] = on * (g * _sigmoid(g))
        return carry

    lax.fori_loop(0, HG_HEADS, head_body, 0)

    for h in range(HG_HEADS):
        ho_ref[:, h * HG_DV:(h + 1) * HG_DV] = ho_scr[h].astype(BF16)

    @pl.when(c >= n_prompt_chunks - 1)
    def _():
        sr_out_ref[0] = sr_scr[...]
        for h in range(HG_HEADS):
            sh_out_ref[0, h] = sh_scr[h].T


def _mixer(p, tables, lbt, hgain, state_ret, state_hg, n_prompt_chunks):
    t = p.shape[0]
    n_chunks = t // CHUNK
    n_streams = state_ret.shape[0]
    n_seq = n_streams + 1
    npc = n_prompt_chunks
    cos, sin, dint, qdec, kdec, cdec = tables

    def stream_of(c):
        return jnp.maximum(c - npc, 0)

    def seq_of(c):
        return jnp.maximum(c - npc + 1, 0)

    const2 = lambda c: (0, 0)
    return pl.pallas_call(
        functools.partial(_mixer_kernel, npc),
        grid=(n_chunks,),
        in_specs=[
            pl.BlockSpec((CHUNK, MIX_WIDTH), lambda c: (c, 0)),
            pl.BlockSpec((CHUNK, LANES), lambda c: (jnp.minimum(c, npc), 0)),
            pl.BlockSpec((CHUNK, LANES), lambda c: (jnp.minimum(c, npc), 0)),
            pl.BlockSpec((RET_HEADS, CHUNK, CHUNK), lambda c: (0, 0, 0)),
            pl.BlockSpec((CHUNK, RET_HEADS * RET_DK), const2),
            pl.BlockSpec((CHUNK, RET_HEADS * RET_DK), const2),
            pl.BlockSpec(memory_space=pltpu.SMEM),
            pl.BlockSpec((8, D_MODEL), const2),
            pl.BlockSpec((1, HG_DV), const2),
            pl.BlockSpec((1, RET_HEADS, RET_DK, RET_DV), lambda c: (stream_of(c), 0, 0, 0)),
            pl.BlockSpec((1, HG_HEADS, HG_DK, HG_DV), lambda c: (stream_of(c), 0, 0, 0)),
        ],
        out_specs=[
            pl.BlockSpec((CHUNK, RET_HEADS * RET_DV), lambda c: (c, 0)),
            pl.BlockSpec((CHUNK, HG_HEADS * HG_DV), lambda c: (c, 0)),
            pl.BlockSpec((1, RET_HEADS, RET_DK, RET_DV), lambda c: (seq_of(c), 0, 0, 0)),
            pl.BlockSpec((1, HG_HEADS, HG_DK, HG_DV), lambda c: (seq_of(c), 0, 0, 0)),
        ],
        out_shape=[
            jax.ShapeDtypeStruct((t, RET_HEADS * RET_DV), BF16),
            jax.ShapeDtypeStruct((t, HG_HEADS * HG_DV), BF16),
            jax.ShapeDtypeStruct((n_seq, RET_HEADS, RET_DK, RET_DV), F32),
            jax.ShapeDtypeStruct((n_seq, HG_HEADS, HG_DK, HG_DV), F32),
        ],
        scratch_shapes=[
            pltpu.VMEM((RET_HEADS, RET_DK, RET_DV), F32),
            pltpu.VMEM((HG_HEADS, HG_DV, HG_DK), F32),
            pltpu.VMEM((HG_HEADS, CHUNK, HG_DK), F32),
            pltpu.VMEM((HG_HEADS, CHUNK, HG_DK), F32),
            pltpu.VMEM((HG_HEADS, CHUNK, HG_DK), F32),
            pltpu.VMEM((HG_HEADS, CHUNK, HG_DV), F32),
            pltpu.VMEM((HG_HEADS, CHUNK, HG_DV), F32),
            pltpu.VMEM((HG_HEADS, CHUNK, HG_DV), F32),
        ],
        compiler_params=pltpu.CompilerParams(
            dimension_semantics=("arbitrary",), vmem_limit_bytes=VMEM_LIMIT),
        name="mixer",
    )(p, cos, sin, dint, qdec, kdec, cdec, lbt, hgain, state_ret, state_hg)


def _post_kernel(gp_ref, ro_ref, ho_ref, x_ref, wr_ref, wh_ref, wo_ref, fg_ref,
                 rwh_ref, rwl_ref, x1_ref, h2_ref, route_ref, cnt_ref, run_scr):
    i = pl.program_id(0)
    tm = x_ref.shape[0]

    @pl.when(i == 0)
    def _():
        run_scr[...] = jnp.zeros_like(run_scr)

    gate_r = _sigmoid(gp_ref[:, 0:D_MODEL])
    gate_h = _sigmoid(gp_ref[:, D_MODEL:2 * D_MODEL])
    merged = gate_r * _dot(ro_ref[...], wr_ref[...]) + gate_h * _dot(ho_ref[...], wh_ref[...])
    x1 = x_ref[...] + _dot(merged.astype(BF16), wo_ref[...])
    x1_ref[...] = x1
    h2 = x1 * lax.rsqrt(jnp.mean(x1 * x1, axis=-1, keepdims=True) + EPS) * fg_ref[...]
    for kk in range(TILE_ROWS):
        h2_ref[:, kk, :] = h2[:, kk * LANES:(kk + 1) * LANES]

    h_hi = h2.astype(BF16)
    h_lo = (h2 - h_hi.astype(F32)).astype(BF16)
    logits = _dot(h_hi, rwh_ref[...]) + (_dot(h_hi, rwl_ref[...]) + _dot(h_lo, rwh_ref[...]))

    lane = lax.broadcasted_iota(jnp.int32, (tm, LANES), 1)
    lanef = lane.astype(F32)
    neg_inf = jnp.float32(-jnp.inf)
    big = jnp.float32(1e9)
    is_g = (lane >= N_EXPERTS) & (lane < N_EXPERTS + N_GROUPS)
    gl = jnp.where(is_g, logits, neg_inf)
    gmax = jnp.max(gl, axis=-1, keepdims=True)
    gidx = jnp.min(jnp.where(gl == gmax, lanef, big), axis=-1, keepdims=True) - N_EXPERTS
    g_w = 1.0 / jnp.sum(jnp.exp(gl - gmax), axis=-1, keepdims=True)
    lane_group = jnp.floor(lanef * (1.0 / EXPERTS_PER_GROUP))
    in_group = (lane < N_EXPERTS) & (lane_group == gidx)
    el = jnp.where(in_group, logits, neg_inf)
    m1 = jnp.max(el, axis=-1, keepdims=True)
    i1 = jnp.min(jnp.where(el == m1, lanef, big), axis=-1, keepdims=True)
    el2 = jnp.where(lanef == i1, neg_inf, el)
    m2 = jnp.max(el2, axis=-1, keepdims=True)
    i2 = jnp.min(jnp.where(el2 == m2, lanef, big), axis=-1, keepdims=True)
    tt = jnp.exp(m2 - m1)
    w1 = g_w / (1.0 + tt)
    w2 = g_w * tt / (1.0 + tt)

    oh1 = lanef == i1
    oh2 = lanef == i2
    e_cnt = jnp.where(oh1, 1.0, 0.0) + jnp.where(oh2, 1.0, 0.0)
    ri = lax.broadcasted_iota(jnp.int32, (tm, tm), 0)
    ci = lax.broadcasted_iota(jnp.int32, (tm, tm), 1)
    strict = jnp.where(ri > ci, 1.0, 0.0).astype(BF16)
    prefix = _dot(strict, e_cnt.astype(BF16)) + run_scr[0:1, :]
    r1 = jnp.sum(jnp.where(oh1, prefix, 0.0), axis=-1, keepdims=True)
    r2 = jnp.sum(jnp.where(oh2, prefix, 0.0), axis=-1, keepdims=True)
    run_scr[0:1, :] = run_scr[0:1, :] + jnp.sum(e_cnt, axis=0, keepdims=True)
    cnt_ref[...] = run_scr[...]

    route = jnp.where(lane == 0, i1, 0.0)
    route = jnp.where(lane == 1, i2, route)
    route = jnp.where(lane == 2, w1, route)
    route = jnp.where(lane == 3, w2, route)
    route = jnp.where(lane == 4, r1, route)
    route = jnp.where(lane == 5, r2, route)
    route_ref[...] = route


def _post(p, ro, ho, x, wr, wh, wo, fgain, rw_hi, rw_lo):
    t = x.shape[0]
    tm = TOKEN_TILE
    gate_block = C_GATES // (2 * D_MODEL)
    const2 = lambda i: (0, 0)
    return pl.pallas_call(
        _post_kernel,
        grid=(t // tm,),
        in_specs=[
            pl.BlockSpec((tm, 2 * D_MODEL), lambda i: (i, gate_block)),
            pl.BlockSpec((tm, RET_HEADS * RET_DV), lambda i: (i, 0)),
            pl.BlockSpec((tm, HG_HEADS * HG_DV), lambda i: (i, 0)),
            pl.BlockSpec((tm, D_MODEL), lambda i: (i, 0)),
            pl.BlockSpec((RET_HEADS * RET_DV, D_MODEL), const2),
            pl.BlockSpec((HG_HEADS * HG_DV, D_MODEL), const2),
            pl.BlockSpec((D_MODEL, D_MODEL), const2),
            pl.BlockSpec((1, D_MODEL), const2),
            pl.BlockSpec((D_MODEL, LANES), const2),
            pl.BlockSpec((D_MODEL, LANES), const2),
        ],
        out_specs=[
            pl.BlockSpec((tm, D_MODEL), lambda i: (i, 0)),
            pl.BlockSpec((tm, TILE_ROWS, LANES), lambda i: (i, 0, 0)),
            pl.BlockSpec((tm, LANES), lambda i: (i, 0)),
            pl.BlockSpec((8, LANES), const2),
        ],
        out_shape=[
            jax.ShapeDtypeStruct((t, D_MODEL), F32),
            jax.ShapeDtypeStruct((t, TILE_ROWS, LANES), F32),
            jax.ShapeDtypeStruct((t, LANES), F32),
            jax.ShapeDtypeStruct((8, LANES), F32),
        ],
        scratch_shapes=[pltpu.VMEM((8, LANES), F32)],
        compiler_params=pltpu.CompilerParams(
            dimension_semantics=("arbitrary",), vmem_limit_bytes=VMEM_LIMIT),
        name="post",
    )(p, ro, ho, x, wr, wh, wo, fgain, rw_hi, rw_lo)


def _plan_kernel(pairs_per_step, n_blocks,
                 codes_ref, cnt_ref, slot_ref, bexp_ref, nused_ref, pstart_ref):
    step = pl.program_id(0)
    base = step * pairs_per_step

    @pl.when(step == 0)
    def _():
        def per_expert(e, carry):
            start, last_e = carry
            pstart_ref[e] = start
            nb = (cnt_ref[e] + (ROW_BLOCK - 1)) // ROW_BLOCK
            first = start // ROW_BLOCK

            def fill(bb, _):
                bexp_ref[first + bb] = e
                return 0

            lax.fori_loop(0, nb, fill, 0)
            return start + nb * ROW_BLOCK, jnp.where(nb > 0, e, last_e)

        total, last_e = lax.fori_loop(0, N_EXPERTS, per_expert, (jnp.int32(0), jnp.int32(0)))
        n_used = total // ROW_BLOCK
        nused_ref[0] = n_used

        def fill_tail(bb, _):
            bexp_ref[bb] = last_e
            return 0

        lax.fori_loop(n_used, n_blocks, fill_tail, 0)

        def init(r, _):
            slot_ref[r] = 0
            return 0

        lax.fori_loop(0, n_blocks * ROW_BLOCK, init, 0)

    def place(i, _):
        code = codes_ref[base + i]
        slot_ref[pstart_ref[code >> 16] + (code & 0xFFFF)] = (base + i) >> 1
        return 0

    lax.fori_loop(0, pairs_per_step, place, 0)


def _plan(codes, counts, n_blocks):
    n_pairs = codes.shape[0]
    pairs_per_step = 2 * TOKEN_TILE
    assert n_pairs % pairs_per_step == 0
    smem = pl.BlockSpec(memory_space=pltpu.SMEM)
    return pl.pallas_call(
        functools.partial(_plan_kernel, pairs_per_step, n_blocks),
        grid=(n_pairs // pairs_per_step,),
        in_specs=[smem, smem],
        out_specs=[smem, smem, smem, smem],
        out_shape=[
            jax.ShapeDtypeStruct((n_blocks * ROW_BLOCK,), jnp.int32),
            jax.ShapeDtypeStruct((n_blocks,), jnp.int32),
            jax.ShapeDtypeStruct((1,), jnp.int32),
            jax.ShapeDtypeStruct((N_EXPERTS,), jnp.int32),
        ],
        compiler_params=pltpu.CompilerParams(dimension_semantics=("arbitrary",)),
        name="plan",
    )(codes, counts)


def _expert_kernel(bexp_ref, nused_ref, slot_ref, h2_hbm, wg_ref, wu_ref, wd_ref, yb_ref,
                   xbuf, sems, wg_s, wu_s, wd_s):
    b = pl.program_id(0)
    n_used = nused_ref[0]

    def row_copy(tok, buf_slot, r):
        return pltpu.make_async_copy(h2_hbm.at[tok], xbuf.at[buf_slot, r], sems.at[buf_slot])

    def gather(block):
        buf_slot = block % 2

        def issue(r, _):
            row_copy(slot_ref[block * ROW_BLOCK + r], buf_slot, r).start()
            return 0

        lax.fori_loop(0, ROW_BLOCK, issue, 0, unroll=8)

    @pl.when((b == 0) & (n_used > 0))
    def _():
        gather(b)

    @pl.when(b + 1 < n_used)
    def _():
        gather(b + 1)

    prev = bexp_ref[jnp.maximum(b - 1, 0)]

    @pl.when((b == 0) | (bexp_ref[b] != prev))
    def _():
        wg_s[...] = wg_ref[0].astype(BF16)
        wu_s[...] = wu_ref[0].astype(BF16)
        wd_s[...] = wd_ref[0].astype(BF16)

    @pl.when(b < n_used)
    def _():
        buf_slot = b % 2

        def drain(r, _):
            row_copy(0, buf_slot, 0).wait()
            return 0

        lax.fori_loop(0, ROW_BLOCK, drain, 0)
        x = jnp.concatenate([xbuf[buf_slot, :, kk, :] for kk in range(TILE_ROWS)], axis=1).astype(BF16)
        g = _dot(x, wg_s[...])
        u = _dot(x, wu_s[...])
        a = (g * _sigmoid(g) * u).astype(BF16)
        y = _dot(a, wd_s[...])
        for kk in range(TILE_ROWS):
            yb_ref[:, kk, :] = y[:, kk * LANES:(kk + 1) * LANES]

    @pl.when(b >= n_used)
    def _():
        yb_ref[...] = jnp.zeros_like(yb_ref)


def _experts(bexp, nused, slot_tok, h2, w_gate, w_up, w_down):
    n_blocks = bexp.shape[0]
    wmap = lambda b, be, nu, st: (be[b], 0, 0)
    return pl.pallas_call(
        _expert_kernel,
        grid_spec=pltpu.PrefetchScalarGridSpec(
            num_scalar_prefetch=3,
            grid=(n_blocks,),
            in_specs=[
                pl.BlockSpec(memory_space=pl.ANY),
                pl.BlockSpec((1, D_MODEL, D_EXPERT), wmap),
                pl.BlockSpec((1, D_MODEL, D_EXPERT), wmap),
                pl.BlockSpec((1, D_EXPERT, D_MODEL), wmap),
            ],
            out_specs=pl.BlockSpec((ROW_BLOCK, TILE_ROWS, LANES), lambda b, be, nu, st: (b, 0, 0)),
            scratch_shapes=[
                pltpu.VMEM((2, ROW_BLOCK, TILE_ROWS, LANES), F32),
                pltpu.SemaphoreType.DMA((2,)),
                pltpu.VMEM((D_MODEL, D_EXPERT), BF16),
                pltpu.VMEM((D_MODEL, D_EXPERT), BF16),
                pltpu.VMEM((D_EXPERT, D_MODEL), BF16),
            ],
        ),
        out_shape=jax.ShapeDtypeStruct((n_blocks * ROW_BLOCK, TILE_ROWS, LANES), F32),
        compiler_params=pltpu.CompilerParams(
            dimension_semantics=("arbitrary",), vmem_limit_bytes=VMEM_LIMIT),
        name="experts",
    )(bexp, nused, slot_tok, h2, w_gate, w_up, w_down)


def _combine_kernel(with_norm, codes_ref, pstart_ref, route_ref, x1_ref, yb_hbm, fg_ref, *rest):
    if with_norm:
        x2_ref, yn_ref, buf, sems = rest
    else:
        x2_ref, buf, sems = rest
    i = pl.program_id(0)
    tm = x1_ref.shape[0]

    def row_copy(d, buf_slot, r):
        return pltpu.make_async_copy(yb_hbm.at[d], buf.at[buf_slot, r], sems.at[buf_slot])

    def dest_of(pair):
        code = codes_ref[pair]
        return pstart_ref[code >> 16] + (code & 0xFFFF)

    def gather(tile):
        buf_slot = tile % 2
        base = tile * tm

        def issue(tt, _):
            row_copy(dest_of(2 * (base + tt)), buf_slot, tt).start()
            row_copy(dest_of(2 * (base + tt) + 1), buf_slot, tm + tt).start()
            return 0

        lax.fori_loop(0, tm, issue, 0, unroll=4)

    @pl.when(i == 0)
    def _():
        gather(i)

    @pl.when(i + 1 < pl.num_programs(0))
    def _():
        gather(i + 1)

    buf_slot = i % 2

    def drain(tt, _):
        row_copy(0, buf_slot, 0).wait()
        return 0

    lax.fori_loop(0, 2 * tm, drain, 0)

    w1 = route_ref[:, 2:3]
    w2 = route_ref[:, 3:4]
    parts = []
    for kk in range(TILE_ROWS):
        y = w1 * buf[buf_slot, 0:tm, kk, :] + w2 * buf[buf_slot, tm:2 * tm, kk, :]
        parts.append(x1_ref[:, kk * LANES:(kk + 1) * LANES] + y)
    x2 = jnp.concatenate(parts, axis=1)
    x2_ref[...] = x2
    if with_norm:
        yn_ref[...] = x2 * lax.rsqrt(jnp.mean(x2 * x2, axis=-1, keepdims=True) + EPS) * fg_ref[...]


def _combine(codes, pstart, route, x1, yb, fgain, with_norm):
    t = x1.shape[0]
    tm = ROW_BLOCK
    row_spec = pl.BlockSpec((tm, D_MODEL), lambda i, cd, ps: (i, 0))
    out_specs = [row_spec, row_spec] if with_norm else [row_spec]
    out_shape = [jax.ShapeDtypeStruct((t, D_MODEL), F32)] * (2 if with_norm else 1)
    return pl.pallas_call(
        functools.partial(_combine_kernel, with_norm),
        grid_spec=pltpu.PrefetchScalarGridSpec(
            num_scalar_prefetch=2,
            grid=(t // tm,),
            in_specs=[
                pl.BlockSpec((tm, LANES), lambda i, cd, ps: (i, 0)),
                row_spec,
                pl.BlockSpec(memory_space=pl.ANY),
                pl.BlockSpec((1, D_MODEL), lambda i, cd, ps: (0, 0)),
            ],
            out_specs=out_specs,
            scratch_shapes=[
                pltpu.VMEM((2, 2 * tm, TILE_ROWS, LANES), F32),
                pltpu.SemaphoreType.DMA((2,)),
            ],
        ),
        out_shape=out_shape,
        compiler_params=pltpu.CompilerParams(
            dimension_semantics=("arbitrary",), vmem_limit_bytes=VMEM_LIMIT),
        name="combine",
    )(codes, pstart, route, x1, yb, fgain)


def _mixer_tables(n_prompt_chunks, prompt_pos0):
    half = RET_DK // 2
    inv_freq = 1.0 / (ROPE_BASE ** jnp.linspace(0.0, 1.0, half, dtype=F32))
    pos_prompt = prompt_pos0 + jnp.arange(n_prompt_chunks * CHUNK)
    pos_sample = PAST_LEN + jnp.arange(CHUNK)
    pos = jnp.concatenate([pos_prompt, pos_sample]).astype(F32)
    ang = pos[:, None] * inv_freq[None, :]
    cos, sin = jnp.cos(ang), jnp.sin(ang)
    log_gamma = jnp.log1p(-jnp.exp2(-5.0 - jnp.arange(RET_HEADS, dtype=F32)))
    idx = jnp.arange(CHUNK, dtype=F32)
    rel = idx[:, None] - idx[None, :]
    causal = rel >= 0
    dint = jnp.where(causal, jnp.exp(log_gamma[:, None, None] * jnp.where(causal, rel, 0.0)), 0.0)
    qdec = jnp.exp(log_gamma[None, :] * (idx[:, None] + 1.0))
    kdec = jnp.exp(log_gamma[None, :] * (CHUNK - 1.0 - idx[:, None]))
    cdec = jnp.exp(log_gamma * CHUNK)
    qdec = jnp.repeat(qdec, RET_DK, axis=1)
    kdec = jnp.repeat(kdec, RET_DK, axis=1)
    return cos, sin, dint, qdec, kdec, cdec


def _deinterleave_qk(w_in_l):
    w = w_in_l[:, :C_RV].reshape(D_MODEL, 2 * RET_HEADS, RET_DK // 2, 2)
    return jnp.swapaxes(w, 2, 3).reshape(D_MODEL, C_RV)


def _router_split(router_group_l, router_expert_l):
    rw = jnp.concatenate(
        [router_expert_l, router_group_l,
         jnp.zeros((D_MODEL, LANES - N_EXPERTS - N_GROUPS), F32)], axis=1)
    hi = rw.astype(BF16)
    lo = (rw - hi.astype(F32)).astype(BF16)
    return hi, lo


def kernel(x_prompt, x_sample, state_ret, state_hgrn, meta_tokens, mix_norm, w_in, hg_lb_logits, hg_norm,
           w_ret_branch, w_hg_branch, w_out, ffn_norm, router_group, router_expert, w_gate, w_up, w_down,
           final_norm):
    depth = w_in.shape[0]
    bp, seq, d = x_prompt.shape
    bs, dec, _ = x_sample.shape
    assert bp == 1 and dec == CHUNK and d == D_MODEL and seq % CHUNK == 0

    n_real = N_META + seq
    n_sample = bs * dec
    n_pad = (-(n_real + n_sample)) % PROJ_TM
    while (n_pad + n_real) % CHUNK:
        n_pad += PROJ_TM
    n_prompt = n_pad + n_real
    assert n_pad > 0
    t = n_prompt + n_sample
    npc = n_prompt // CHUNK
    n_blocks = -(-(2 * t) // ROW_BLOCK) + N_EXPERTS

    x = jnp.concatenate([
        jnp.zeros((n_pad, d), F32),
        meta_tokens.astype(F32),
        x_prompt[0],
        x_sample.reshape(n_sample, d),
    ], axis=0)

    tables = _mixer_tables(npc, -(n_pad + N_META))

    prob = jax.nn.softmax(hg_lb_logits.astype(F32), axis=0)
    cum = jnp.cumsum(prob, axis=0)
    lb_all = cum - cum[0:1]

    ret_states, hg_states = [], []
    y_final = None
    for l in range(depth):
        lb = lb_all[l]
        lbt = jnp.concatenate(
            [jnp.log(lb)[None], jnp.log1p(-lb)[None], (1.0 - lb)[None], jnp.zeros((5, d), F32)], axis=0)
        p = _proj(x, mix_norm[l][None], _deinterleave_qk(w_in[l]), w_in[l])
        ro, ho, s_ret, s_hg = _mixer(p, tables, lbt, hg_norm[l][None].astype(F32),
                                     state_ret[l].astype(F32), state_hgrn[l].astype(F32), npc)
        ret_states.append(s_ret)
        hg_states.append(s_hg)
        rw_hi, rw_lo = _router_split(router_group[l], router_expert[l])
        x1, h2, route, counts = _post(
            p, ro, ho, x, w_ret_branch[l].astype(BF16), w_hg_branch[l].astype(BF16),
            w_out[l].astype(BF16), ffn_norm[l][None], rw_hi, rw_lo)
        ids = route[:, 0:2].astype(jnp.int32)
        ranks = route[:, 4:6].astype(jnp.int32)
        codes = (ids * 65536 + ranks).reshape(-1)
        slot_tok, bexp, nused, pstart = _plan(codes, counts[0].astype(jnp.int32), n_blocks)
        yb = _experts(bexp, nused, slot_tok, h2, w_gate[l], w_up[l], w_down[l])
        last = l == depth - 1
        res = _combine(codes, pstart, route, x1, yb, final_norm[None], last)
        x = res[0]
        if last:
            y_final = res[1]

    y_prompt = y_final[n_prompt - seq:n_prompt].reshape(bp, seq, d)
    y_sample = y_final[n_prompt:].reshape(bs, dec, d)
    ret_prompt = jnp.stack([s[0:1] for s in ret_states]).astype(x_prompt.dtype)
    hgrn_prompt = jnp.stack([s[0:1] for s in hg_states]).astype(x_prompt.dtype)
    ret_sample = jnp.stack([s[1:] for s in ret_states]).astype(x_sample.dtype)
    hgrn_sample = jnp.stack([s[1:] for s in hg_states]).astype(x_sample.dtype)
    return (y_prompt, y_sample, ret_prompt, hgrn_prompt, ret_sample, hgrn_sample)
```

```python
import functools

import numpy as np
import jax
import jax.numpy as jnp
from jax import lax
from jax.experimental import pallas as pl
from jax.experimental.pallas import tpu as pltpu

F32 = jnp.float32
BF16 = jnp.bfloat16

D_MODEL = 1024
CHUNK = 64
N_META = 16
RET_HEADS = 4
RET_DK = 256
RET_DV = 512
HG_HEADS = 8
HG_DK = 128
HG_DV = 128
N_GROUPS = 4
EXPERTS_PER_GROUP = 8
N_EXPERTS = 32
D_EXPERT = 512
ROPE_BASE = 10000.0
EPS = 1e-6
PAST_LEN = 4096
LOG2_E = 1.4426950408889634

C_RQ, C_RK, C_RV, C_RG = 0, 1024, 2048, 4096
C_HQ, C_HF, C_HI, C_HG = 6144, 7168, 8192, 9216
C_GATES = 10240
PROJ_WIDTH = 12288
MIX_WIDTH = C_GATES

TOKEN_TILE = 256
PROJ_TN = 2048
PROJ_TM = 768
ROW_BLOCK = 128
SUB = 16
LANES = 128
TILE_ROWS = 8
VMEM_LIMIT = 56 * 1024 * 1024


def _sigmoid(x):
    return 1.0 / (1.0 + jnp.exp(-x))


def _dot(a, b):
    return jnp.dot(a, b, preferred_element_type=F32)


def _dot_nt(a, b):
    return lax.dot_general(a, b, (((1,), (1,)), ((), ())), preferred_element_type=F32)


def _dot_tn(a, b):
    return lax.dot_general(a, b, (((0,), (0,)), ((), ())), preferred_element_type=F32)


def _proj_kernel(x_ref, g_ref, wqk_ref, w_ref, o_ref, wbf_ref):
    j = pl.program_id(0)
    i = pl.program_id(1)

    @pl.when((i == 0) & (j == 0))
    def _():
        wbf_ref[...] = wqk_ref[...].astype(BF16)

    @pl.when((i == 0) & (j > 0))
    def _():
        wbf_ref[...] = w_ref[...].astype(BF16)

    x = x_ref[...]
    xn = x * lax.rsqrt(jnp.mean(x * x, axis=-1, keepdims=True) + EPS) * g_ref[...]
    o_ref[...] = _dot(xn.astype(BF16), wbf_ref[...])


def _proj(x, gain, w_qk, w_in):
    t = x.shape[0]
    tm, tn = PROJ_TM, PROJ_TN
    assert t % tm == 0
    return pl.pallas_call(
        _proj_kernel,
        grid=(PROJ_WIDTH // tn, t // tm),
        in_specs=[
            pl.BlockSpec((tm, D_MODEL), lambda j, i: (i, 0)),
            pl.BlockSpec((1, D_MODEL), lambda j, i: (0, 0)),
            pl.BlockSpec((D_MODEL, tn), lambda j, i: (0, 0)),
            pl.BlockSpec((D_MODEL, tn), lambda j, i: (0, j)),
        ],
        out_specs=pl.BlockSpec((tm, tn), lambda j, i: (i, j)),
        out_shape=jax.ShapeDtypeStruct((t, PROJ_WIDTH), F32),
        scratch_shapes=[pltpu.VMEM((D_MODEL, tn), BF16)],
        compiler_params=pltpu.CompilerParams(
            dimension_semantics=("arbitrary", "arbitrary"), vmem_limit_bytes=VMEM_LIMIT),
        name="proj",
    )(x, gain, w_qk, w_in)


def _mixer_kernel(n_prompt_chunks,
                  p_ref, cos_ref, sin_ref, dint_ref, qdec_ref, kdec_ref, cdec_ref,
                  lbt_ref, hgain_ref, sr_in_ref, sh_in_ref,
                  ro_ref, ho_ref, sr_out_ref, sh_out_ref,
                  sr_scr, sh_scr, hq_scr, hb_scr, hk_scr):
    c = pl.program_id(0)
    half = RET_DK // 2

    @pl.when(c == 0)
    def _():
        sr_scr[...] = jnp.zeros_like(sr_scr)
        sh_scr[...] = jnp.zeros_like(sh_scr)

    @pl.when(c >= n_prompt_chunks)
    def _():
        sr_scr[...] = sr_in_ref[0]
        for h in range(HG_HEADS):
            sh_scr[h] = sh_in_ref[0, h].T

    cos = cos_ref[...]
    sin = sin_ref[...]
    scores_v, to_state = [], []
    for h in range(RET_HEADS):
        q = p_ref[:, C_RQ + h * RET_DK:C_RQ + (h + 1) * RET_DK]
        q1, q2 = q[:, :half], q[:, half:]
        qr = jnp.concatenate([q1 * cos - q2 * sin, q1 * sin + q2 * cos], axis=1)
        to_state.append((qr * qdec_ref[:, h * RET_DK:(h + 1) * RET_DK]).astype(BF16))
        scores_v.append(qr.astype(BF16))
    for h in range(RET_HEADS):
        k = p_ref[:, C_RK + h * RET_DK:C_RK + (h + 1) * RET_DK]
        k1, k2 = k[:, :half], k[:, half:]
        kr = jnp.concatenate([k1 * cos - k2 * sin, k1 * sin + k2 * cos], axis=1) * (RET_DK ** -0.5)
        to_state.append((kr * kdec_ref[:, h * RET_DK:(h + 1) * RET_DK]).astype(BF16))
        scores_v[h] = _dot_nt(scores_v[h], kr.astype(BF16)) * dint_ref[h]
    pr = lax.broadcasted_iota(jnp.int32, (RET_DK, RET_DK), 0)
    pc = lax.broadcasted_iota(jnp.int32, (RET_DK, RET_DK), 1)
    src_lane = jnp.where(pr < half, 2 * pr, 2 * (pr - half) + 1)
    perm = jnp.where(pc == src_lane, 1.0, 0.0).astype(BF16)
    natural = _dot(jnp.concatenate(to_state, axis=0), perm).astype(BF16)
    for h in range(RET_HEADS):
        v = p_ref[:, C_RV + h * RET_DV:C_RV + (h + 1) * RET_DV].astype(BF16)
        s = sr_scr[h]
        qd = natural[h * CHUNK:(h + 1) * CHUNK]
        kd = natural[(RET_HEADS + h) * CHUNK:(RET_HEADS + h + 1) * CHUNK]
        o = _dot(scores_v[h].astype(BF16), v) + _dot(qd, s.astype(BF16))
        sr_scr[h] = cdec_ref[h] * s + _dot_tn(kd, v)
        on = o * lax.rsqrt(jnp.mean(o * o, axis=-1, keepdims=True) + EPS)
        g = p_ref[:, C_RG + h * RET_DV:C_RG + (h + 1) * RET_DV]
        ro_ref[:, h * RET_DV:(h + 1) * RET_DV] = (on * (g * _sigmoid(g))).astype(BF16)

    hf = p_ref[:, C_HF:C_HF + D_MODEL]
    log_lb = lbt_ref[0:1, :]
    log1m_lb = lbt_ref[1:2, :]
    one_m_lb = lbt_ref[2:3, :]
    log_sig = jnp.minimum(hf, 0.0) - jnp.log1p(jnp.exp(-jnp.abs(hf)))
    b_ = log1m_lb + log_sig
    log_f = jnp.maximum(log_lb, b_) + jnp.log1p(jnp.exp(-jnp.abs(log_lb - b_)))
    k_in = one_m_lb * _sigmoid(-hf)
    hq = p_ref[:, C_HQ:C_HQ + D_MODEL]
    q_all = hq * _sigmoid(hq)

    ri = lax.broadcasted_iota(jnp.int32, (CHUNK, CHUNK), 0)
    ci = lax.broadcasted_iota(jnp.int32, (CHUNK, CHUNK), 1)
    tri = jnp.where(ri >= ci, 1.0, 0.0).astype(BF16)
    f_hi = log_f.astype(BF16)
    r1 = log_f - f_hi.astype(F32)
    f_mid = r1.astype(BF16)
    f_lo = (r1 - f_mid.astype(F32)).astype(BF16)
    hb_scr[...] = (_dot(tri, f_hi) + _dot(tri, f_mid) + _dot(tri, f_lo)) * LOG2_E
    hq_scr[...] = q_all
    hk_scr[...] = k_in

    n_sub = CHUNK // SUB
    row = lax.broadcasted_iota(jnp.int32, (CHUNK, HG_DK), 0)
    col_sub = lax.broadcasted_iota(jnp.int32, (SUB, CHUNK), 1)
    hgain = hgain_ref[...]

    for h in range(HG_HEADS):
        sl = slice(h * HG_DK, (h + 1) * HG_DK)
        q = hq_scr[:, sl]
        b = hb_scr[:, sl]
        k = hk_scr[:, sl]
        vb = p_ref[:, C_HI + h * HG_DV:C_HI + (h + 1) * HG_DV].astype(BF16)
        st = sh_scr[h]
        o = _dot_nt((q * jnp.exp2(b)).astype(BF16), st.astype(BF16))
        b_last = b[CHUNK - 1:CHUNK, :]
        kd = (k * jnp.exp2(b_last - b)).astype(BF16)
        sh_scr[h] = st * jnp.exp2(b_last) + _dot_tn(vb, kd)
        b_end = [b[SUB * (jb + 1) - 1:SUB * (jb + 1), :] for jb in range(n_sub)]
        b_end_rows = jnp.concatenate(
            [jnp.broadcast_to(b_end[jb], (SUB, HG_DK)) for jb in range(n_sub)], axis=0)
        k_hat = k * jnp.exp2(b_end_rows - b)
        lhs, rhs = [], []
        for jb in range(n_sub - 1):
            q_hat = q * jnp.exp2(b - b_end[jb])
            lhs.append(jnp.where(row >= SUB * (jb + 1), q_hat, 0.0).astype(BF16))
            rhs.append(jnp.where((row >= SUB * jb) & (row < SUB * (jb + 1)), k_hat, 0.0).astype(BF16))
        a_off = _dot_nt(jnp.concatenate(lhs, axis=1), jnp.concatenate(rhs, axis=1))
        diag = []
        for ib in range(n_sub):
            r0 = SUB * ib
            b_i = b[r0:r0 + SUB, :]
            q_i = q[r0:r0 + SUB, :]
            blk = jnp.zeros((SUB, CHUNK), F32)
            for j in range(SUB):
                b_j = b[r0 + j:r0 + j + 1, :]
                k_j = k[r0 + j:r0 + j + 1, :]
                sc = jnp.sum(jnp.exp2(b_i - b_j) * q_i * k_j, axis=-1, keepdims=True)
                blk = jnp.where(col_sub == r0 + j, sc, blk)
            diag.append(blk)
        a_diag = jnp.where(ri >= ci, jnp.concatenate(diag, axis=0), 0.0)
        o = o + _dot((a_off + a_diag).astype(BF16), vb)
        on = o * lax.rsqrt(jnp.mean(o * o, axis=-1, keepdims=True) + EPS) * hgain
        g = p_ref[:, C_HG + h * HG_DV:C_HG + (h + 1) * HG_DV]
        ho_ref[:, h * HG_DV:(h + 1) * HG_DV] = (on * (g * _sigmoid(g))).astype(BF16)

    @pl.when(c >= n_prompt_chunks - 1)
    def _():
        sr_out_ref[0] = sr_scr[...]
        for h in range(HG_HEADS):
            sh_out_ref[0, h] = sh_scr[h].T


def _mixer(p, tables, lbt, hgain, state_ret, state_hg, n_prompt_chunks):
    t = p.shape[0]
    n_chunks = t // CHUNK
    n_streams = state_ret.shape[0]
    n_seq = n_streams + 1
    npc = n_prompt_chunks
    cos, sin, dint, qdec, kdec, cdec = tables

    def stream_of(c):
        return jnp.maximum(c - npc, 0)

    def seq_of(c):
        return jnp.maximum(c - npc + 1, 0)

    const2 = lambda c: (0, 0)
    return pl.pallas_call(
        functools.partial(_mixer_kernel, npc),
        grid=(n_chunks,),
        in_specs=[
            pl.BlockSpec((CHUNK, MIX_WIDTH), lambda c: (c, 0)),
            pl.BlockSpec((CHUNK, LANES), lambda c: (jnp.minimum(c, npc), 0)),
            pl.BlockSpec((CHUNK, LANES), lambda c: (jnp.minimum(c, npc), 0)),
            pl.BlockSpec((RET_HEADS, CHUNK, CHUNK), lambda c: (0, 0, 0)),
            pl.BlockSpec((CHUNK, RET_HEADS * RET_DK), const2),
            pl.BlockSpec((CHUNK, RET_HEADS * RET_DK), const2),
            pl.BlockSpec(memory_space=pltpu.SMEM),
            pl.BlockSpec((8, D_MODEL), const2),
            pl.BlockSpec((1, HG_DV), const2),
            pl.BlockSpec((1, RET_HEADS, RET_DK, RET_DV), lambda c: (stream_of(c), 0, 0, 0)),
            pl.BlockSpec((1, HG_HEADS, HG_DK, HG_DV), lambda c: (stream_of(c), 0, 0, 0)),
        ],
        out_specs=[
            pl.BlockSpec((CHUNK, RET_HEADS * RET_DV), lambda c: (c, 0)),
            pl.BlockSpec((CHUNK, HG_HEADS * HG_DV), lambda c: (c, 0)),
            pl.BlockSpec((1, RET_HEADS, RET_DK, RET_DV), lambda c: (seq_of(c), 0, 0, 0)),
            pl.BlockSpec((1, HG_HEADS, HG_DK, HG_DV), lambda c: (seq_of(c), 0, 0, 0)),
        ],
        out_shape=[
            jax.ShapeDtypeStruct((t, RET_HEADS * RET_DV), BF16),
            jax.ShapeDtypeStruct((t, HG_HEADS * HG_DV), BF16),
            jax.ShapeDtypeStruct((n_seq, RET_HEADS, RET_DK, RET_DV), F32),
            jax.ShapeDtypeStruct((n_seq, HG_HEADS, HG_DK, HG_DV), F32),
        ],
        scratch_shapes=[
            pltpu.VMEM((RET_HEADS, RET_DK, RET_DV), F32),
            pltpu.VMEM((HG_HEADS, HG_DV, HG_DK), F32),
            pltpu.VMEM((CHUNK, HG_HEADS * HG_DK), F32),
            pltpu.VMEM((CHUNK, HG_HEADS * HG_DK), F32),
            pltpu.VMEM((CHUNK, HG_HEADS * HG_DK), F32),
        ],
        compiler_params=pltpu.CompilerParams(
            dimension_semantics=("arbitrary",), vmem_limit_bytes=VMEM_LIMIT),
        name="mixer",
    )(p, cos, sin, dint, qdec, kdec, cdec, lbt, hgain, state_ret, state_hg)


def _post_kernel(gp_ref, ro_ref, ho_ref, x_ref, wr_ref, wh_ref, wo_ref, fg_ref,
                 rwh_ref, rwl_ref, x1_ref, h2_ref, route_ref, cnt_ref, run_scr):
    i = pl.program_id(0)
    tm = x_ref.shape[0]

    @pl.when(i == 0)
    def _():
        run_scr[...] = jnp.zeros_like(run_scr)

    gate_r = _sigmoid(gp_ref[:, 0:D_MODEL])
    gate_h = _sigmoid(gp_ref[:, D_MODEL:2 * D_MODEL])
    merged = gate_r * _dot(ro_ref[...], wr_ref[...]) + gate_h * _dot(ho_ref[...], wh_ref[...])
    x1 = x_ref[...] + _dot(merged.astype(BF16), wo_ref[...])
    x1_ref[...] = x1
    h2 = x1 * lax.rsqrt(jnp.mean(x1 * x1, axis=-1, keepdims=True) + EPS) * fg_ref[...]
    for kk in range(TILE_ROWS):
        h2_ref[:, kk, :] = h2[:, kk * LANES:(kk + 1) * LANES]

    h_hi = h2.astype(BF16)
    h_lo = (h2 - h_hi.astype(F32)).astype(BF16)
    logits = _dot(h_hi, rwh_ref[...]) + (_dot(h_hi, rwl_ref[...]) + _dot(h_lo, rwh_ref[...]))

    lane = lax.broadcasted_iota(jnp.int32, (tm, LANES), 1)
    lanef = lane.astype(F32)
    neg_inf = jnp.float32(-jnp.inf)
    big = jnp.float32(1e9)
    is_g = (lane >= N_EXPERTS) & (lane < N_EXPERTS + N_GROUPS)
    gl = jnp.where(is_g, logits, neg_inf)
    gmax = jnp.max(gl, axis=-1, keepdims=True)
    gidx = jnp.min(jnp.where(gl == gmax, lanef, big), axis=-1, keepdims=True) - N_EXPERTS
    g_w = 1.0 / jnp.sum(jnp.exp(gl - gmax), axis=-1, keepdims=True)
    lane_group = jnp.floor(lanef * (1.0 / EXPERTS_PER_GROUP))
    in_group = (lane < N_EXPERTS) & (lane_group == gidx)
    el = jnp.where(in_group, logits, neg_inf)
    m1 = jnp.max(el, axis=-1, keepdims=True)
    i1 = jnp.min(jnp.where(el == m1, lanef, big), axis=-1, keepdims=True)
    el2 = jnp.where(lanef == i1, neg_inf, el)
    m2 = jnp.max(el2, axis=-1, keepdims=True)
    i2 = jnp.min(jnp.where(el2 == m2, lanef, big), axis=-1, keepdims=True)
    tt = jnp.exp(m2 - m1)
    w1 = g_w / (1.0 + tt)
    w2 = g_w * tt / (1.0 + tt)

    oh1 = lanef == i1
    oh2 = lanef == i2
    e_cnt = jnp.where(oh1, 1.0, 0.0) + jnp.where(oh2, 1.0, 0.0)
    ri = lax.broadcasted_iota(jnp.int32, (tm, tm), 0)
    ci = lax.broadcasted_iota(jnp.int32, (tm, tm), 1)
    strict = jnp.where(ri > ci, 1.0, 0.0).astype(BF16)
    prefix = _dot(strict, e_cnt.astype(BF16)) + run_scr[0:1, :]
    r1 = jnp.sum(jnp.where(oh1, prefix, 0.0), axis=-1, keepdims=True)
    r2 = jnp.sum(jnp.where(oh2, prefix, 0.0), axis=-1, keepdims=True)
    run_scr[0:1, :] = run_scr[0:1, :] + jnp.sum(e_cnt, axis=0, keepdims=True)
    cnt_ref[...] = run_scr[...]

    route = jnp.where(lane == 0, i1, 0.0)
    route = jnp.where(lane == 1, i2, route)
    route = jnp.where(lane == 2, w1, route)
    route = jnp.where(lane == 3, w2, route)
    route = jnp.where(lane == 4, r1, route)
    route = jnp.where(lane == 5, r2, route)
    route_ref[...] = route


def _post(p, ro, ho, x, wr, wh, wo, fgain, rw_hi, rw_lo):
    t = x.shape[0]
    tm = TOKEN_TILE
    gate_block = C_GATES // (2 * D_MODEL)
    const2 = lambda i: (0, 0)
    return pl.pallas_call(
        _post_kernel,
        grid=(t // tm,),
        in_specs=[
            pl.BlockSpec((tm, 2 * D_MODEL), lambda i: (i, gate_block)),
            pl.BlockSpec((tm, RET_HEADS * RET_DV), lambda i: (i, 0)),
            pl.BlockSpec((tm, HG_HEADS * HG_DV), lambda i: (i, 0)),
            pl.BlockSpec((tm, D_MODEL), lambda i: (i, 0)),
            pl.BlockSpec((RET_HEADS * RET_DV, D_MODEL), const2),
            pl.BlockSpec((HG_HEADS * HG_DV, D_MODEL), const2),
            pl.BlockSpec((D_MODEL, D_MODEL), const2),
            pl.BlockSpec((1, D_MODEL), const2),
            pl.BlockSpec((D_MODEL, LANES), const2),
            pl.BlockSpec((D_MODEL, LANES), const2),
        ],
        out_specs=[
            pl.BlockSpec((tm, D_MODEL), lambda i: (i, 0)),
            pl.BlockSpec((tm, TILE_ROWS, LANES), lambda i: (i, 0, 0)),
            pl.BlockSpec((tm, LANES), lambda i: (i, 0)),
            pl.BlockSpec((8, LANES), const2),
        ],
        out_shape=[
            jax.ShapeDtypeStruct((t, D_MODEL), F32),
            jax.ShapeDtypeStruct((t, TILE_ROWS, LANES), F32),
            jax.ShapeDtypeStruct((t, LANES), F32),
            jax.ShapeDtypeStruct((8, LANES), F32),
        ],
        scratch_shapes=[pltpu.VMEM((8, LANES), F32)],
        compiler_params=pltpu.CompilerParams(
            dimension_semantics=("arbitrary",), vmem_limit_bytes=VMEM_LIMIT),
        name="post",
    )(p, ro, ho, x, wr, wh, wo, fgain, rw_hi, rw_lo)


def _plan_kernel(pairs_per_step, n_blocks,
                 codes_ref, cnt_ref, slot_ref, bexp_ref, nused_ref, pstart_ref):
    step = pl.program_id(0)
    base = step * pairs_per_step

    @pl.when(step == 0)
    def _():
        def per_expert(e, carry):
            start, last_e = carry
            pstart_ref[e] = start
            nb = (cnt_ref[e] + (ROW_BLOCK - 1)) // ROW_BLOCK
            first = start // ROW_BLOCK

            def fill(bb, _):
                bexp_ref[first + bb] = e
                return 0

            lax.fori_loop(0, nb, fill, 0)
            return start + nb * ROW_BLOCK, jnp.where(nb > 0, e, last_e)

        total, last_e = lax.fori_loop(0, N_EXPERTS, per_expert, (jnp.int32(0), jnp.int32(0)))
        n_used = total // ROW_BLOCK
        nused_ref[0] = n_used

        def fill_tail(bb, _):
            bexp_ref[bb] = last_e
            return 0

        lax.fori_loop(n_used, n_blocks, fill_tail, 0)

        def init(r, _):
            slot_ref[r] = 0
            return 0

        lax.fori_loop(0, n_blocks * ROW_BLOCK, init, 0, unroll=16)

    def place(i, _):
        code = codes_ref[base + i]
        slot_ref[pstart_ref[code >> 16] + (code & 0xFFFF)] = (base + i) >> 1
        return 0

    lax.fori_loop(0, pairs_per_step, place, 0, unroll=8)


def _plan(codes, counts, n_blocks):
    n_pairs = codes.shape[0]
    pairs_per_step = 2 * TOKEN_TILE
    assert n_pairs % pairs_per_step == 0
    smem = pl.BlockSpec(memory_space=pltpu.SMEM)
    return pl.pallas_call(
        functools.partial(_plan_kernel, pairs_per_step, n_blocks),
        grid=(n_pairs // pairs_per_step,),
        in_specs=[smem, smem],
        out_specs=[smem, smem, smem, smem],
        out_shape=[
            jax.ShapeDtypeStruct((n_blocks * ROW_BLOCK,), jnp.int32),
            jax.ShapeDtypeStruct((n_blocks,), jnp.int32),
            jax.ShapeDtypeStruct((1,), jnp.int32),
            jax.ShapeDtypeStruct((N_EXPERTS,), jnp.int32),
        ],
        compiler_params=pltpu.CompilerParams(dimension_semantics=("arbitrary",)),
        name="plan",
    )(codes, counts)


def _expert_kernel(bexp_ref, nused_ref, slot_ref, h2_hbm, wg_ref, wu_ref, wd_ref, yb_ref,
                   xbuf, sems, wg_s, wu_s, wd_s):
    b = pl.program_id(0)
    n_used = nused_ref[0]

    def row_copy(tok, buf_slot, r):
        return pltpu.make_async_copy(h2_hbm.at[tok], xbuf.at[buf_slot, r], sems.at[buf_slot])

    def gather(block):
        buf_slot = block % 2

        def issue(r, _):
            row_copy(slot_ref[block * ROW_BLOCK + r], buf_slot, r).start()
            return 0

        lax.fori_loop(0, ROW_BLOCK, issue, 0, unroll=8)

    @pl.when((b == 0) & (n_used > 0))
    def _():
        gather(b)

    @pl.when(b + 1 < n_used)
    def _():
        gather(b + 1)

    prev = bexp_ref[jnp.maximum(b - 1, 0)]

    @pl.when((b == 0) | (bexp_ref[b] != prev))
    def _():
        wg_s[...] = wg_ref[0].astype(BF16)
        wu_s[...] = wu_ref[0].astype(BF16)
        wd_s[...] = wd_ref[0].astype(BF16)

    @pl.when(b < n_used)
    def _():
        buf_slot = b % 2
        pltpu.make_async_copy(h2_hbm.at[pl.ds(0, ROW_BLOCK)], xbuf.at[buf_slot], sems.at[buf_slot]).wait()
        x = jnp.concatenate([xbuf[buf_slot, :, kk, :] for kk in range(TILE_ROWS)], axis=1).astype(BF16)
        g = _dot(x, wg_s[...])
        u = _dot(x, wu_s[...])
        a = (g * _sigmoid(g) * u).astype(BF16)
        y = _dot(a, wd_s[...])
        for kk in range(TILE_ROWS):
            yb_ref[:, kk, :] = y[:, kk * LANES:(kk + 1) * LANES]

    @pl.when(b >= n_used)
    def _():
        yb_ref[...] = jnp.zeros_like(yb_ref)


def _experts(bexp, nused, slot_tok, h2, w_gate, w_up, w_down):
    n_blocks = bexp.shape[0]
    wmap = lambda b, be, nu, st: (be[b], 0, 0)
    return pl.pallas_call(
        _expert_kernel,
        grid_spec=pltpu.PrefetchScalarGridSpec(
            num_scalar_prefetch=3,
            grid=(n_blocks,),
            in_specs=[
                pl.BlockSpec(memory_space=pl.ANY),
                pl.BlockSpec((1, D_MODEL, D_EXPERT), wmap),
                pl.BlockSpec((1, D_MODEL, D_EXPERT), wmap),
                pl.BlockSpec((1, D_EXPERT, D_MODEL), wmap),
            ],
            out_specs=pl.BlockSpec((ROW_BLOCK, TILE_ROWS, LANES), lambda b, be, nu, st: (b, 0, 0)),
            scratch_shapes=[
                pltpu.VMEM((2, ROW_BLOCK, TILE_ROWS, LANES), F32),
                pltpu.SemaphoreType.DMA((2,)),
                pltpu.VMEM((D_MODEL, D_EXPERT), BF16),
                pltpu.VMEM((D_MODEL, D_EXPERT), BF16),
                pltpu.VMEM((D_EXPERT, D_MODEL), BF16),
            ],
        ),
        out_shape=jax.ShapeDtypeStruct((n_blocks * ROW_BLOCK, TILE_ROWS, LANES), F32),
        compiler_params=pltpu.CompilerParams(
            dimension_semantics=("arbitrary",), vmem_limit_bytes=VMEM_LIMIT),
        name="experts",
    )(bexp, nused, slot_tok, h2, w_gate, w_up, w_down)


def _combine_kernel(with_norm, codes_ref, pstart_ref, route_ref, x1_ref, yb_hbm, fg_ref, *rest):
    if with_norm:
        x2_ref, yn_ref, buf, sems = rest
    else:
        x2_ref, buf, sems = rest
    i = pl.program_id(0)
    tm = x1_ref.shape[0]

    def row_copy(d, buf_slot, r):
        return pltpu.make_async_copy(yb_hbm.at[d], buf.at[buf_slot, r], sems.at[buf_slot])

    def dest_of(pair):
        code = codes_ref[pair]
        return pstart_ref[code >> 16] + (code & 0xFFFF)

    def gather(tile):
        buf_slot = tile % 2
        base = tile * tm

        def issue(tt, _):
            row_copy(dest_of(2 * (base + tt)), buf_slot, tt).start()
            row_copy(dest_of(2 * (base + tt) + 1), buf_slot, tm + tt).start()
            return 0

        lax.fori_loop(0, tm, issue, 0, unroll=4)

    @pl.when(i == 0)
    def _():
        gather(i)

    @pl.when(i + 1 < pl.num_programs(0))
    def _():
        gather(i + 1)

    buf_slot = i % 2
    pltpu.make_async_copy(yb_hbm.at[pl.ds(0, 2 * tm)], buf.at[buf_slot], sems.at[buf_slot]).wait()

    w1 = route_ref[:, 2:3]
    w2 = route_ref[:, 3:4]
    parts = []
    for kk in range(TILE_ROWS):
        y = w1 * buf[buf_slot, 0:tm, kk, :] + w2 * buf[buf_slot, tm:2 * tm, kk, :]
        parts.append(x1_ref[:, kk * LANES:(kk + 1) * LANES] + y)
    x2 = jnp.concatenate(parts, axis=1)
    x2_ref[...] = x2
    if with_norm:
        yn_ref[...] = x2 * lax.rsqrt(jnp.mean(x2 * x2, axis=-1, keepdims=True) + EPS) * fg_ref[...]


def _combine(codes, pstart, route, x1, yb, fgain, with_norm):
    t = x1.shape[0]
    tm = ROW_BLOCK
    row_spec = pl.BlockSpec((tm, D_MODEL), lambda i, cd, ps: (i, 0))
    out_specs = [row_spec, row_spec] if with_norm else [row_spec]
    out_shape = [jax.ShapeDtypeStruct((t, D_MODEL), F32)] * (2 if with_norm else 1)
    return pl.pallas_call(
        functools.partial(_combine_kernel, with_norm),
        grid_spec=pltpu.PrefetchScalarGridSpec(
            num_scalar_prefetch=2,
            grid=(t // tm,),
            in_specs=[
                pl.BlockSpec((tm, LANES), lambda i, cd, ps: (i, 0)),
                row_spec,
                pl.BlockSpec(memory_space=pl.ANY),
                pl.BlockSpec((1, D_MODEL), lambda i, cd, ps: (0, 0)),
            ],
            out_specs=out_specs,
            scratch_shapes=[
                pltpu.VMEM((2, 2 * tm, TILE_ROWS, LANES), F32),
                pltpu.SemaphoreType.DMA((2,)),
            ],
        ),
        out_shape=out_shape,
        compiler_params=pltpu.CompilerParams(
            dimension_semantics=("arbitrary",), vmem_limit_bytes=VMEM_LIMIT),
        name="combine",
    )(codes, pstart, route, x1, yb, fgain)


def _mixer_tables(n_prompt_chunks, prompt_pos0):
    half = RET_DK // 2
    inv_freq = 1.0 / (ROPE_BASE ** jnp.linspace(0.0, 1.0, half, dtype=F32))
    pos_prompt = prompt_pos0 + jnp.arange(n_prompt_chunks * CHUNK)
    pos_sample = PAST_LEN + jnp.arange(CHUNK)
    pos = jnp.concatenate([pos_prompt, pos_sample]).astype(F32)
    ang = pos[:, None] * inv_freq[None, :]
    cos, sin = jnp.cos(ang), jnp.sin(ang)
    log_gamma = jnp.log1p(-jnp.exp2(-5.0 - jnp.arange(RET_HEADS, dtype=F32)))
    idx = jnp.arange(CHUNK, dtype=F32)
    rel = idx[:, None] - idx[None, :]
    causal = rel >= 0
    dint = jnp.where(causal, jnp.exp(log_gamma[:, None, None] * jnp.where(causal, rel, 0.0)), 0.0)
    qdec = jnp.exp(log_gamma[None, :] * (idx[:, None] + 1.0))
    kdec = jnp.exp(log_gamma[None, :] * (CHUNK - 1.0 - idx[:, None]))
    cdec = jnp.exp(log_gamma * CHUNK)
    qdec = jnp.repeat(qdec, RET_DK, axis=1)
    kdec = jnp.repeat(kdec, RET_DK, axis=1)
    return cos, sin, dint, qdec, kdec, cdec


def _deinterleave_qk(w_in_l):
    w = w_in_l[:, :C_RV].reshape(D_MODEL, 2 * RET_HEADS, RET_DK // 2, 2)
    return jnp.swapaxes(w, 2, 3).reshape(D_MODEL, C_RV)


def _router_split(router_group_l, router_expert_l):
    rw = jnp.concatenate(
        [router_expert_l, router_group_l,
         jnp.zeros((D_MODEL, LANES - N_EXPERTS - N_GROUPS), F32)], axis=1)
    hi = rw.astype(BF16)
    lo = (rw - hi.astype(F32)).astype(BF16)
    return hi, lo


def kernel(x_prompt, x_sample, state_ret, state_hgrn, meta_tokens, mix_norm, w_in, hg_lb_logits, hg_norm,
           w_ret_branch, w_hg_branch, w_out, ffn_norm, router_group, router_expert, w_gate, w_up, w_down,
           final_norm):
    depth = w_in.shape[0]
    bp, seq, d = x_prompt.shape
    bs, dec, _ = x_sample.shape
    assert bp == 1 and dec == CHUNK and d == D_MODEL and seq % CHUNK == 0

    n_real = N_META + seq
    n_sample = bs * dec
    n_pad = (-(n_real + n_sample)) % PROJ_TM
    while (n_pad + n_real) % CHUNK:
        n_pad += PROJ_TM
    n_prompt = n_pad + n_real
    assert n_pad > 0
    t = n_prompt + n_sample
    npc = n_prompt // CHUNK
    n_blocks = -(-(2 * t) // ROW_BLOCK) + N_EXPERTS

    x = jnp.concatenate([
        jnp.zeros((n_pad, d), F32),
        meta_tokens.astype(F32),
        x_prompt[0],
        x_sample.reshape(n_sample, d),
    ], axis=0)

    tables = _mixer_tables(npc, -(n_pad + N_META))

    prob = jax.nn.softmax(hg_lb_logits.astype(F32), axis=0)
    cum = jnp.cumsum(prob, axis=0)
    lb_all = cum - cum[0:1]

    ret_states, hg_states = [], []
    y_final = None
    for l in range(depth):
        lb = lb_all[l]
        lbt = jnp.concatenate(
            [jnp.log(lb)[None], jnp.log1p(-lb)[None], (1.0 - lb)[None], jnp.zeros((5, d), F32)], axis=0)
        p = _proj(x, mix_norm[l][None], _deinterleave_qk(w_in[l]), w_in[l])
        ro, ho, s_ret, s_hg = _mixer(p, tables, lbt, hg_norm[l][None].astype(F32),
                                     state_ret[l].astype(F32), state_hgrn[l].astype(F32), npc)
        ret_states.append(s_ret)
        hg_states.append(s_hg)
        rw_hi, rw_lo = _router_split(router_group[l], router_expert[l])
        x1, h2, route, counts = _post(
            p, ro, ho, x, w_ret_branch[l].astype(BF16), w_hg_branch[l].astype(BF16),
            w_out[l].astype(BF16), ffn_norm[l][None], rw_hi, rw_lo)
        ids = route[:, 0:2].astype(jnp.int32)
        ranks = route[:, 4:6].astype(jnp.int32)
        codes = (ids * 65536 + ranks).reshape(-1)
        slot_tok, bexp, nused, pstart = _plan(codes, counts[0].astype(jnp.int32), n_blocks)
        yb = _experts(bexp, nused, slot_tok, h2, w_gate[l], w_up[l], w_down[l])
        last = l == depth - 1
        res = _combine(codes, pstart, route, x1, yb, final_norm[None], last)
        x = res[0]
        if last:
            y_final = res[1]

    y_prompt = y_final[n_prompt - seq:n_prompt].reshape(bp, seq, d)
    y_sample = y_final[n_prompt:].reshape(bs, dec, d)
    ret_prompt = jnp.stack([s[0:1] for s in ret_states]).astype(x_prompt.dtype)
    hgrn_prompt = jnp.stack([s[0:1] for s in hg_states]).astype(x_prompt.dtype)
    ret_sample = jnp.stack([s[1:] for s in ret_states]).astype(x_sample.dtype)
    hgrn_sample = jnp.stack([s[1:] for s in hg_states]).astype(x_sample.dtype)
    return (y_prompt, y_sample, ret_prompt, hgrn_prompt, ret_sample, hgrn_sample)
```

```python
import functools

import numpy as np
import jax
import jax.numpy as jnp
from jax import lax
from jax.experimental import pallas as pl
from jax.experimental.pallas import tpu as pltpu

F32 = jnp.float32
BF16 = jnp.bfloat16

D_MODEL = 1024
CHUNK = 64
N_META = 16
RET_HEADS = 4
RET_DK = 256
RET_DV = 512
HG_HEADS = 8
HG_DK = 128
HG_DV = 128
N_GROUPS = 4
EXPERTS_PER_GROUP = 8
N_EXPERTS = 32
D_EXPERT = 512
ROPE_BASE = 10000.0
EPS = 1e-6
PAST_LEN = 4096
LOG2_E = 1.4426950408889634

C_RQ, C_RK, C_RV, C_RG = 0, 1024, 2048, 4096
C_HQ, C_HF, C_HI, C_HG = 6144, 7168, 8192, 9216
C_GATES = 10240
PROJ_WIDTH = 12288
MIX_WIDTH = C_GATES

TOKEN_TILE = 256
PROJ_TN = 2048
PROJ_TM = 768
ROW_BLOCK = 256
COMBINE_TILE = 128
SUB = 16
LANES = 128
TILE_ROWS = 8
VMEM_LIMIT = 56 * 1024 * 1024


def _sigmoid(x):
    return 1.0 / (1.0 + jnp.exp(-x))


def _dot(a, b):
    return jnp.dot(a, b, preferred_element_type=F32)


def _dot_nt(a, b):
    return lax.dot_general(a, b, (((1,), (1,)), ((), ())), preferred_element_type=F32)


def _dot_tn(a, b):
    return lax.dot_general(a, b, (((0,), (0,)), ((), ())), preferred_element_type=F32)


def _proj_kernel(x_ref, g_ref, wqk_ref, w_ref, o_ref, wbf_ref):
    j = pl.program_id(0)
    i = pl.program_id(1)

    @pl.when((i == 0) & (j == 0))
    def _():
        wbf_ref[...] = wqk_ref[...].astype(BF16)

    @pl.when((i == 0) & (j > 0))
    def _():
        wbf_ref[...] = w_ref[...].astype(BF16)

    x = x_ref[...]
    xn = x * lax.rsqrt(jnp.mean(x * x, axis=-1, keepdims=True) + EPS) * g_ref[...]
    o_ref[...] = _dot(xn.astype(BF16), wbf_ref[...])


def _proj(layer, x, gain, w_qk, w_in):
    t = x.shape[0]
    tm, tn = PROJ_TM, PROJ_TN
    assert t % tm == 0
    return pl.pallas_call(
        _proj_kernel,
        grid=(PROJ_WIDTH // tn, t // tm),
        in_specs=[
            pl.BlockSpec((tm, D_MODEL), lambda j, i: (i, 0)),
            pl.BlockSpec((1, D_MODEL), lambda j, i: (0, 0)),
            pl.BlockSpec((D_MODEL, tn), lambda j, i: (0, 0)),
            pl.BlockSpec((None, D_MODEL, tn), lambda j, i: (layer, 0, j)),
        ],
        out_specs=pl.BlockSpec((tm, tn), lambda j, i: (i, j)),
        out_shape=jax.ShapeDtypeStruct((t, PROJ_WIDTH), F32),
        scratch_shapes=[pltpu.VMEM((D_MODEL, tn), BF16)],
        compiler_params=pltpu.CompilerParams(
            dimension_semantics=("arbitrary", "arbitrary"), vmem_limit_bytes=VMEM_LIMIT),
        name="proj",
    )(x, gain, w_qk, w_in)


def _mixer_kernel(n_prompt_chunks,
                  p_ref, cos_ref, sin_ref, dint_ref, qdec_ref, kdec_ref, cdec_ref,
                  lbt_ref, hgain_ref, sr_in_ref, sh_in_ref, *rest):
    (ro_ref, ho_ref, srp_out_ref, shp_out_ref, srs_out_ref, shs_out_ref,
     sr_scr, sh_scr, hq_scr, hb_scr, hk_scr) = rest[4:]
    c = pl.program_id(0)
    half = RET_DK // 2

    @pl.when(c == 0)
    def _():
        sr_scr[...] = jnp.zeros_like(sr_scr)
        sh_scr[...] = jnp.zeros_like(sh_scr)

    @pl.when(c >= n_prompt_chunks)
    def _():
        sr_scr[...] = sr_in_ref[...]
        for h in range(HG_HEADS):
            sh_scr[h] = sh_in_ref[h].T

    cos = cos_ref[...]
    sin = sin_ref[...]
    scores_v, to_state = [], []
    for h in range(RET_HEADS):
        q = p_ref[:, C_RQ + h * RET_DK:C_RQ + (h + 1) * RET_DK]
        q1, q2 = q[:, :half], q[:, half:]
        qr = jnp.concatenate([q1 * cos - q2 * sin, q1 * sin + q2 * cos], axis=1)
        to_state.append((qr * qdec_ref[:, h * RET_DK:(h + 1) * RET_DK]).astype(BF16))
        scores_v.append(qr.astype(BF16))
    for h in range(RET_HEADS):
        k = p_ref[:, C_RK + h * RET_DK:C_RK + (h + 1) * RET_DK]
        k1, k2 = k[:, :half], k[:, half:]
        kr = jnp.concatenate([k1 * cos - k2 * sin, k1 * sin + k2 * cos], axis=1) * (RET_DK ** -0.5)
        to_state.append((kr * kdec_ref[:, h * RET_DK:(h + 1) * RET_DK]).astype(BF16))
        scores_v[h] = _dot_nt(scores_v[h], kr.astype(BF16)) * dint_ref[h]
    pr = lax.broadcasted_iota(jnp.int32, (RET_DK, RET_DK), 0)
    pc = lax.broadcasted_iota(jnp.int32, (RET_DK, RET_DK), 1)
    src_lane = jnp.where(pr < half, 2 * pr, 2 * (pr - half) + 1)
    perm = jnp.where(pc == src_lane, 1.0, 0.0).astype(BF16)
    natural = _dot(jnp.concatenate(to_state, axis=0), perm).astype(BF16)
    for h in range(RET_HEADS):
        v = p_ref[:, C_RV + h * RET_DV:C_RV + (h + 1) * RET_DV].astype(BF16)
        s = sr_scr[h]
        qd = natural[h * CHUNK:(h + 1) * CHUNK]
        kd = natural[(RET_HEADS + h) * CHUNK:(RET_HEADS + h + 1) * CHUNK]
        o = _dot(scores_v[h].astype(BF16), v) + _dot(qd, s.astype(BF16))
        sr_scr[h] = cdec_ref[h] * s + _dot_tn(kd, v)
        on = o * lax.rsqrt(jnp.mean(o * o, axis=-1, keepdims=True) + EPS)
        g = p_ref[:, C_RG + h * RET_DV:C_RG + (h + 1) * RET_DV]
        ro_ref[:, h * RET_DV:(h + 1) * RET_DV] = (on * (g * _sigmoid(g))).astype(BF16)

    hf = p_ref[:, C_HF:C_HF + D_MODEL]
    log_lb = lbt_ref[0:1, :]
    log1m_lb = lbt_ref[1:2, :]
    one_m_lb = lbt_ref[2:3, :]
    log_sig = jnp.minimum(hf, 0.0) - jnp.log1p(jnp.exp(-jnp.abs(hf)))
    b_ = log1m_lb + log_sig
    log_f = jnp.maximum(log_lb, b_) + jnp.log1p(jnp.exp(-jnp.abs(log_lb - b_)))
    k_in = one_m_lb * _sigmoid(-hf)
    hq = p_ref[:, C_HQ:C_HQ + D_MODEL]
    q_all = hq * _sigmoid(hq)

    ri = lax.broadcasted_iota(jnp.int32, (CHUNK, CHUNK), 0)
    ci = lax.broadcasted_iota(jnp.int32, (CHUNK, CHUNK), 1)
    tri = jnp.where(ri >= ci, 1.0, 0.0).astype(BF16)
    f_hi = log_f.astype(BF16)
    r1 = log_f - f_hi.astype(F32)
    f_mid = r1.astype(BF16)
    f_lo = (r1 - f_mid.astype(F32)).astype(BF16)
    hb_scr[...] = (_dot(tri, f_hi) + _dot(tri, f_mid) + _dot(tri, f_lo)) * LOG2_E
    hq_scr[...] = q_all
    hk_scr[...] = k_in

    n_sub = CHUNK // SUB
    row = lax.broadcasted_iota(jnp.int32, (CHUNK, HG_DK), 0)
    col_sub = lax.broadcasted_iota(jnp.int32, (SUB, CHUNK), 1)
    hgain = hgain_ref[...]

    for h in range(HG_HEADS):
        sl = slice(h * HG_DK, (h + 1) * HG_DK)
        q = hq_scr[:, sl]
        b = hb_scr[:, sl]
        k = hk_scr[:, sl]
        vb = p_ref[:, C_HI + h * HG_DV:C_HI + (h + 1) * HG_DV].astype(BF16)
        st = sh_scr[h]
        o = _dot_nt((q * jnp.exp2(b)).astype(BF16), st.astype(BF16))
        b_last = b[CHUNK - 1:CHUNK, :]
        kd = (k * jnp.exp2(b_last - b)).astype(BF16)
        sh_scr[h] = st * jnp.exp2(b_last) + _dot_tn(vb, kd)
        b_end = [b[SUB * (jb + 1) - 1:SUB * (jb + 1), :] for jb in range(n_sub)]
        b_end_rows = jnp.concatenate(
            [jnp.broadcast_to(b_end[jb], (SUB, HG_DK)) for jb in range(n_sub)], axis=0)
        k_hat = k * jnp.exp2(b_end_rows - b)
        lhs, rhs = [], []
        for jb in range(n_sub - 1):
            q_hat = q * jnp.exp2(b - b_end[jb])
            lhs.append(jnp.where(row >= SUB * (jb + 1), q_hat, 0.0).astype(BF16))
            rhs.append(jnp.where((row >= SUB * jb) & (row < SUB * (jb + 1)), k_hat, 0.0).astype(BF16))
        a_off = _dot_nt(jnp.concatenate(lhs, axis=1), jnp.concatenate(rhs, axis=1))
        diag = []
        for ib in range(n_sub):
            r0 = SUB * ib
            b_i = b[r0:r0 + SUB, :]
            q_i = q[r0:r0 + SUB, :]
            blk = jnp.zeros((SUB, CHUNK), F32)
            for j in range(SUB):
                b_j = b[r0 + j:r0 + j + 1, :]
                k_j = k[r0 + j:r0 + j + 1, :]
                sc = jnp.sum(jnp.exp2(b_i - b_j) * q_i * k_j, axis=-1, keepdims=True)
                blk = jnp.where(col_sub == r0 + j, sc, blk)
            diag.append(blk)
        a_diag = jnp.where(ri >= ci, jnp.concatenate(diag, axis=0), 0.0)
        o = o + _dot((a_off + a_diag).astype(BF16), vb)
        on = o * lax.rsqrt(jnp.mean(o * o, axis=-1, keepdims=True) + EPS) * hgain
        g = p_ref[:, C_HG + h * HG_DV:C_HG + (h + 1) * HG_DV]
        ho_ref[:, h * HG_DV:(h + 1) * HG_DV] = (on * (g * _sigmoid(g))).astype(BF16)

    @pl.when(c == n_prompt_chunks - 1)
    def _():
        srp_out_ref[...] = sr_scr[...]
        for h in range(HG_HEADS):
            shp_out_ref[h] = sh_scr[h].T

    @pl.when(c >= n_prompt_chunks)
    def _():
        srs_out_ref[...] = sr_scr[...]
        for h in range(HG_HEADS):
            shs_out_ref[h] = sh_scr[h].T


def _mixer(layer, p, tables, lbt, hgain, state_ret, state_hg, n_prompt_chunks, prev_states):
    t = p.shape[0]
    n_chunks = t // CHUNK
    depth, n_streams = state_ret.shape[0], state_ret.shape[1]
    npc = n_prompt_chunks
    cos, sin, dint, qdec, kdec, cdec = tables

    def stream_of(c):
        return jnp.maximum(c - npc, 0)

    const2 = lambda c: (0, 0)
    ret_blk = (None, None, RET_HEADS, RET_DK, RET_DV)
    hg_blk = (None, None, HG_HEADS, HG_DK, HG_DV)
    in_specs = [
        pl.BlockSpec((CHUNK, MIX_WIDTH), lambda c: (c, 0)),
        pl.BlockSpec((CHUNK, LANES), lambda c: (jnp.minimum(c, npc), 0)),
        pl.BlockSpec((CHUNK, LANES), lambda c: (jnp.minimum(c, npc), 0)),
        pl.BlockSpec((RET_HEADS, CHUNK, CHUNK), lambda c: (0, 0, 0)),
        pl.BlockSpec((CHUNK, RET_HEADS * RET_DK), const2),
        pl.BlockSpec((CHUNK, RET_HEADS * RET_DK), const2),
        pl.BlockSpec(memory_space=pltpu.SMEM),
        pl.BlockSpec((8, D_MODEL), const2),
        pl.BlockSpec((1, HG_DV), const2),
        pl.BlockSpec(ret_blk, lambda c: (layer, stream_of(c), 0, 0, 0)),
        pl.BlockSpec(hg_blk, lambda c: (layer, stream_of(c), 0, 0, 0)),
    ]
    args = [p, cos, sin, dint, qdec, kdec, cdec, lbt, hgain, state_ret, state_hg]
    aliases = {}
    for n, prev in enumerate(prev_states):
        aliases[len(args)] = 2 + n
        in_specs.append(pl.BlockSpec(memory_space=pl.ANY))
        args.append(prev)
    return pl.pallas_call(
        functools.partial(_mixer_kernel, npc),
        grid=(n_chunks,),
        in_specs=in_specs,
        out_specs=[
            pl.BlockSpec((CHUNK, RET_HEADS * RET_DV), lambda c: (c, 0)),
            pl.BlockSpec((CHUNK, HG_HEADS * HG_DV), lambda c: (c, 0)),
            pl.BlockSpec(ret_blk, lambda c: (layer, 0, 0, 0, 0)),
            pl.BlockSpec(hg_blk, lambda c: (layer, 0, 0, 0, 0)),
            pl.BlockSpec(ret_blk, lambda c: (layer, stream_of(c), 0, 0, 0)),
            pl.BlockSpec(hg_blk, lambda c: (layer, stream_of(c), 0, 0, 0)),
        ],
        out_shape=[
            jax.ShapeDtypeStruct((t, RET_HEADS * RET_DV), BF16),
            jax.ShapeDtypeStruct((t, HG_HEADS * HG_DV), BF16),
            jax.ShapeDtypeStruct((depth, 1, RET_HEADS, RET_DK, RET_DV), F32),
            jax.ShapeDtypeStruct((depth, 1, HG_HEADS, HG_DK, HG_DV), F32),
            jax.ShapeDtypeStruct((depth, n_streams, RET_HEADS, RET_DK, RET_DV), F32),
            jax.ShapeDtypeStruct((depth, n_streams, HG_HEADS, HG_DK, HG_DV), F32),
        ],
        input_output_aliases=aliases,
        scratch_shapes=[
            pltpu.VMEM((RET_HEADS, RET_DK, RET_DV), F32),
            pltpu.VMEM((HG_HEADS, HG_DV, HG_DK), F32),
            pltpu.VMEM((CHUNK, HG_HEADS * HG_DK), F32),
            pltpu.VMEM((CHUNK, HG_HEADS * HG_DK), F32),
            pltpu.VMEM((CHUNK, HG_HEADS * HG_DK), F32),
        ],
        compiler_params=pltpu.CompilerParams(
            dimension_semantics=("arbitrary",), vmem_limit_bytes=VMEM_LIMIT),
        name="mixer",
    )(*args)


def _post_kernel(gp_ref, ro_ref, ho_ref, x_ref, wr_ref, wh_ref, wo_ref, fg_ref,
                 rwh_ref, rwl_ref, x1_ref, h2_ref, route_ref, cnt_ref, run_scr):
    i = pl.program_id(0)
    tm = x_ref.shape[0]

    @pl.when(i == 0)
    def _():
        run_scr[...] = jnp.zeros_like(run_scr)

    gate_r = _sigmoid(gp_ref[:, 0:D_MODEL])
    gate_h = _sigmoid(gp_ref[:, D_MODEL:2 * D_MODEL])
    merged = gate_r * _dot(ro_ref[...], wr_ref[...]) + gate_h * _dot(ho_ref[...], wh_ref[...])
    x1 = x_ref[...] + _dot(merged.astype(BF16), wo_ref[...])
    x1_ref[...] = x1
    h2 = x1 * lax.rsqrt(jnp.mean(x1 * x1, axis=-1, keepdims=True) + EPS) * fg_ref[...]
    for kk in range(TILE_ROWS):
        h2_ref[pl.ds(kk, tm, stride=TILE_ROWS), :] = h2[:, kk * LANES:(kk + 1) * LANES]

    h_hi = h2.astype(BF16)
    h_lo = (h2 - h_hi.astype(F32)).astype(BF16)
    logits = _dot(h_hi, rwh_ref[...]) + (_dot(h_hi, rwl_ref[...]) + _dot(h_lo, rwh_ref[...]))

    lane = lax.broadcasted_iota(jnp.int32, (tm, LANES), 1)
    lanef = lane.astype(F32)
    neg_inf = jnp.float32(-jnp.inf)
    big = jnp.float32(1e9)
    is_g = (lane >= N_EXPERTS) & (lane < N_EXPERTS + N_GROUPS)
    gl = jnp.where(is_g, logits, neg_inf)
    gmax = jnp.max(gl, axis=-1, keepdims=True)
    gidx = jnp.min(jnp.where(gl == gmax, lanef, big), axis=-1, keepdims=True) - N_EXPERTS
    g_w = 1.0 / jnp.sum(jnp.exp(gl - gmax), axis=-1, keepdims=True)
    lane_group = jnp.floor(lanef * (1.0 / EXPERTS_PER_GROUP))
    in_group = (lane < N_EXPERTS) & (lane_group == gidx)
    el = jnp.where(in_group, logits, neg_inf)
    m1 = jnp.max(el, axis=-1, keepdims=True)
    i1 = jnp.min(jnp.where(el == m1, lanef, big), axis=-1, keepdims=True)
    el2 = jnp.where(lanef == i1, neg_inf, el)
    m2 = jnp.max(el2, axis=-1, keepdims=True)
    i2 = jnp.min(jnp.where(el2 == m2, lanef, big), axis=-1, keepdims=True)
    tt = jnp.exp(m2 - m1)
    w1 = g_w / (1.0 + tt)
    w2 = g_w * tt / (1.0 + tt)

    oh1 = lanef == i1
    oh2 = lanef == i2
    e_cnt = jnp.where(oh1, 1.0, 0.0) + jnp.where(oh2, 1.0, 0.0)
    ri = lax.broadcasted_iota(jnp.int32, (tm, tm), 0)
    ci = lax.broadcasted_iota(jnp.int32, (tm, tm), 1)
    strict = jnp.where(ri > ci, 1.0, 0.0).astype(BF16)
    prefix = _dot(strict, e_cnt.astype(BF16)) + run_scr[0:1, :]
    r1 = jnp.sum(jnp.where(oh1, prefix, 0.0), axis=-1, keepdims=True)
    r2 = jnp.sum(jnp.where(oh2, prefix, 0.0), axis=-1, keepdims=True)
    run_scr[0:1, :] = run_scr[0:1, :] + jnp.sum(e_cnt, axis=0, keepdims=True)
    cnt_ref[...] = run_scr[...]

    route = jnp.where(lane == 0, i1, 0.0)
    route = jnp.where(lane == 1, i2, route)
    route = jnp.where(lane == 2, w1, route)
    route = jnp.where(lane == 3, w2, route)
    route = jnp.where(lane == 4, r1, route)
    route = jnp.where(lane == 5, r2, route)
    route_ref[...] = route


def _post(p, ro, ho, x, wr, wh, wo, fgain, rw_hi, rw_lo):
    t = x.shape[0]
    tm = TOKEN_TILE
    gate_block = C_GATES // (2 * D_MODEL)
    const2 = lambda i: (0, 0)
    return pl.pallas_call(
        _post_kernel,
        grid=(t // tm,),
        in_specs=[
            pl.BlockSpec((tm, 2 * D_MODEL), lambda i: (i, gate_block)),
            pl.BlockSpec((tm, RET_HEADS * RET_DV), lambda i: (i, 0)),
            pl.BlockSpec((tm, HG_HEADS * HG_DV), lambda i: (i, 0)),
            pl.BlockSpec((tm, D_MODEL), lambda i: (i, 0)),
            pl.BlockSpec((RET_HEADS * RET_DV, D_MODEL), const2),
            pl.BlockSpec((HG_HEADS * HG_DV, D_MODEL), const2),
            pl.BlockSpec((D_MODEL, D_MODEL), const2),
            pl.BlockSpec((1, D_MODEL), const2),
            pl.BlockSpec((D_MODEL, LANES), const2),
            pl.BlockSpec((D_MODEL, LANES), const2),
        ],
        out_specs=[
            pl.BlockSpec((tm, D_MODEL), lambda i: (i, 0)),
            pl.BlockSpec((tm * TILE_ROWS, LANES), lambda i: (i, 0)),
            pl.BlockSpec((tm, LANES), lambda i: (i, 0)),
            pl.BlockSpec((8, LANES), const2),
        ],
        out_shape=[
            jax.ShapeDtypeStruct((t, D_MODEL), F32),
            jax.ShapeDtypeStruct((t * TILE_ROWS, LANES), F32),
            jax.ShapeDtypeStruct((t, LANES), F32),
            jax.ShapeDtypeStruct((8, LANES), F32),
        ],
        scratch_shapes=[pltpu.VMEM((8, LANES), F32)],
        compiler_params=pltpu.CompilerParams(
            dimension_semantics=("arbitrary",), vmem_limit_bytes=VMEM_LIMIT),
        name="post",
    )(p, ro, ho, x, wr, wh, wo, fgain, rw_hi, rw_lo)


def _plan_kernel(pairs_per_step, n_blocks,
                 codes_ref, cnt_ref, slot_ref, bexp_ref, nused_ref, pstart_ref):
    step = pl.program_id(0)
    base = step * pairs_per_step

    @pl.when(step == 0)
    def _():
        def per_expert(e, carry):
            start, last_e = carry
            pstart_ref[e] = start
            nb = (cnt_ref[e] + (ROW_BLOCK - 1)) // ROW_BLOCK
            first = start // ROW_BLOCK

            def fill(bb, _):
                bexp_ref[first + bb] = e
                return 0

            lax.fori_loop(0, nb, fill, 0)
            return start + nb * ROW_BLOCK, jnp.where(nb > 0, e, last_e)

        total, last_e = lax.fori_loop(0, N_EXPERTS, per_expert, (jnp.int32(0), jnp.int32(0)))
        n_used = total // ROW_BLOCK
        nused_ref[0] = n_used

        def fill_tail(bb, _):
            bexp_ref[bb] = last_e
            return 0

        lax.fori_loop(n_used, n_blocks, fill_tail, 0)

        def init(r, _):
            slot_ref[r] = 0
            return 0

        lax.fori_loop(0, n_blocks * ROW_BLOCK, init, 0, unroll=16)

    def place(i, _):
        code = codes_ref[base + i]
        slot_ref[pstart_ref[code >> 16] + (code & 0xFFFF)] = (base + i) >> 1
        return 0

    lax.fori_loop(0, pairs_per_step, place, 0, unroll=8)


def _plan(codes, counts, n_blocks):
    n_pairs = codes.shape[0]
    pairs_per_step = 2 * TOKEN_TILE
    assert n_pairs % pairs_per_step == 0
    smem = pl.BlockSpec(memory_space=pltpu.SMEM)
    return pl.pallas_call(
        functools.partial(_plan_kernel, pairs_per_step, n_blocks),
        grid=(n_pairs // pairs_per_step,),
        in_specs=[smem, smem],
        out_specs=[smem, smem, smem, smem],
        out_shape=[
            jax.ShapeDtypeStruct((n_blocks * ROW_BLOCK,), jnp.int32),
            jax.ShapeDtypeStruct((n_blocks,), jnp.int32),
            jax.ShapeDtypeStruct((1,), jnp.int32),
            jax.ShapeDtypeStruct((N_EXPERTS,), jnp.int32),
        ],
        compiler_params=pltpu.CompilerParams(dimension_semantics=("arbitrary",)),
        name="plan",
    )(codes, counts)


def _expert_kernel(bexp_ref, nused_ref, slot_ref, h2_hbm, wg_ref, wu_ref, wd_ref, yb_ref,
                   xbuf, sems, wg_s, wu_s, wd_s):
    b = pl.program_id(0)
    n_used = nused_ref[0]

    def row_copy(tok, buf_slot, r):
        return pltpu.make_async_copy(
            h2_hbm.at[pl.ds(pl.multiple_of(tok * TILE_ROWS, TILE_ROWS), TILE_ROWS)],
            xbuf.at[buf_slot, pl.ds(pl.multiple_of(r * TILE_ROWS, TILE_ROWS), TILE_ROWS)],
            sems.at[buf_slot])

    def gather(block):
        buf_slot = block % 2

        def issue(r, _):
            row_copy(slot_ref[block * ROW_BLOCK + r], buf_slot, r).start()
            return 0

        lax.fori_loop(0, ROW_BLOCK, issue, 0, unroll=8)

    @pl.when((b == 0) & (n_used > 0))
    def _():
        gather(b)

    @pl.when(b + 1 < n_used)
    def _():
        gather(b + 1)

    prev = bexp_ref[jnp.maximum(b - 1, 0)]

    @pl.when((b == 0) | (bexp_ref[b] != prev))
    def _():
        wg_s[...] = wg_ref[...].astype(BF16)
        wu_s[...] = wu_ref[...].astype(BF16)
        wd_s[...] = wd_ref[...].astype(BF16)

    @pl.when(b < n_used)
    def _():
        buf_slot = b % 2
        pltpu.make_async_copy(h2_hbm.at[pl.ds(0, ROW_BLOCK * TILE_ROWS)], xbuf.at[buf_slot],
                              sems.at[buf_slot]).wait()
        x = jnp.concatenate(
            [xbuf[buf_slot, pl.ds(kk, ROW_BLOCK, stride=TILE_ROWS), :] for kk in range(TILE_ROWS)],
            axis=1).astype(BF16)
        g = _dot(x, wg_s[...])
        u = _dot(x, wu_s[...])
        a = (g * _sigmoid(g) * u).astype(BF16)
        y = _dot(a, wd_s[...])
        for kk in range(TILE_ROWS):
            yb_ref[pl.ds(kk, ROW_BLOCK, stride=TILE_ROWS), :] = y[:, kk * LANES:(kk + 1) * LANES]

    @pl.when(b >= n_used)
    def _():
        yb_ref[...] = jnp.zeros_like(yb_ref)


def _experts(layer, bexp, nused, slot_tok, h2, w_gate, w_up, w_down):
    n_blocks = bexp.shape[0]
    wmap = lambda b, be, nu, st: (layer, be[b], 0, 0)
    return pl.pallas_call(
        _expert_kernel,
        grid_spec=pltpu.PrefetchScalarGridSpec(
            num_scalar_prefetch=3,
            grid=(n_blocks,),
            in_specs=[
                pl.BlockSpec(memory_space=pl.ANY),
                pl.BlockSpec((None, None, D_MODEL, D_EXPERT), wmap),
                pl.BlockSpec((None, None, D_MODEL, D_EXPERT), wmap),
                pl.BlockSpec((None, None, D_EXPERT, D_MODEL), wmap),
            ],
            out_specs=pl.BlockSpec((ROW_BLOCK * TILE_ROWS, LANES), lambda b, be, nu, st: (b, 0)),
            scratch_shapes=[
                pltpu.VMEM((2, ROW_BLOCK * TILE_ROWS, LANES), F32),
                pltpu.SemaphoreType.DMA((2,)),
                pltpu.VMEM((D_MODEL, D_EXPERT), BF16),
                pltpu.VMEM((D_MODEL, D_EXPERT), BF16),
                pltpu.VMEM((D_EXPERT, D_MODEL), BF16),
            ],
        ),
        out_shape=jax.ShapeDtypeStruct((n_blocks * ROW_BLOCK * TILE_ROWS, LANES), F32),
        compiler_params=pltpu.CompilerParams(
            dimension_semantics=("arbitrary",), vmem_limit_bytes=VMEM_LIMIT),
        name="experts",
    )(bexp, nused, slot_tok, h2, w_gate, w_up, w_down)


def _combine_kernel(final_tiles, codes_ref, pstart_ref, route_ref, x1_ref, yb_hbm, fg_ref, *rest):
    if final_tiles is not None:
        yp_ref, ys_ref, buf, sems = rest
    else:
        x2_ref, buf, sems = rest
    i = pl.program_id(0)
    tm = x1_ref.shape[0]

    def row_copy(d, buf_slot, r):
        return pltpu.make_async_copy(
            yb_hbm.at[pl.ds(pl.multiple_of(d * TILE_ROWS, TILE_ROWS), TILE_ROWS)],
            buf.at[buf_slot, pl.ds(pl.multiple_of(r * TILE_ROWS, TILE_ROWS), TILE_ROWS)],
            sems.at[buf_slot])

    def dest_of(pair):
        code = codes_ref[pair]
        return pstart_ref[code >> 16] + (code & 0xFFFF)

    def gather(tile):
        buf_slot = tile % 2
        base = tile * tm

        def issue(tt, _):
            row_copy(dest_of(2 * (base + tt)), buf_slot, tt).start()
            row_copy(dest_of(2 * (base + tt) + 1), buf_slot, tm + tt).start()
            return 0

        lax.fori_loop(0, tm, issue, 0, unroll=4)

    @pl.when(i == 0)
    def _():
        gather(i)

    @pl.when(i + 1 < pl.num_programs(0))
    def _():
        gather(i + 1)

    buf_slot = i % 2
    pltpu.make_async_copy(yb_hbm.at[pl.ds(0, 2 * tm * TILE_ROWS)], buf.at[buf_slot], sems.at[buf_slot]).wait()

    w1 = route_ref[:, 2:3]
    w2 = route_ref[:, 3:4]
    parts = []
    for kk in range(TILE_ROWS):
        y = (w1 * buf[buf_slot, pl.ds(kk, tm, stride=TILE_ROWS), :]
             + w2 * buf[buf_slot, pl.ds(tm * TILE_ROWS + kk, tm, stride=TILE_ROWS), :])
        parts.append(x1_ref[:, kk * LANES:(kk + 1) * LANES] + y)
    x2 = jnp.concatenate(parts, axis=1)
    if final_tiles is None:
        x2_ref[...] = x2
    else:
        first_real, n_prompt_tiles = final_tiles
        yn = x2 * lax.rsqrt(jnp.mean(x2 * x2, axis=-1, keepdims=True) + EPS) * fg_ref[...]

        @pl.when((i >= first_real) & (i < n_prompt_tiles))
        def _():
            yp_ref[...] = yn

        @pl.when(i >= n_prompt_tiles)
        def _():
            ys_ref[...] = yn


def _combine(codes, pstart, route, x1, yb, fgain, final_rows=None):
    t = x1.shape[0]
    tm = COMBINE_TILE
    row_spec = pl.BlockSpec((tm, D_MODEL), lambda i, cd, ps: (i, 0))
    if final_rows is None:
        final_tiles = None
        out_specs = [row_spec]
        out_shape = [jax.ShapeDtypeStruct((t, D_MODEL), F32)]
    else:
        first_row, n_prompt = final_rows
        assert first_row % tm == 0 and n_prompt % tm == 0
        first_real, npt = first_row // tm, n_prompt // tm
        final_tiles = (first_real, npt)
        out_specs = [
            pl.BlockSpec((tm, D_MODEL), lambda i, cd, ps: (jnp.clip(i - first_real, 0, npt - first_real - 1), 0)),
            pl.BlockSpec((tm, D_MODEL), lambda i, cd, ps: (jnp.maximum(i - npt, 0), 0)),
        ]
        out_shape = [jax.ShapeDtypeStruct((n_prompt - first_row, D_MODEL), F32),
                     jax.ShapeDtypeStruct((t - n_prompt, D_MODEL), F32)]
    return pl.pallas_call(
        functools.partial(_combine_kernel, final_tiles),
        grid_spec=pltpu.PrefetchScalarGridSpec(
            num_scalar_prefetch=2,
            grid=(t // tm,),
            in_specs=[
                pl.BlockSpec((tm, LANES), lambda i, cd, ps: (i, 0)),
                row_spec,
                pl.BlockSpec(memory_space=pl.ANY),
                pl.BlockSpec((1, D_MODEL), lambda i, cd, ps: (0, 0)),
            ],
            out_specs=out_specs,
            scratch_shapes=[
                pltpu.VMEM((2, 2 * tm * TILE_ROWS, LANES), F32),
                pltpu.SemaphoreType.DMA((2,)),
            ],
        ),
        out_shape=out_shape,
        compiler_params=pltpu.CompilerParams(
            dimension_semantics=("arbitrary",), vmem_limit_bytes=VMEM_LIMIT),
        name="combine",
    )(codes, pstart, route, x1, yb, fgain)


def _mixer_tables(n_prompt_chunks, prompt_pos0):
    half = RET_DK // 2
    inv_freq = 1.0 / (ROPE_BASE ** jnp.linspace(0.0, 1.0, half, dtype=F32))
    pos_prompt = prompt_pos0 + jnp.arange(n_prompt_chunks * CHUNK)
    pos_sample = PAST_LEN + jnp.arange(CHUNK)
    pos = jnp.concatenate([pos_prompt, pos_sample]).astype(F32)
    ang = pos[:, None] * inv_freq[None, :]
    cos, sin = jnp.cos(ang), jnp.sin(ang)
    log_gamma = jnp.log1p(-jnp.exp2(-5.0 - jnp.arange(RET_HEADS, dtype=F32)))
    idx = jnp.arange(CHUNK, dtype=F32)
    rel = idx[:, None] - idx[None, :]
    causal = rel >= 0
    dint = jnp.where(causal, jnp.exp(log_gamma[:, None, None] * jnp.where(causal, rel, 0.0)), 0.0)
    qdec = jnp.exp(log_gamma[None, :] * (idx[:, None] + 1.0))
    kdec = jnp.exp(log_gamma[None, :] * (CHUNK - 1.0 - idx[:, None]))
    cdec = jnp.exp(log_gamma * CHUNK)
    qdec = jnp.repeat(qdec, RET_DK, axis=1)
    kdec = jnp.repeat(kdec, RET_DK, axis=1)
    return cos, sin, dint, qdec, kdec, cdec


def _deinterleave_qk(w_in_l):
    w = w_in_l[:, :C_RV].reshape(D_MODEL, 2 * RET_HEADS, RET_DK // 2, 2)
    return jnp.swapaxes(w, 2, 3).reshape(D_MODEL, C_RV)


def _router_split(router_group_l, router_expert_l):
    rw = jnp.concatenate(
        [router_expert_l, router_group_l,
         jnp.zeros((D_MODEL, LANES - N_EXPERTS - N_GROUPS), F32)], axis=1)
    hi = rw.astype(BF16)
    lo = (rw - hi.astype(F32)).astype(BF16)
    return hi, lo


def kernel(x_prompt, x_sample, state_ret, state_hgrn, meta_tokens, mix_norm, w_in, hg_lb_logits, hg_norm,
           w_ret_branch, w_hg_branch, w_out, ffn_norm, router_group, router_expert, w_gate, w_up, w_down,
           final_norm):
    depth = w_in.shape[0]
    bp, seq, d = x_prompt.shape
    bs, dec, _ = x_sample.shape
    assert bp == 1 and dec == CHUNK and d == D_MODEL and seq % CHUNK == 0

    n_real = N_META + seq
    n_sample = bs * dec
    n_pad = (-(n_real + n_sample)) % PROJ_TM
    while (n_pad + n_real) % CHUNK:
        n_pad += PROJ_TM
    n_prompt = n_pad + n_real
    assert n_pad > 0
    t = n_prompt + n_sample
    npc = n_prompt // CHUNK
    n_blocks = -(-(2 * t) // ROW_BLOCK) + N_EXPERTS

    x = jnp.concatenate([
        jnp.zeros((n_pad, d), F32),
        meta_tokens.astype(F32),
        x_prompt[0],
        x_sample.reshape(n_sample, d),
    ], axis=0)

    tables = _mixer_tables(npc, -(n_pad + N_META))

    prob = jax.nn.softmax(hg_lb_logits.astype(F32), axis=0)
    cum = jnp.cumsum(prob, axis=0)
    lb_all = cum - cum[0:1]

    assert state_ret.dtype == F32 and state_hgrn.dtype == F32 and w_in.dtype == F32
    states = [jnp.zeros((depth, 1) + state_ret.shape[2:], F32), jnp.zeros((depth, 1) + state_hgrn.shape[2:], F32),
              jnp.zeros(state_ret.shape, F32), jnp.zeros(state_hgrn.shape, F32)]
    for l in range(depth):
        lb = lb_all[l]
        lbt = jnp.concatenate(
            [jnp.log(lb)[None], jnp.log1p(-lb)[None], (1.0 - lb)[None], jnp.zeros((5, d), F32)], axis=0)
        p = _proj(l, x, mix_norm[l][None], _deinterleave_qk(w_in[l, :, :C_RV]), w_in)
        ro, ho, *states = _mixer(l, p, tables, lbt, hg_norm[l][None].astype(F32),
                                 state_ret, state_hgrn, npc, states)
        rw_hi, rw_lo = _router_split(router_group[l], router_expert[l])
        x1, h2, route, counts = _post(
            p, ro, ho, x, w_ret_branch[l].astype(BF16), w_hg_branch[l].astype(BF16),
            w_out[l].astype(BF16), ffn_norm[l][None], rw_hi, rw_lo)
        ids = route[:, 0:2].astype(jnp.int32)
        ranks = route[:, 4:6].astype(jnp.int32)
        codes = (ids * 65536 + ranks).reshape(-1)
        slot_tok, bexp, nused, pstart = _plan(codes, counts[0].astype(jnp.int32), n_blocks)
        yb = _experts(l, bexp, nused, slot_tok, h2, w_gate, w_up, w_down)
        if l < depth - 1:
            x, = _combine(codes, pstart, route, x1, yb, final_norm[None])
        else:
            y_prompt, y_sample = _combine(codes, pstart, route, x1, yb, final_norm[None],
                                          final_rows=(n_prompt - seq, n_prompt))

    ret_prompt, hgrn_prompt, ret_sample, hgrn_sample = states
    return (y_prompt.reshape(bp, seq, d), y_sample.reshape(bs, dec, d),
            ret_prompt, hgrn_prompt, ret_sample, hgrn_sample)
```

```python
import functools

import numpy as np
import jax
import jax.numpy as jnp
from jax import lax
from jax.experimental import pallas as pl
from jax.experimental.pallas import tpu as pltpu

F32 = jnp.float32
BF16 = jnp.bfloat16

D_MODEL = 1024
CHUNK = 64
N_META = 16
RET_HEADS = 4
RET_DK = 256
RET_DV = 512
HG_HEADS = 8
HG_DK = 128
HG_DV = 128
N_GROUPS = 4
EXPERTS_PER_GROUP = 8
N_EXPERTS = 32
D_EXPERT = 512
ROPE_BASE = 10000.0
EPS = 1e-6
PAST_LEN = 4096
LOG2_E = 1.4426950408889634
MAX_SUBBLOCK_DROP_LOG2 = 100.0

C_RQ, C_RK, C_RV, C_RG = 0, 1024, 2048, 4096
C_HQ, C_HF, C_HI, C_HG = 6144, 7168, 8192, 9216
C_GATES = 10240
PROJ_WIDTH = 12288
MIX_WIDTH = C_GATES

TOKEN_TILE = 256
PROJ_TN = 2048
PROJ_TM = 768
ROW_BLOCK = 256
COMBINE_TILE = 128
SUB = 16
LANES = 128
TILE_ROWS = 8
VMEM_LIMIT = 56 * 1024 * 1024


def _sigmoid(x):
    return 1.0 / (1.0 + jnp.exp(-x))


def _dot(a, b):
    return jnp.dot(a, b, preferred_element_type=F32)


def _dot_nt(a, b):
    return lax.dot_general(a, b, (((1,), (1,)), ((), ())), preferred_element_type=F32)


def _dot_tn(a, b):
    return lax.dot_general(a, b, (((0,), (0,)), ((), ())), preferred_element_type=F32)


def _proj_kernel(x_ref, g_ref, wqk_ref, w_ref, o_ref, wbf_ref):
    j = pl.program_id(0)
    i = pl.program_id(1)

    @pl.when((i == 0) & (j == 0))
    def _():
        wbf_ref[...] = wqk_ref[...].astype(BF16)

    @pl.when((i == 0) & (j > 0))
    def _():
        wbf_ref[...] = w_ref[...].astype(BF16)

    x = x_ref[...]
    xn = x * lax.rsqrt(jnp.mean(x * x, axis=-1, keepdims=True) + EPS) * g_ref[...]
    o_ref[...] = _dot(xn.astype(BF16), wbf_ref[...])


def _proj(layer, x, gain, w_qk, w_in):
    t = x.shape[0]
    tm, tn = PROJ_TM, PROJ_TN
    assert t % tm == 0
    return pl.pallas_call(
        _proj_kernel,
        grid=(PROJ_WIDTH // tn, t // tm),
        in_specs=[
            pl.BlockSpec((tm, D_MODEL), lambda j, i: (i, 0)),
            pl.BlockSpec((1, D_MODEL), lambda j, i: (0, 0)),
            pl.BlockSpec((D_MODEL, tn), lambda j, i: (0, 0)),
            pl.BlockSpec((None, D_MODEL, tn), lambda j, i: (layer, 0, j)),
        ],
        out_specs=pl.BlockSpec((tm, tn), lambda j, i: (i, j)),
        out_shape=jax.ShapeDtypeStruct((t, PROJ_WIDTH), F32),
        scratch_shapes=[pltpu.VMEM((D_MODEL, tn), BF16)],
        compiler_params=pltpu.CompilerParams(
            dimension_semantics=("arbitrary", "arbitrary"), vmem_limit_bytes=VMEM_LIMIT),
        name="proj",
    )(x, gain, w_qk, w_in)


def _mixer_kernel(n_prompt_chunks,
                  p_ref, cos_ref, sin_ref, dint_ref, qdec_ref, kdec_ref, cdec_ref,
                  lbt_ref, hgain_ref, sr_in_ref, sh_in_ref, *rest):
    (ro_ref, ho_ref, srp_out_ref, shp_out_ref, srs_out_ref, shs_out_ref,
     sr_scr, sh_scr, hq_scr, hb_scr, hk_scr, ho_scr) = rest[4:]
    c = pl.program_id(0)
    half = RET_DK // 2

    @pl.when(c == 0)
    def _():
        sr_scr[...] = jnp.zeros_like(sr_scr)
        sh_scr[...] = jnp.zeros_like(sh_scr)

    @pl.when(c >= n_prompt_chunks)
    def _():
        sr_scr[...] = sr_in_ref[...]
        for h in range(HG_HEADS):
            sh_scr[h] = sh_in_ref[h].T

    cos = cos_ref[...]
    sin = sin_ref[...]
    scores_v, to_state = [], []
    for h in range(RET_HEADS):
        q = p_ref[:, C_RQ + h * RET_DK:C_RQ + (h + 1) * RET_DK]
        q1, q2 = q[:, :half], q[:, half:]
        qr = jnp.concatenate([q1 * cos - q2 * sin, q1 * sin + q2 * cos], axis=1)
        to_state.append((qr * qdec_ref[:, h * RET_DK:(h + 1) * RET_DK]).astype(BF16))
        scores_v.append(qr.astype(BF16))
    for h in range(RET_HEADS):
        k = p_ref[:, C_RK + h * RET_DK:C_RK + (h + 1) * RET_DK]
        k1, k2 = k[:, :half], k[:, half:]
        kr = jnp.concatenate([k1 * cos - k2 * sin, k1 * sin + k2 * cos], axis=1) * (RET_DK ** -0.5)
        to_state.append((kr * kdec_ref[:, h * RET_DK:(h + 1) * RET_DK]).astype(BF16))
        scores_v[h] = _dot_nt(scores_v[h], kr.astype(BF16)) * dint_ref[h]
    pr = lax.broadcasted_iota(jnp.int32, (RET_DK, RET_DK), 0)
    pc = lax.broadcasted_iota(jnp.int32, (RET_DK, RET_DK), 1)
    src_lane = jnp.where(pr < half, 2 * pr, 2 * (pr - half) + 1)
    perm = jnp.where(pc == src_lane, 1.0, 0.0).astype(BF16)
    natural = _dot(jnp.concatenate(to_state, axis=0), perm).astype(BF16)
    for h in range(RET_HEADS):
        v = p_ref[:, C_RV + h * RET_DV:C_RV + (h + 1) * RET_DV].astype(BF16)
        s = sr_scr[h]
        qd = natural[h * CHUNK:(h + 1) * CHUNK]
        kd = natural[(RET_HEADS + h) * CHUNK:(RET_HEADS + h + 1) * CHUNK]
        o = _dot(scores_v[h].astype(BF16), v) + _dot(qd, s.astype(BF16))
        sr_scr[h] = cdec_ref[h] * s + _dot_tn(kd, v)
        on = o * lax.rsqrt(jnp.mean(o * o, axis=-1, keepdims=True) + EPS)
        g = p_ref[:, C_RG + h * RET_DV:C_RG + (h + 1) * RET_DV]
        ro_ref[:, h * RET_DV:(h + 1) * RET_DV] = (on * (g * _sigmoid(g))).astype(BF16)

    hf = p_ref[:, C_HF:C_HF + D_MODEL]
    log_lb = lbt_ref[0:1, :]
    log1m_lb = lbt_ref[1:2, :]
    one_m_lb = lbt_ref[2:3, :]
    log_sig = jnp.minimum(hf, 0.0) - jnp.log(1.0 + jnp.exp(-jnp.abs(hf)))
    b_ = log1m_lb + log_sig
    log_f = jnp.maximum(log_lb, b_) + jnp.log(1.0 + jnp.exp(-jnp.abs(log_lb - b_)))
    k_in = one_m_lb * _sigmoid(-hf)
    hq = p_ref[:, C_HQ:C_HQ + D_MODEL]
    q_all = hq * _sigmoid(hq)

    ri = lax.broadcasted_iota(jnp.int32, (CHUNK, CHUNK), 0)
    ci = lax.broadcasted_iota(jnp.int32, (CHUNK, CHUNK), 1)
    tri = jnp.where(ri >= ci, 1.0, 0.0).astype(BF16)
    f_hi = log_f.astype(BF16)
    r1 = log_f - f_hi.astype(F32)
    f_mid = r1.astype(BF16)
    f_lo = (r1 - f_mid.astype(F32)).astype(BF16)
    hb_scr[...] = (_dot(tri, f_hi) + _dot(tri, f_mid) + _dot(tri, f_lo)) * LOG2_E
    hq_scr[...] = q_all
    hk_scr[...] = k_in

    n_sub = CHUNK // SUB
    row = lax.broadcasted_iota(jnp.int32, (CHUNK, HG_DK), 0)
    col_sub = lax.broadcasted_iota(jnp.int32, (SUB, CHUNK), 1)
    hgain = hgain_ref[...]

    drops = [hb_scr[SUB * ib:SUB * ib + 1, :] - hb_scr[SUB * (ib + 1) - 1:SUB * (ib + 1), :]
             for ib in range(n_sub)]
    max_drop = jnp.max(jnp.concatenate(drops, axis=0))
    factorise_ok = max_drop <= MAX_SUBBLOCK_DROP_LOG2

    def hgrn_head(h, factorised):
        sl = slice(h * HG_DK, (h + 1) * HG_DK)
        q = hq_scr[:, sl]
        b = hb_scr[:, sl]
        k = hk_scr[:, sl]
        vb = p_ref[:, C_HI + h * HG_DV:C_HI + (h + 1) * HG_DV].astype(BF16)
        if factorised:
            st = sh_scr[h]
            o = _dot_nt((q * jnp.exp2(b)).astype(BF16), st.astype(BF16))
            ho_scr[:, sl] = o
            b_last = b[CHUNK - 1:CHUNK, :]
            kd = (k * jnp.exp2(b_last - b)).astype(BF16)
            sh_scr[h] = st * jnp.exp2(b_last) + _dot_tn(vb, kd)
        else:
            o = ho_scr[:, sl]
        b_end = [b[SUB * (jb + 1) - 1:SUB * (jb + 1), :] for jb in range(n_sub)]
        b_end_rows = jnp.concatenate(
            [jnp.broadcast_to(b_end[jb], (SUB, HG_DK)) for jb in range(n_sub)], axis=0)
        k_hat = k * jnp.exp2(b_end_rows - b)
        lhs, rhs = [], []
        for jb in range(n_sub - 1):
            q_hat = q * jnp.exp2(b - b_end[jb])
            lhs.append(jnp.where(row >= SUB * (jb + 1), q_hat, 0.0).astype(BF16))
            rhs.append(jnp.where((row >= SUB * jb) & (row < SUB * (jb + 1)), k_hat, 0.0).astype(BF16))
        a_off = _dot_nt(jnp.concatenate(lhs, axis=1), jnp.concatenate(rhs, axis=1))
        if factorised:
            b_first = jnp.concatenate(
                [jnp.broadcast_to(b[SUB * ib:SUB * ib + 1, :], (SUB, HG_DK)) for ib in range(n_sub)], axis=0)
            q_t = (q * jnp.exp2(b - b_first)).astype(BF16)
            k_t = (k * jnp.exp2(b_first - b)).astype(BF16)
            same_block = (ri // SUB) == (ci // SUB)
            a_diag = jnp.where(same_block & (ri >= ci), _dot_nt(q_t, k_t), 0.0)
        else:
            diag = []
            for ib in range(n_sub):
                r0 = SUB * ib
                b_i = b[r0:r0 + SUB, :]
                q_i = q[r0:r0 + SUB, :]
                blk = jnp.zeros((SUB, CHUNK), F32)
                for j in range(SUB):
                    b_j = b[r0 + j:r0 + j + 1, :]
                    k_j = k[r0 + j:r0 + j + 1, :]
                    sc = jnp.sum(jnp.exp2(b_i - b_j) * q_i * k_j, axis=-1, keepdims=True)
                    blk = jnp.where(col_sub == r0 + j, sc, blk)
                diag.append(blk)
            a_diag = jnp.where(ri >= ci, jnp.concatenate(diag, axis=0), 0.0)
        o = o + _dot((a_off + a_diag).astype(BF16), vb)
        on = o * lax.rsqrt(jnp.mean(o * o, axis=-1, keepdims=True) + EPS) * hgain
        g = p_ref[:, C_HG + h * HG_DV:C_HG + (h + 1) * HG_DV]
        ho_ref[:, h * HG_DV:(h + 1) * HG_DV] = (on * (g * _sigmoid(g))).astype(BF16)

    for h in range(HG_HEADS):
        hgrn_head(h, True)

    @pl.when(jnp.logical_not(factorise_ok))
    def _():
        for h in range(HG_HEADS):
            hgrn_head(h, False)

    @pl.when(c == n_prompt_chunks - 1)
    def _():
        srp_out_ref[...] = sr_scr[...]
        for h in range(HG_HEADS):
            shp_out_ref[h] = sh_scr[h].T

    @pl.when(c >= n_prompt_chunks)
    def _():
        srs_out_ref[...] = sr_scr[...]
        for h in range(HG_HEADS):
            shs_out_ref[h] = sh_scr[h].T


def _mixer(layer, p, tables, lbt, hgain, state_ret, state_hg, n_prompt_chunks, prev_states):
    t = p.shape[0]
    n_chunks = t // CHUNK
    depth, n_streams = state_ret.shape[0], state_ret.shape[1]
    npc = n_prompt_chunks
    cos, sin, dint, qdec, kdec, cdec = tables

    def stream_of(c):
        return jnp.maximum(c - npc, 0)

    const2 = lambda c: (0, 0)
    ret_blk = (None, None, RET_HEADS, RET_DK, RET_DV)
    hg_blk = (None, None, HG_HEADS, HG_DK, HG_DV)
    in_specs = [
        pl.BlockSpec((CHUNK, MIX_WIDTH), lambda c: (c, 0)),
        pl.BlockSpec((CHUNK, LANES), lambda c: (jnp.minimum(c, npc), 0)),
        pl.BlockSpec((CHUNK, LANES), lambda c: (jnp.minimum(c, npc), 0)),
        pl.BlockSpec((RET_HEADS, CHUNK, CHUNK), lambda c: (0, 0, 0)),
        pl.BlockSpec((CHUNK, RET_HEADS * RET_DK), const2),
        pl.BlockSpec((CHUNK, RET_HEADS * RET_DK), const2),
        pl.BlockSpec(memory_space=pltpu.SMEM),
        pl.BlockSpec((8, D_MODEL), const2),
        pl.BlockSpec((1, HG_DV), const2),
        pl.BlockSpec(ret_blk, lambda c: (layer, stream_of(c), 0, 0, 0)),
        pl.BlockSpec(hg_blk, lambda c: (layer, stream_of(c), 0, 0, 0)),
    ]
    args = [p, cos, sin, dint, qdec, kdec, cdec, lbt, hgain, state_ret, state_hg]
    aliases = {}
    for n, prev in enumerate(prev_states):
        aliases[len(args)] = 2 + n
        in_specs.append(pl.BlockSpec(memory_space=pl.ANY))
        args.append(prev)
    return pl.pallas_call(
        functools.partial(_mixer_kernel, npc),
        grid=(n_chunks,),
        in_specs=in_specs,
        out_specs=[
            pl.BlockSpec((CHUNK, RET_HEADS * RET_DV), lambda c: (c, 0)),
            pl.BlockSpec((CHUNK, HG_HEADS * HG_DV), lambda c: (c, 0)),
            pl.BlockSpec(ret_blk, lambda c: (layer, 0, 0, 0, 0)),
            pl.BlockSpec(hg_blk, lambda c: (layer, 0, 0, 0, 0)),
            pl.BlockSpec(ret_blk, lambda c: (layer, stream_of(c), 0, 0, 0)),
            pl.BlockSpec(hg_blk, lambda c: (layer, stream_of(c), 0, 0, 0)),
        ],
        out_shape=[
            jax.ShapeDtypeStruct((t, RET_HEADS * RET_DV), BF16),
            jax.ShapeDtypeStruct((t, HG_HEADS * HG_DV), BF16),
            jax.ShapeDtypeStruct((depth, 1, RET_HEADS, RET_DK, RET_DV), F32),
            jax.ShapeDtypeStruct((depth, 1, HG_HEADS, HG_DK, HG_DV), F32),
            jax.ShapeDtypeStruct((depth, n_streams, RET_HEADS, RET_DK, RET_DV), F32),
            jax.ShapeDtypeStruct((depth, n_streams, HG_HEADS, HG_DK, HG_DV), F32),
        ],
        input_output_aliases=aliases,
        scratch_shapes=[
            pltpu.VMEM((RET_HEADS, RET_DK, RET_DV), F32),
            pltpu.VMEM((HG_HEADS, HG_DV, HG_DK), F32),
            pltpu.VMEM((CHUNK, HG_HEADS * HG_DK), F32),
            pltpu.VMEM((CHUNK, HG_HEADS * HG_DK), F32),
            pltpu.VMEM((CHUNK, HG_HEADS * HG_DK), F32),
            pltpu.VMEM((CHUNK, HG_HEADS * HG_DV), F32),
        ],
        compiler_params=pltpu.CompilerParams(
            dimension_semantics=("arbitrary",), vmem_limit_bytes=VMEM_LIMIT),
        name="mixer",
    )(*args)


def _post_kernel(gp_ref, ro_ref, ho_ref, x_ref, wr_ref, wh_ref, wo_ref, fg_ref,
                 rwh_ref, rwl_ref, x1_ref, h2_ref, route_ref, cnt_ref, run_scr):
    i = pl.program_id(0)
    tm = x_ref.shape[0]

    @pl.when(i == 0)
    def _():
        run_scr[...] = jnp.zeros_like(run_scr)

    gate_r = _sigmoid(gp_ref[:, 0:D_MODEL])
    gate_h = _sigmoid(gp_ref[:, D_MODEL:2 * D_MODEL])
    merged = gate_r * _dot(ro_ref[...], wr_ref[...]) + gate_h * _dot(ho_ref[...], wh_ref[...])
    x1 = x_ref[...] + _dot(merged.astype(BF16), wo_ref[...])
    x1_ref[...] = x1
    h2 = x1 * lax.rsqrt(jnp.mean(x1 * x1, axis=-1, keepdims=True) + EPS) * fg_ref[...]
    for kk in range(TILE_ROWS):
        h2_ref[pl.ds(kk, tm, stride=TILE_ROWS), :] = h2[:, kk * LANES:(kk + 1) * LANES]

    h_hi = h2.astype(BF16)
    h_lo = (h2 - h_hi.astype(F32)).astype(BF16)
    logits = _dot(h_hi, rwh_ref[...]) + (_dot(h_hi, rwl_ref[...]) + _dot(h_lo, rwh_ref[...]))

    lane = lax.broadcasted_iota(jnp.int32, (tm, LANES), 1)
    lanef = lane.astype(F32)
    neg_inf = jnp.float32(-jnp.inf)
    big = jnp.float32(1e9)
    is_g = (lane >= N_EXPERTS) & (lane < N_EXPERTS + N_GROUPS)
    gl = jnp.where(is_g, logits, neg_inf)
    gmax = jnp.max(gl, axis=-1, keepdims=True)
    gidx = jnp.min(jnp.where(gl == gmax, lanef, big), axis=-1, keepdims=True) - N_EXPERTS
    g_w = 1.0 / jnp.sum(jnp.exp(gl - gmax), axis=-1, keepdims=True)
    lane_group = jnp.floor(lanef * (1.0 / EXPERTS_PER_GROUP))
    in_group = (lane < N_EXPERTS) & (lane_group == gidx)
    el = jnp.where(in_group, logits, neg_inf)
    m1 = jnp.max(el, axis=-1, keepdims=True)
    i1 = jnp.min(jnp.where(el == m1, lanef, big), axis=-1, keepdims=True)
    el2 = jnp.where(lanef == i1, neg_inf, el)
    m2 = jnp.max(el2, axis=-1, keepdims=True)
    i2 = jnp.min(jnp.where(el2 == m2, lanef, big), axis=-1, keepdims=True)
    tt = jnp.exp(m2 - m1)
    w1 = g_w / (1.0 + tt)
    w2 = g_w * tt / (1.0 + tt)

    oh1 = lanef == i1
    oh2 = lanef == i2
    e_cnt = jnp.where(oh1, 1.0, 0.0) + jnp.where(oh2, 1.0, 0.0)
    ri = lax.broadcasted_iota(jnp.int32, (tm, tm), 0)
    ci = lax.broadcasted_iota(jnp.int32, (tm, tm), 1)
    strict = jnp.where(ri > ci, 1.0, 0.0).astype(BF16)
    prefix = _dot(strict, e_cnt.astype(BF16)) + run_scr[0:1, :]
    r1 = jnp.sum(jnp.where(oh1, prefix, 0.0), axis=-1, keepdims=True)
    r2 = jnp.sum(jnp.where(oh2, prefix, 0.0), axis=-1, keepdims=True)
    run_scr[0:1, :] = run_scr[0:1, :] + jnp.sum(e_cnt, axis=0, keepdims=True)
    cnt_ref[...] = run_scr[...]

    route = jnp.where(lane == 0, i1, 0.0)
    route = jnp.where(lane == 1, i2, route)
    route = jnp.where(lane == 2, w1, route)
    route = jnp.where(lane == 3, w2, route)
    route = jnp.where(lane == 4, r1, route)
    route = jnp.where(lane == 5, r2, route)
    route_ref[...] = route


def _post(p, ro, ho, x, wr, wh, wo, fgain, rw_hi, rw_lo):
    t = x.shape[0]
    tm = TOKEN_TILE
    gate_block = C_GATES // (2 * D_MODEL)
    const2 = lambda i: (0, 0)
    return pl.pallas_call(
        _post_kernel,
        grid=(t // tm,),
        in_specs=[
            pl.BlockSpec((tm, 2 * D_MODEL), lambda i: (i, gate_block)),
            pl.BlockSpec((tm, RET_HEADS * RET_DV), lambda i: (i, 0)),
            pl.BlockSpec((tm, HG_HEADS * HG_DV), lambda i: (i, 0)),
            pl.BlockSpec((tm, D_MODEL), lambda i: (i, 0)),
            pl.BlockSpec((RET_HEADS * RET_DV, D_MODEL), const2),
            pl.BlockSpec((HG_HEADS * HG_DV, D_MODEL), const2),
            pl.BlockSpec((D_MODEL, D_MODEL), const2),
            pl.BlockSpec((1, D_MODEL), const2),
            pl.BlockSpec((D_MODEL, LANES), const2),
            pl.BlockSpec((D_MODEL, LANES), const2),
        ],
        out_specs=[
            pl.BlockSpec((tm, D_MODEL), lambda i: (i, 0)),
            pl.BlockSpec((tm * TILE_ROWS, LANES), lambda i: (i, 0)),
            pl.BlockSpec((tm, LANES), lambda i: (i, 0)),
            pl.BlockSpec((8, LANES), const2),
        ],
        out_shape=[
            jax.ShapeDtypeStruct((t, D_MODEL), F32),
            jax.ShapeDtypeStruct((t * TILE_ROWS, LANES), F32),
            jax.ShapeDtypeStruct((t, LANES), F32),
            jax.ShapeDtypeStruct((8, LANES), F32),
        ],
        scratch_shapes=[pltpu.VMEM((8, LANES), F32)],
        compiler_params=pltpu.CompilerParams(
            dimension_semantics=("arbitrary",), vmem_limit_bytes=VMEM_LIMIT),
        name="post",
    )(p, ro, ho, x, wr, wh, wo, fgain, rw_hi, rw_lo)


def _plan_kernel(pairs_per_step, n_blocks,
                 codes_ref, cnt_ref, slot_ref, bexp_ref, nused_ref, pstart_ref):
    step = pl.program_id(0)
    base = step * pairs_per_step

    @pl.when(step == 0)
    def _():
        def per_expert(e, carry):
            start, last_e = carry
            pstart_ref[e] = start
            nb = (cnt_ref[e] + (ROW_BLOCK - 1)) // ROW_BLOCK
            first = start // ROW_BLOCK

            def fill(bb, _):
                bexp_ref[first + bb] = e
                return 0

            lax.fori_loop(0, nb, fill, 0)
            return start + nb * ROW_BLOCK, jnp.where(nb > 0, e, last_e)

        total, last_e = lax.fori_loop(0, N_EXPERTS, per_expert, (jnp.int32(0), jnp.int32(0)))
        n_used = total // ROW_BLOCK
        nused_ref[0] = n_used

        def fill_tail(bb, _):
            bexp_ref[bb] = last_e
            return 0

        lax.fori_loop(n_used, n_blocks, fill_tail, 0)

        def init(r, _):
            slot_ref[r] = 0
            return 0

        lax.fori_loop(0, n_blocks * ROW_BLOCK, init, 0, unroll=16)

    def place(i, _):
        code = codes_ref[base + i]
        slot_ref[pstart_ref[code >> 16] + (code & 0xFFFF)] = (base + i) >> 1
        return 0

    lax.fori_loop(0, pairs_per_step, place, 0, unroll=8)


def _plan(codes, counts, n_blocks):
    n_pairs = codes.shape[0]
    pairs_per_step = 2 * TOKEN_TILE
    assert n_pairs % pairs_per_step == 0
    smem = pl.BlockSpec(memory_space=pltpu.SMEM)
    return pl.pallas_call(
        functools.partial(_plan_kernel, pairs_per_step, n_blocks),
        grid=(n_pairs // pairs_per_step,),
        in_specs=[smem, smem],
        out_specs=[smem, smem, smem, smem],
        out_shape=[
            jax.ShapeDtypeStruct((n_blocks * ROW_BLOCK,), jnp.int32),
            jax.ShapeDtypeStruct((n_blocks,), jnp.int32),
            jax.ShapeDtypeStruct((1,), jnp.int32),
            jax.ShapeDtypeStruct((N_EXPERTS,), jnp.int32),
        ],
        compiler_params=pltpu.CompilerParams(dimension_semantics=("arbitrary",)),
        name="plan",
    )(codes, counts)


def _expert_kernel(bexp_ref, nused_ref, slot_ref, h2_hbm, wg_ref, wu_ref, wd_ref, yb_ref,
                   xbuf, sems, wg_s, wu_s, wd_s):
    b = pl.program_id(0)
    n_used = nused_ref[0]

    def row_copy(tok, buf_slot, r):
        return pltpu.make_async_copy(
            h2_hbm.at[pl.ds(pl.multiple_of(tok * TILE_ROWS, TILE_ROWS), TILE_ROWS)],
            xbuf.at[buf_slot, pl.ds(pl.multiple_of(r * TILE_ROWS, TILE_ROWS), TILE_ROWS)],
            sems.at[buf_slot])

    def wait_block(block):
        buf_slot = block % 2
        pltpu.make_async_copy(h2_hbm.at[pl.ds(0, ROW_BLOCK * TILE_ROWS)], xbuf.at[buf_slot],
                              sems.at[buf_slot]).wait()

    @pl.when(b == 0)
    def _():
        def issue(r, _):
            row_copy(slot_ref[r], 0, r).start()
            return 0

        lax.fori_loop(0, ROW_BLOCK, issue, 0, unroll=8)

    prev = bexp_ref[jnp.maximum(b - 1, 0)]

    @pl.when((b == 0) | (bexp_ref[b] != prev))
    def _():
        wg_s[...] = wg_ref[...].astype(BF16)
        wu_s[...] = wu_ref[...].astype(BF16)
        wd_s[...] = wd_ref[...].astype(BF16)

    @pl.when(b < n_used)
    def _():
        buf_slot = b % 2
        wait_block(b)
        nxt = (b + 1) * ROW_BLOCK
        for r in range(ROW_BLOCK):
            row_copy(slot_ref[nxt + r], 1 - buf_slot, r).start(priority=r % 2)
        x = jnp.concatenate(
            [xbuf[buf_slot, pl.ds(kk, ROW_BLOCK, stride=TILE_ROWS), :] for kk in range(TILE_ROWS)],
            axis=1).astype(BF16)
        g = _dot(x, wg_s[...])
        u = _dot(x, wu_s[...])
        a = (g * _sigmoid(g) * u).astype(BF16)
        y = _dot(a, wd_s[...])
        for kk in range(TILE_ROWS):
            yb_ref[pl.ds(kk, ROW_BLOCK, stride=TILE_ROWS), :] = y[:, kk * LANES:(kk + 1) * LANES]

    @pl.when(b == n_used)
    def _():
        wait_block(b)

    @pl.when(b >= n_used)
    def _():
        yb_ref[...] = jnp.zeros_like(yb_ref)


def _experts(layer, bexp, nused, slot_tok, h2, w_gate, w_up, w_down):
    n_blocks = bexp.shape[0]
    assert slot_tok.shape[0] == n_blocks * ROW_BLOCK
    wmap = lambda b, be, nu, st: (layer, be[b], 0, 0)
    return pl.pallas_call(
        _expert_kernel,
        grid_spec=pltpu.PrefetchScalarGridSpec(
            num_scalar_prefetch=3,
            grid=(n_blocks,),
            in_specs=[
                pl.BlockSpec(memory_space=pl.ANY),
                pl.BlockSpec((None, None, D_MODEL, D_EXPERT), wmap),
                pl.BlockSpec((None, None, D_MODEL, D_EXPERT), wmap),
                pl.BlockSpec((None, None, D_EXPERT, D_MODEL), wmap),
            ],
            out_specs=pl.BlockSpec((ROW_BLOCK * TILE_ROWS, LANES), lambda b, be, nu, st: (b, 0)),
            scratch_shapes=[
                pltpu.VMEM((2, ROW_BLOCK * TILE_ROWS, LANES), F32),
                pltpu.SemaphoreType.DMA((2,)),
                pltpu.VMEM((D_MODEL, D_EXPERT), BF16),
                pltpu.VMEM((D_MODEL, D_EXPERT), BF16),
                pltpu.VMEM((D_EXPERT, D_MODEL), BF16),
            ],
        ),
        out_shape=jax.ShapeDtypeStruct((n_blocks * ROW_BLOCK * TILE_ROWS, LANES), F32),
        compiler_params=pltpu.CompilerParams(
            dimension_semantics=("arbitrary",), vmem_limit_bytes=VMEM_LIMIT),
        name="experts",
    )(bexp, nused, slot_tok, h2, w_gate, w_up, w_down)


def _combine_kernel(final_tiles, codes_ref, pstart_ref, route_ref, x1_ref, yb_hbm, fg_ref, *rest):
    if final_tiles is not None:
        yp_ref, ys_ref, buf, sems = rest
    else:
        x2_ref, buf, sems = rest
    i = pl.program_id(0)
    tm = x1_ref.shape[0]

    def row_copy(d, buf_slot, r):
        return pltpu.make_async_copy(
            yb_hbm.at[pl.ds(pl.multiple_of(d * TILE_ROWS, TILE_ROWS), TILE_ROWS)],
            buf.at[buf_slot, pl.ds(pl.multiple_of(r * TILE_ROWS, TILE_ROWS), TILE_ROWS)],
            sems.at[buf_slot])

    def dest_of(pair):
        code = codes_ref[pair]
        return pstart_ref[code >> 16] + (code & 0xFFFF)

    def gather(tile):
        buf_slot = tile % 2
        base = tile * tm

        def issue(tt, _):
            row_copy(dest_of(2 * (base + tt)), buf_slot, tt).start(priority=0)
            row_copy(dest_of(2 * (base + tt) + 1), buf_slot, tm + tt).start(priority=1)
            return 0

        lax.fori_loop(0, tm, issue, 0, unroll=4)

    @pl.when(i == 0)
    def _():
        gather(i)

    @pl.when(i + 1 < pl.num_programs(0))
    def _():
        gather(i + 1)

    buf_slot = i % 2
    pltpu.make_async_copy(yb_hbm.at[pl.ds(0, 2 * tm * TILE_ROWS)], buf.at[buf_slot], sems.at[buf_slot]).wait()

    w1 = route_ref[:, 2:3]
    w2 = route_ref[:, 3:4]
    parts = []
    for kk in range(TILE_ROWS):
        y = (w1 * buf[buf_slot, pl.ds(kk, tm, stride=TILE_ROWS), :]
             + w2 * buf[buf_slot, pl.ds(tm * TILE_ROWS + kk, tm, stride=TILE_ROWS), :])
        parts.append(x1_ref[:, kk * LANES:(kk + 1) * LANES] + y)
    x2 = jnp.concatenate(parts, axis=1)
    if final_tiles is None:
        x2_ref[...] = x2
    else:
        first_real, n_prompt_tiles = final_tiles
        yn = x2 * lax.rsqrt(jnp.mean(x2 * x2, axis=-1, keepdims=True) + EPS) * fg_ref[...]

        @pl.when((i >= first_real) & (i < n_prompt_tiles))
        def _():
            yp_ref[...] = yn

        @pl.when(i >= n_prompt_tiles)
        def _():
            ys_ref[...] = yn


def _combine(codes, pstart, route, x1, yb, fgain, final_rows=None):
    t = x1.shape[0]
    tm = COMBINE_TILE
    row_spec = pl.BlockSpec((tm, D_MODEL), lambda i, cd, ps: (i, 0))
    if final_rows is None:
        final_tiles = None
        out_specs = [row_spec]
        out_shape = [jax.ShapeDtypeStruct((t, D_MODEL), F32)]
    else:
        first_row, n_prompt = final_rows
        assert first_row % tm == 0 and n_prompt % tm == 0
        first_real, npt = first_row // tm, n_prompt // tm
        final_tiles = (first_real, npt)
        out_specs = [
            pl.BlockSpec((tm, D_MODEL), lambda i, cd, ps: (jnp.clip(i - first_real, 0, npt - first_real - 1), 0)),
            pl.BlockSpec((tm, D_MODEL), lambda i, cd, ps: (jnp.maximum(i - npt, 0), 0)),
        ]
        out_shape = [jax.ShapeDtypeStruct((n_prompt - first_row, D_MODEL), F32),
                     jax.ShapeDtypeStruct((t - n_prompt, D_MODEL), F32)]
    return pl.pallas_call(
        functools.partial(_combine_kernel, final_tiles),
        grid_spec=pltpu.PrefetchScalarGridSpec(
            num_scalar_prefetch=2,
            grid=(t // tm,),
            in_specs=[
                pl.BlockSpec((tm, LANES), lambda i, cd, ps: (i, 0)),
                row_spec,
                pl.BlockSpec(memory_space=pl.ANY),
                pl.BlockSpec((1, D_MODEL), lambda i, cd, ps: (0, 0)),
            ],
            out_specs=out_specs,
            scratch_shapes=[
                pltpu.VMEM((2, 2 * tm * TILE_ROWS, LANES), F32),
                pltpu.SemaphoreType.DMA((2,)),
            ],
        ),
        out_shape=out_shape,
        compiler_params=pltpu.CompilerParams(
            dimension_semantics=("arbitrary",), vmem_limit_bytes=VMEM_LIMIT),
        name="combine",
    )(codes, pstart, route, x1, yb, fgain)


def _mixer_tables(n_prompt_chunks, prompt_pos0):
    half = RET_DK // 2
    inv_freq = 1.0 / (ROPE_BASE ** jnp.linspace(0.0, 1.0, half, dtype=F32))
    pos_prompt = prompt_pos0 + jnp.arange(n_prompt_chunks * CHUNK)
    pos_sample = PAST_LEN + jnp.arange(CHUNK)
    pos = jnp.concatenate([pos_prompt, pos_sample]).astype(F32)
    ang = pos[:, None] * inv_freq[None, :]
    cos, sin = jnp.cos(ang), jnp.sin(ang)
    log_gamma = jnp.log1p(-jnp.exp2(-5.0 - jnp.arange(RET_HEADS, dtype=F32)))
    idx = jnp.arange(CHUNK, dtype=F32)
    rel = idx[:, None] - idx[None, :]
    causal = rel >= 0
    dint = jnp.where(causal, jnp.exp(log_gamma[:, None, None] * jnp.where(causal, rel, 0.0)), 0.0)
    qdec = jnp.exp(log_gamma[None, :] * (idx[:, None] + 1.0))
    kdec = jnp.exp(log_gamma[None, :] * (CHUNK - 1.0 - idx[:, None]))
    cdec = jnp.exp(log_gamma * CHUNK)
    qdec = jnp.repeat(qdec, RET_DK, axis=1)
    kdec = jnp.repeat(kdec, RET_DK, axis=1)
    return cos, sin, dint, qdec, kdec, cdec


def _deinterleave_qk(w_in_l):
    w = w_in_l[:, :C_RV].reshape(D_MODEL, 2 * RET_HEADS, RET_DK // 2, 2)
    return jnp.swapaxes(w, 2, 3).reshape(D_MODEL, C_RV)


def _router_split(router_group_l, router_expert_l):
    rw = jnp.concatenate(
        [router_expert_l, router_group_l,
         jnp.zeros((D_MODEL, LANES - N_EXPERTS - N_GROUPS), F32)], axis=1)
    hi = rw.astype(BF16)
    lo = (rw - hi.astype(F32)).astype(BF16)
    return hi, lo


def kernel(x_prompt, x_sample, state_ret, state_hgrn, meta_tokens, mix_norm, w_in, hg_lb_logits, hg_norm,
           w_ret_branch, w_hg_branch, w_out, ffn_norm, router_group, router_expert, w_gate, w_up, w_down,
           final_norm):
    depth = w_in.shape[0]
    bp, seq, d = x_prompt.shape
    bs, dec, _ = x_sample.shape
    assert bp == 1 and dec == CHUNK and d == D_MODEL and seq % CHUNK == 0

    n_real = N_META + seq
    n_sample = bs * dec
    n_pad = (-(n_real + n_sample)) % PROJ_TM
    while (n_pad + n_real) % CHUNK:
        n_pad += PROJ_TM
    n_prompt = n_pad + n_real
    assert n_pad > 0
    t = n_prompt + n_sample
    npc = n_prompt // CHUNK
    n_blocks = -(-(2 * t) // ROW_BLOCK) + N_EXPERTS + 1

    x = jnp.concatenate([
        jnp.zeros((n_pad, d), F32),
        meta_tokens.astype(F32),
        x_prompt[0],
        x_sample.reshape(n_sample, d),
    ], axis=0)

    tables = _mixer_tables(npc, -(n_pad + N_META))

    prob = jax.nn.softmax(hg_lb_logits.astype(F32), axis=0)
    cum = jnp.cumsum(prob, axis=0)
    lb_all = cum - cum[0:1]

    assert state_ret.dtype == F32 and state_hgrn.dtype == F32 and w_in.dtype == F32
    states = [jnp.zeros((depth, 1) + state_ret.shape[2:], F32), jnp.zeros((depth, 1) + state_hgrn.shape[2:], F32),
              jnp.zeros(state_ret.shape, F32), jnp.zeros(state_hgrn.shape, F32)]
    for l in range(depth):
        lb = lb_all[l]
        lbt = jnp.concatenate(
            [jnp.log(lb)[None], jnp.log1p(-lb)[None], (1.0 - lb)[None], jnp.zeros((5, d), F32)], axis=0)
        p = _proj(l, x, mix_norm[l][None], _deinterleave_qk(w_in[l, :, :C_RV]), w_in)
        ro, ho, *states = _mixer(l, p, tables, lbt, hg_norm[l][None].astype(F32),
                                 state_ret, state_hgrn, npc, states)
        rw_hi, rw_lo = _router_split(router_group[l], router_expert[l])
        x1, h2, route, counts = _post(
            p, ro, ho, x, w_ret_branch[l].astype(BF16), w_hg_branch[l].astype(BF16),
            w_out[l].astype(BF16), ffn_norm[l][None], rw_hi, rw_lo)
        ids = route[:, 0:2].astype(jnp.int32)
        ranks = route[:, 4:6].astype(jnp.int32)
        codes = (ids * 65536 + ranks).reshape(-1)
        slot_tok, bexp, nused, pstart = _plan(codes, counts[0].astype(jnp.int32), n_blocks)
        yb = _experts(l, bexp, nused, slot_tok, h2, w_gate, w_up, w_down)
        if l < depth - 1:
            x, = _combine(codes, pstart, route, x1, yb, final_norm[None])
        else:
            y_prompt, y_sample = _combine(codes, pstart, route, x1, yb, final_norm[None],
                                          final_rows=(n_prompt - seq, n_prompt))

    ret_prompt, hgrn_prompt, ret_sample, hgrn_sample = states
    return (y_prompt.reshape(bp, seq, d), y_sample.reshape(bs, dec, d),
            ret_prompt, hgrn_prompt, ret_sample, hgrn_sample)
```

```python
import functools

import jax
import jax.numpy as jnp
from jax import lax
from jax.experimental import pallas as pl
from jax.experimental.pallas import tpu as pltpu

F32 = jnp.float32
BF16 = jnp.bfloat16

D_MODEL = 1024
CHUNK = 64
N_META = 16
RET_HEADS = 4
RET_DK = 256
RET_DV = 512
HG_HEADS = 8
HG_DK = 128
HG_DV = 128
N_GROUPS = 4
EXPERTS_PER_GROUP = 8
N_EXPERTS = 32
D_EXPERT = 512
ROPE_BASE = 10000.0
EPS = 1e-6
PAST_LEN = 4096
LOG2_E = 1.4426950408889634
MAX_SUBBLOCK_DROP_LOG2 = 100.0

C_RQ, C_RK, C_RV, C_RG = 0, 1024, 2048, 4096
C_HQ, C_HF, C_HI, C_HG = 6144, 7168, 8192, 9216
C_GATES = 10240
PROJ_WIDTH = 12288
MIX_WIDTH = C_GATES

TOKEN_TILE = 256
PROJ_TN = 2048
PROJ_TM = 768
PROJ_PRECISE_TN = 1024
ROW_BLOCK = 256
COMBINE_TILE = 128
SUB = 16
HALF = 64
PROMPT_CHUNK = 128
LANES = 128
TILE_ROWS = 8
VMEM_LIMIT = 56 * 1024 * 1024


def _sigmoid(x):
    return 1.0 / (1.0 + jnp.exp(-x))


def _dot(a, b):
    return jnp.dot(a, b, preferred_element_type=F32)


def _dot_nt(a, b):
    return lax.dot_general(a, b, (((1,), (1,)), ((), ())), preferred_element_type=F32)


def _dot_tn(a, b):
    return lax.dot_general(a, b, (((0,), (0,)), ((), ())), preferred_element_type=F32)


def _split(a):
    hi = a.astype(BF16)
    return hi, (a - hi.astype(F32)).astype(BF16)


def _mm(dot, a, b, precise):
    if not precise:
        return dot(a.astype(BF16), b.astype(BF16))
    ah, al = _split(a)
    bh, bl = _split(b)
    return dot(ah, bh) + (dot(ah, bl) + dot(al, bh))


def _mm_w(a, w_hi, w_lo, precise):
    if not precise:
        return _dot(a.astype(BF16), w_hi)
    ah, al = _split(a)
    return _dot(ah, w_hi) + (_dot(ah, w_lo) + _dot(al, w_hi))


def _proj_kernel(x_ref, g_ref, wqk_ref, w_ref, o_ref, wbf_ref):
    j = pl.program_id(0)
    i = pl.program_id(1)

    @pl.when((i == 0) & (j == 0))
    def _():
        wbf_ref[...] = wqk_ref[...].astype(BF16)

    @pl.when((i == 0) & (j > 0))
    def _():
        wbf_ref[...] = w_ref[...].astype(BF16)

    x = x_ref[...]
    xn = x * lax.rsqrt(jnp.mean(x * x, axis=-1, keepdims=True) + EPS) * g_ref[...]
    o_ref[...] = _dot(xn.astype(BF16), wbf_ref[...])


def _proj(layer, x, gain, w_qk, w_in):
    t = x.shape[0]
    tm, tn = PROJ_TM, PROJ_TN
    assert t % tm == 0
    return pl.pallas_call(
        _proj_kernel,
        grid=(PROJ_WIDTH // tn, t // tm),
        in_specs=[
            pl.BlockSpec((tm, D_MODEL), lambda j, i: (i, 0)),
            pl.BlockSpec((1, D_MODEL), lambda j, i: (0, 0)),
            pl.BlockSpec((D_MODEL, tn), lambda j, i: (0, 0)),
            pl.BlockSpec((None, D_MODEL, tn), lambda j, i: (layer, 0, j)),
        ],
        out_specs=pl.BlockSpec((tm, tn), lambda j, i: (i, j)),
        out_shape=jax.ShapeDtypeStruct((t, PROJ_WIDTH), F32),
        scratch_shapes=[pltpu.VMEM((D_MODEL, tn), BF16)],
        compiler_params=pltpu.CompilerParams(
            dimension_semantics=("arbitrary", "arbitrary"), vmem_limit_bytes=VMEM_LIMIT),
        name="proj",
    )(x, gain, w_qk, w_in)


def _proj_precise_kernel(x_ref, g_ref, wqk_ref, w_ref, o_ref, whi_ref, wlo_ref):
    j = pl.program_id(0)
    i = pl.program_id(1)

    @pl.when((i == 0) & (j < C_RV // PROJ_PRECISE_TN))
    def _():
        hi, lo = _split(wqk_ref[...])
        whi_ref[...] = hi
        wlo_ref[...] = lo

    @pl.when((i == 0) & (j >= C_RV // PROJ_PRECISE_TN))
    def _():
        hi, lo = _split(w_ref[...])
        whi_ref[...] = hi
        wlo_ref[...] = lo

    x = x_ref[...]
    xn = x * lax.rsqrt(jnp.mean(x * x, axis=-1, keepdims=True) + EPS) * g_ref[...]
    o_ref[...] = _mm_w(xn, whi_ref[...], wlo_ref[...], True)


def _proj_precise(layer, x, first_row, n_rows, gain, w_qk, w_in):
    tm, tn = TOKEN_TILE, PROJ_PRECISE_TN
    assert first_row % tm == 0 and n_rows % tm == 0
    first_tile = first_row // tm
    n_qk = C_RV // tn
    return pl.pallas_call(
        _proj_precise_kernel,
        grid=(PROJ_WIDTH // tn, n_rows // tm),
        in_specs=[
            pl.BlockSpec((tm, D_MODEL), lambda j, i: (first_tile + i, 0)),
            pl.BlockSpec((1, D_MODEL), lambda j, i: (0, 0)),
            pl.BlockSpec((D_MODEL, tn), lambda j, i: (0, jnp.minimum(j, n_qk - 1))),
            pl.BlockSpec((None, D_MODEL, tn), lambda j, i: (layer, 0, j)),
        ],
        out_specs=pl.BlockSpec((tm, tn), lambda j, i: (i, j)),
        out_shape=jax.ShapeDtypeStruct((n_rows, PROJ_WIDTH), F32),
        scratch_shapes=[pltpu.VMEM((D_MODEL, tn), BF16), pltpu.VMEM((D_MODEL, tn), BF16)],
        compiler_params=pltpu.CompilerParams(
            dimension_semantics=("arbitrary", "arbitrary"), vmem_limit_bytes=VMEM_LIMIT),
        name="proj_precise",
    )(x, gain, w_qk, w_in)


def _mixer_kernel(chunk, is_sample, precise,
                  p_ref, cos_ref, sin_ref, dint_ref, qdec_ref, kdec_ref, cdec_ref,
                  lbt_ref, hgain_ref, *rest):
    if is_sample:
        sr_in_ref, sh_in_ref = rest[:2]
        rest = rest[2:]
    (ro_ref, ho_ref, sr_out_ref, sh_out_ref,
     sr_scr, sh_scr, hq_scr, hb_scr, hk_scr, ho_scr) = rest[2:]
    c = pl.program_id(0)
    half = RET_DK // 2

    if is_sample:
        sr_scr[...] = sr_in_ref[...]
        for h in range(HG_HEADS):
            sh_scr[h] = sh_in_ref[h].T
    else:
        @pl.when(c == 0)
        def _():
            sr_scr[...] = jnp.zeros_like(sr_scr)
            sh_scr[...] = jnp.zeros_like(sh_scr)

    cos = cos_ref[...]
    sin = sin_ref[...]
    scores_v, to_state = [], []
    for h in range(RET_HEADS):
        q = p_ref[:, C_RQ + h * RET_DK:C_RQ + (h + 1) * RET_DK]
        q1, q2 = q[:, :half], q[:, half:]
        qr = jnp.concatenate([q1 * cos - q2 * sin, q1 * sin + q2 * cos], axis=1)
        to_state.append(qr * qdec_ref[:, h * RET_DK:(h + 1) * RET_DK])
        scores_v.append(qr)
    for h in range(RET_HEADS):
        k = p_ref[:, C_RK + h * RET_DK:C_RK + (h + 1) * RET_DK]
        k1, k2 = k[:, :half], k[:, half:]
        kr = jnp.concatenate([k1 * cos - k2 * sin, k1 * sin + k2 * cos], axis=1) * (RET_DK ** -0.5)
        to_state.append(kr * kdec_ref[:, h * RET_DK:(h + 1) * RET_DK])
        scores_v[h] = _mm(_dot_nt, scores_v[h], kr, precise) * dint_ref[h]
    pr = lax.broadcasted_iota(jnp.int32, (RET_DK, RET_DK), 0)
    pc = lax.broadcasted_iota(jnp.int32, (RET_DK, RET_DK), 1)
    src_lane = jnp.where(pr < half, 2 * pr, 2 * (pr - half) + 1)
    perm = jnp.where(pc == src_lane, 1.0, 0.0)
    natural = _mm(_dot, jnp.concatenate(to_state, axis=0), perm, precise)
    for h in range(RET_HEADS):
        v = p_ref[:, C_RV + h * RET_DV:C_RV + (h + 1) * RET_DV]
        s = sr_scr[h]
        qd = natural[h * chunk:(h + 1) * chunk]
        kd = natural[(RET_HEADS + h) * chunk:(RET_HEADS + h + 1) * chunk]
        o = _mm(_dot, scores_v[h], v, precise) + _mm(_dot, qd, s, precise)
        sr_scr[h] = cdec_ref[h] * s + _mm(_dot_tn, kd, v, precise)
        on = o * lax.rsqrt(jnp.mean(o * o, axis=-1, keepdims=True) + EPS)
        g = p_ref[:, C_RG + h * RET_DV:C_RG + (h + 1) * RET_DV]
        ro_ref[:, h * RET_DV:(h + 1) * RET_DV] = (on * (g * _sigmoid(g))).astype(ro_ref.dtype)

    hf = p_ref[:, C_HF:C_HF + D_MODEL]
    log_lb = lbt_ref[0:1, :]
    log1m_lb = lbt_ref[1:2, :]
    one_m_lb = lbt_ref[2:3, :]
    log_sig = jnp.minimum(hf, 0.0) - jnp.log(1.0 + jnp.exp(-jnp.abs(hf)))
    b_ = log1m_lb + log_sig
    log_f = jnp.maximum(log_lb, b_) + jnp.log(1.0 + jnp.exp(-jnp.abs(log_lb - b_)))
    k_in = one_m_lb * _sigmoid(-hf)
    hq = p_ref[:, C_HQ:C_HQ + D_MODEL]
    q_all = hq * _sigmoid(hq)

    tr = lax.broadcasted_iota(jnp.int32, (chunk, chunk), 0)
    tc = lax.broadcasted_iota(jnp.int32, (chunk, chunk), 1)
    tri = jnp.where(tr >= tc, 1.0, 0.0).astype(BF16)
    f_hi = log_f.astype(BF16)
    r1 = log_f - f_hi.astype(F32)
    f_mid = r1.astype(BF16)
    f_lo = (r1 - f_mid.astype(F32)).astype(BF16)
    hb_scr[...] = (_dot(tri, f_hi) + _dot(tri, f_mid) + _dot(tri, f_lo)) * LOG2_E
    hq_scr[...] = q_all
    hk_scr[...] = k_in

    n_sub = HALF // SUB
    ri = lax.broadcasted_iota(jnp.int32, (HALF, HALF), 0)
    ci = lax.broadcasted_iota(jnp.int32, (HALF, HALF), 1)
    row = lax.broadcasted_iota(jnp.int32, (HALF, HG_DK), 0)
    col_sub = lax.broadcasted_iota(jnp.int32, (SUB, HALF), 1)
    hgain = hgain_ref[...]

    drops = [hb_scr[SUB * ib:SUB * ib + 1, :] - hb_scr[SUB * (ib + 1) - 1:SUB * (ib + 1), :]
             for ib in range(chunk // SUB)]
    max_drop = jnp.max(jnp.concatenate(drops, axis=0))
    factorise_ok = max_drop <= MAX_SUBBLOCK_DROP_LOG2

    def half_scores(q, k, b, factorised):
        b_end = [b[SUB * (jb + 1) - 1:SUB * (jb + 1), :] for jb in range(n_sub)]
        b_end_rows = jnp.concatenate(
            [jnp.broadcast_to(b_end[jb], (SUB, HG_DK)) for jb in range(n_sub)], axis=0)
        k_hat = k * jnp.exp2(b_end_rows - b)
        lhs, rhs = [], []
        for jb in range(n_sub - 1):
            q_hat = q * jnp.exp2(b - b_end[jb])
            lhs.append(jnp.where(row >= SUB * (jb + 1), q_hat, 0.0))
            rhs.append(jnp.where((row >= SUB * jb) & (row < SUB * (jb + 1)), k_hat, 0.0))
        a_off = _mm(_dot_nt, jnp.concatenate(lhs, axis=1), jnp.concatenate(rhs, axis=1), precise)
        if factorised:
            b_first = jnp.concatenate(
                [jnp.broadcast_to(b[SUB * ib:SUB * ib + 1, :], (SUB, HG_DK)) for ib in range(n_sub)], axis=0)
            q_t = q * jnp.exp2(b - b_first)
            k_t = k * jnp.exp2(b_first - b)
            same_block = (ri // SUB) == (ci // SUB)
            a_diag = jnp.where(same_block & (ri >= ci), _mm(_dot_nt, q_t, k_t, precise), 0.0)
        else:
            diag = []
            for ib in range(n_sub):
                r0 = SUB * ib
                b_i = b[r0:r0 + SUB, :]
                q_i = q[r0:r0 + SUB, :]
                blk = jnp.zeros((SUB, HALF), F32)
                for j in range(SUB):
                    b_j = b[r0 + j:r0 + j + 1, :]
                    k_j = k[r0 + j:r0 + j + 1, :]
                    sc = jnp.sum(jnp.exp2(b_i - b_j) * q_i * k_j, axis=-1, keepdims=True)
                    blk = jnp.where(col_sub == r0 + j, sc, blk)
                diag.append(blk)
            a_diag = jnp.where(ri >= ci, jnp.concatenate(diag, axis=0), 0.0)
        return a_off + a_diag

    def hgrn_head(h, factorised):
        sl = slice(h * HG_DK, (h + 1) * HG_DK)
        q = hq_scr[:, sl]
        b = hb_scr[:, sl]
        k = hk_scr[:, sl]
        vb = p_ref[:, C_HI + h * HG_DV:C_HI + (h + 1) * HG_DV]
        if factorised:
            st = sh_scr[h]
            o = _mm(_dot_nt, q * jnp.exp2(b), st, precise)
            ho_scr[:, sl] = o
            b_last = b[chunk - 1:chunk, :]
            kd = k * jnp.exp2(b_last - b)
            sh_scr[h] = st * jnp.exp2(b_last) + _mm(_dot_tn, vb, kd, precise)
        else:
            o = ho_scr[:, sl]
        intra = []
        for hh in range(chunk // HALF):
            lo, hi = hh * HALF, (hh + 1) * HALF
            a = half_scores(q[lo:hi], k[lo:hi], b[lo:hi], factorised)
            if hh == 0:
                intra.append(_mm(_dot, a, vb[lo:hi], precise))
            else:
                b_mid = b[lo - 1:lo, :]
                q_c = q[lo:hi] * jnp.exp2(b[lo:hi] - b_mid)
                k_c = k[lo - HALF:lo] * jnp.exp2(b_mid - b[lo - HALF:lo])
                both = jnp.concatenate([_mm(_dot_nt, q_c, k_c, precise), a], axis=1)
                intra.append(_mm(_dot, both, vb[lo - HALF:hi], precise))
        o = o + jnp.concatenate(intra, axis=0)
        on = o * lax.rsqrt(jnp.mean(o * o, axis=-1, keepdims=True) + EPS) * hgain
        g = p_ref[:, C_HG + h * HG_DV:C_HG + (h + 1) * HG_DV]
        ho_ref[:, h * HG_DV:(h + 1) * HG_DV] = (on * (g * _sigmoid(g))).astype(ho_ref.dtype)

    for h in range(HG_HEADS):
        hgrn_head(h, True)

    @pl.when(jnp.logical_not(factorise_ok))
    def _():
        for h in range(HG_HEADS):
            hgrn_head(h, False)

    def emit_states():
        sr_out_ref[...] = sr_scr[...]
        for h in range(HG_HEADS):
            sh_out_ref[h] = sh_scr[h].T

    if is_sample:
        emit_states()
    else:
        pl.when(c == pl.num_programs(0) - 1)(emit_states)


def _mixer(layer, p, chunk, first_row, n_rows, rotary, decays, lbt, hgain, state_out_prev, states_in=None,
           precise=False):
    assert chunk in (HALF, 2 * HALF) and first_row % chunk == 0 and n_rows % chunk == 0
    is_sample = states_in is not None
    first_chunk, n_chunks = first_row // chunk, n_rows // chunk
    cos, sin = rotary
    dint, qdec, kdec, cdec = decays
    rot_idx = (lambda c: (0, 0)) if is_sample else (lambda c: (c, 0))
    seq_idx = (lambda c: (layer, c, 0, 0, 0)) if is_sample else (lambda c: (layer, 0, 0, 0, 0))
    const2 = lambda c: (0, 0)
    ret_blk = (None, None, RET_HEADS, RET_DK, RET_DV)
    hg_blk = (None, None, HG_HEADS, HG_DK, HG_DV)
    in_specs = [
        pl.BlockSpec((chunk, MIX_WIDTH), lambda c: (first_chunk + c, 0)),
        pl.BlockSpec((chunk, LANES), rot_idx),
        pl.BlockSpec((chunk, LANES), rot_idx),
        pl.BlockSpec((RET_HEADS, chunk, chunk), lambda c: (0, 0, 0)),
        pl.BlockSpec((chunk, RET_HEADS * RET_DK), const2),
        pl.BlockSpec((chunk, RET_HEADS * RET_DK), const2),
        pl.BlockSpec(memory_space=pltpu.SMEM),
        pl.BlockSpec((8, D_MODEL), const2),
        pl.BlockSpec((1, HG_DV), const2),
    ]
    args = [p, cos, sin, dint, qdec, kdec, cdec, lbt, hgain]
    if is_sample:
        in_specs += [pl.BlockSpec(ret_blk, seq_idx), pl.BlockSpec(hg_blk, seq_idx)]
        args += list(states_in)
    aliases = {}
    for n, prev in enumerate(state_out_prev):
        aliases[len(args)] = 2 + n
        in_specs.append(pl.BlockSpec(memory_space=pl.ANY))
        args.append(prev)
    return pl.pallas_call(
        functools.partial(_mixer_kernel, chunk, is_sample, precise),
        grid=(n_chunks,),
        in_specs=in_specs,
        out_specs=[
            pl.BlockSpec((chunk, RET_HEADS * RET_DV), lambda c: (c, 0)),
            pl.BlockSpec((chunk, HG_HEADS * HG_DV), lambda c: (c, 0)),
            pl.BlockSpec(ret_blk, seq_idx),
            pl.BlockSpec(hg_blk, seq_idx),
        ],
        out_shape=[
            jax.ShapeDtypeStruct((n_rows, RET_HEADS * RET_DV), F32 if precise else BF16),
            jax.ShapeDtypeStruct((n_rows, HG_HEADS * HG_DV), F32 if precise else BF16),
            jax.ShapeDtypeStruct(state_out_prev[0].shape, F32),
            jax.ShapeDtypeStruct(state_out_prev[1].shape, F32),
        ],
        input_output_aliases=aliases,
        scratch_shapes=[
            pltpu.VMEM((RET_HEADS, RET_DK, RET_DV), F32),
            pltpu.VMEM((HG_HEADS, HG_DV, HG_DK), F32),
            pltpu.VMEM((chunk, HG_HEADS * HG_DK), F32),
            pltpu.VMEM((chunk, HG_HEADS * HG_DK), F32),
            pltpu.VMEM((chunk, HG_HEADS * HG_DK), F32),
            pltpu.VMEM((chunk, HG_HEADS * HG_DV), F32),
        ],
        compiler_params=pltpu.CompilerParams(
            dimension_semantics=("arbitrary",), vmem_limit_bytes=VMEM_LIMIT),
        name="mixer_sample" if is_sample else "mixer_prompt",
    )(*args)


def _post_kernel(n_prompt_tiles, precise_samples, gp_ref, rop_ref, ros_ref, hop_ref, hos_ref, x_ref,
                 wr_ref, wh_ref, wo_ref, *rest):
    if precise_samples:
        gps_ref, wrl_ref, whl_ref, wol_ref = rest[:4]
        rest = rest[4:]
    fg_ref, rwh_ref, rwl_ref, x1_ref, h2_ref, route_ref, cnt_ref, run_scr = rest
    i = pl.program_id(0)
    tm = x_ref.shape[0]

    @pl.when(i == 0)
    def _():
        run_scr[...] = jnp.zeros_like(run_scr)

    def mix(gp, ro, ho, w_lo, precise):
        gate_r = _sigmoid(gp[:, 0:D_MODEL])
        gate_h = _sigmoid(gp[:, D_MODEL:2 * D_MODEL])
        merged = (gate_r * _mm_w(ro, wr_ref[...], w_lo[0], precise)
                  + gate_h * _mm_w(ho, wh_ref[...], w_lo[1], precise))
        x1_ref[...] = x_ref[...] + _mm_w(merged, wo_ref[...], w_lo[2], precise)

    @pl.when(i < n_prompt_tiles)
    def _():
        mix(gp_ref[...], rop_ref[...], hop_ref[...], (None, None, None), False)

    @pl.when(i >= n_prompt_tiles)
    def _():
        if precise_samples:
            mix(gps_ref[...], ros_ref[...], hos_ref[...], (wrl_ref[...], whl_ref[...], wol_ref[...]), True)
        else:
            mix(gp_ref[...], ros_ref[...], hos_ref[...], (None, None, None), False)

    x1 = x1_ref[...]
    h2 = x1 * lax.rsqrt(jnp.mean(x1 * x1, axis=-1, keepdims=True) + EPS) * fg_ref[...]
    for kk in range(TILE_ROWS):
        h2_ref[pl.ds(kk, tm, stride=TILE_ROWS), :] = h2[:, kk * LANES:(kk + 1) * LANES]

    h_hi = h2.astype(BF16)
    h_lo = (h2 - h_hi.astype(F32)).astype(BF16)
    logits = _dot(h_hi, rwh_ref[...]) + (_dot(h_hi, rwl_ref[...]) + _dot(h_lo, rwh_ref[...]))

    lane = lax.broadcasted_iota(jnp.int32, (tm, LANES), 1)
    lanef = lane.astype(F32)
    neg_inf = jnp.float32(-jnp.inf)
    big = jnp.float32(1e9)
    is_g = (lane >= N_EXPERTS) & (lane < N_EXPERTS + N_GROUPS)
    gl = jnp.where(is_g, logits, neg_inf)
    gmax = jnp.max(gl, axis=-1, keepdims=True)
    gidx = jnp.min(jnp.where(gl == gmax, lanef, big), axis=-1, keepdims=True) - N_EXPERTS
    g_w = 1.0 / jnp.sum(jnp.exp(gl - gmax), axis=-1, keepdims=True)
    lane_group = jnp.floor(lanef * (1.0 / EXPERTS_PER_GROUP))
    in_group = (lane < N_EXPERTS) & (lane_group == gidx)
    el = jnp.where(in_group, logits, neg_inf)
    m1 = jnp.max(el, axis=-1, keepdims=True)
    i1 = jnp.min(jnp.where(el == m1, lanef, big), axis=-1, keepdims=True)
    el2 = jnp.where(lanef == i1, neg_inf, el)
    m2 = jnp.max(el2, axis=-1, keepdims=True)
    i2 = jnp.min(jnp.where(el2 == m2, lanef, big), axis=-1, keepdims=True)
    tt = jnp.exp(m2 - m1)
    w1 = g_w / (1.0 + tt)
    w2 = g_w * tt / (1.0 + tt)

    oh1 = lanef == i1
    oh2 = lanef == i2
    e_cnt = jnp.where(oh1, 1.0, 0.0) + jnp.where(oh2, 1.0, 0.0)
    ri = lax.broadcasted_iota(jnp.int32, (tm, tm), 0)
    ci = lax.broadcasted_iota(jnp.int32, (tm, tm), 1)
    strict = jnp.where(ri > ci, 1.0, 0.0).astype(BF16)
    prefix = _dot(strict, e_cnt.astype(BF16)) + run_scr[0:1, :]
    r1 = jnp.sum(jnp.where(oh1, prefix, 0.0), axis=-1, keepdims=True)
    r2 = jnp.sum(jnp.where(oh2, prefix, 0.0), axis=-1, keepdims=True)
    run_scr[0:1, :] = run_scr[0:1, :] + jnp.sum(e_cnt, axis=0, keepdims=True)
    cnt_ref[...] = run_scr[...]

    route = jnp.where(lane == 0, i1, 0.0)
    route = jnp.where(lane == 1, i2, route)
    route = jnp.where(lane == 2, w1, route)
    route = jnp.where(lane == 3, w2, route)
    route = jnp.where(lane == 4, r1, route)
    route = jnp.where(lane == 5, r2, route)
    route_ref[...] = route


def _post(p, ro_p, ro_s, ho_p, ho_s, x, wr, wh, wo, fgain, rw_hi, rw_lo, precise=None):
    t = x.shape[0]
    tm = TOKEN_TILE
    assert ro_p.shape[0] % tm == 0 and ro_s.shape[0] % tm == 0
    npt = ro_p.shape[0] // tm
    prompt_idx = lambda i: (jnp.minimum(i, npt - 1), 0)
    sample_idx = lambda i: (jnp.maximum(i - npt, 0), 0)
    gate_block = C_GATES // (2 * D_MODEL)
    const2 = lambda i: (0, 0)
    in_specs = [
        pl.BlockSpec((tm, 2 * D_MODEL), lambda i: (i, gate_block)),
        pl.BlockSpec((tm, RET_HEADS * RET_DV), prompt_idx),
        pl.BlockSpec((tm, RET_HEADS * RET_DV), sample_idx),
        pl.BlockSpec((tm, HG_HEADS * HG_DV), prompt_idx),
        pl.BlockSpec((tm, HG_HEADS * HG_DV), sample_idx),
        pl.BlockSpec((tm, D_MODEL), lambda i: (i, 0)),
        pl.BlockSpec((RET_HEADS * RET_DV, D_MODEL), const2),
        pl.BlockSpec((HG_HEADS * HG_DV, D_MODEL), const2),
        pl.BlockSpec((D_MODEL, D_MODEL), const2),
    ]
    args = [p, ro_p, ro_s, ho_p, ho_s, x, wr, wh, wo]
    if precise is not None:
        in_specs += [
            pl.BlockSpec((tm, 2 * D_MODEL), lambda i: (jnp.maximum(i - npt, 0), gate_block)),
            pl.BlockSpec((RET_HEADS * RET_DV, D_MODEL), const2),
            pl.BlockSpec((HG_HEADS * HG_DV, D_MODEL), const2),
            pl.BlockSpec((D_MODEL, D_MODEL), const2),
        ]
        args += list(precise)
    in_specs += [
        pl.BlockSpec((1, D_MODEL), const2),
        pl.BlockSpec((D_MODEL, LANES), const2),
        pl.BlockSpec((D_MODEL, LANES), const2),
    ]
    args += [fgain, rw_hi, rw_lo]
    return pl.pallas_call(
        functools.partial(_post_kernel, npt, precise is not None),
        grid=(t // tm,),
        in_specs=in_specs,
        out_specs=[
            pl.BlockSpec((tm, D_MODEL), lambda i: (i, 0)),
            pl.BlockSpec((tm * TILE_ROWS, LANES), lambda i: (i, 0)),
            pl.BlockSpec((tm, LANES), lambda i: (i, 0)),
            pl.BlockSpec((8, LANES), const2),
        ],
        out_shape=[
            jax.ShapeDtypeStruct((t, D_MODEL), F32),
            jax.ShapeDtypeStruct((t * TILE_ROWS, LANES), F32),
            jax.ShapeDtypeStruct((t, LANES), F32),
            jax.ShapeDtypeStruct((8, LANES), F32),
        ],
        scratch_shapes=[pltpu.VMEM((8, LANES), F32)],
        compiler_params=pltpu.CompilerParams(
            dimension_semantics=("arbitrary",), vmem_limit_bytes=VMEM_LIMIT),
        name="post",
    )(*args)


def _plan_kernel(pairs_per_step, n_blocks,
                 dest_ref, cnt_ref, slot_ref, bexp_ref, nused_ref):
    step = pl.program_id(0)
    base = step * pairs_per_step

    @pl.when(step == 0)
    def _():
        def per_expert(e, carry):
            start, last_e = carry
            nb = (cnt_ref[e] + (ROW_BLOCK - 1)) // ROW_BLOCK
            first = start // ROW_BLOCK

            def fill(bb, _):
                bexp_ref[first + bb] = e
                return 0

            lax.fori_loop(0, nb, fill, 0)
            return start + nb * ROW_BLOCK, jnp.where(nb > 0, e, last_e)

        total, last_e = lax.fori_loop(0, N_EXPERTS, per_expert, (jnp.int32(0), jnp.int32(0)))
        n_used = total // ROW_BLOCK
        nused_ref[0] = n_used

        def fill_tail(bb, _):
            bexp_ref[bb] = last_e
            return 0

        lax.fori_loop(n_used, n_blocks, fill_tail, 0)

        def init(r, _):
            slot_ref[r] = 0
            return 0

        lax.fori_loop(0, n_blocks * ROW_BLOCK, init, 0, unroll=16)

    def place(i, _):
        slot_ref[dest_ref[base + i]] = (base + i) >> 1
        return 0

    lax.fori_loop(0, pairs_per_step, place, 0, unroll=16)


def _plan(dest, counts, n_blocks):
    n_pairs = dest.shape[0]
    pairs_per_step = 2 * TOKEN_TILE
    assert n_pairs % pairs_per_step == 0
    smem = pl.BlockSpec(memory_space=pltpu.SMEM)
    return pl.pallas_call(
        functools.partial(_plan_kernel, pairs_per_step, n_blocks),
        grid=(n_pairs // pairs_per_step,),
        in_specs=[smem, smem],
        out_specs=[smem, smem, smem],
        out_shape=[
            jax.ShapeDtypeStruct((n_blocks * ROW_BLOCK,), jnp.int32),
            jax.ShapeDtypeStruct((n_blocks,), jnp.int32),
            jax.ShapeDtypeStruct((1,), jnp.int32),
        ],
        compiler_params=pltpu.CompilerParams(dimension_semantics=("arbitrary",)),
        name="plan",
    )(dest, counts)


def _expert_kernel(bexp_ref, nused_ref, slot_ref, h2_hbm, wg_ref, wu_ref, wd_ref, yb_ref,
                   xbuf, sems, wg_s, wu_s, wd_s):
    b = pl.program_id(0)
    n_used = nused_ref[0]

    def row_copy(tok, buf_slot, r):
        return pltpu.make_async_copy(
            h2_hbm.at[pl.ds(pl.multiple_of(tok * TILE_ROWS, TILE_ROWS), TILE_ROWS)],
            xbuf.at[buf_slot, pl.ds(pl.multiple_of(r * TILE_ROWS, TILE_ROWS), TILE_ROWS)],
            sems.at[buf_slot])

    def wait_block(block):
        buf_slot = block % 2
        pltpu.make_async_copy(h2_hbm.at[pl.ds(0, ROW_BLOCK * TILE_ROWS)], xbuf.at[buf_slot],
                              sems.at[buf_slot]).wait()

    def gather(block):
        def issue(r, _):
            row_copy(slot_ref[block * ROW_BLOCK + r], block % 2, r).start()
            return 0

        lax.fori_loop(0, ROW_BLOCK, issue, 0, unroll=8)

    @pl.when(b == 0)
    def _():
        gather(b)

    prev = bexp_ref[jnp.maximum(b - 1, 0)]

    @pl.when((b == 0) | (bexp_ref[b] != prev))
    def _():
        wg_s[...] = wg_ref[...].astype(BF16)
        wu_s[...] = wu_ref[...].astype(BF16)
        wd_s[...] = wd_ref[...].astype(BF16)

    @pl.when(b < n_used)
    def _():
        buf_slot = b % 2
        wait_block(b)
        gather(b + 1)
        x = jnp.concatenate(
            [xbuf[buf_slot, pl.ds(kk, ROW_BLOCK, stride=TILE_ROWS), :] for kk in range(TILE_ROWS)],
            axis=1).astype(BF16)
        g = _dot(x, wg_s[...])
        u = _dot(x, wu_s[...])
        a = (g * _sigmoid(g) * u).astype(BF16)
        y = _dot(a, wd_s[...])
        for kk in range(TILE_ROWS):
            yb_ref[pl.ds(kk, ROW_BLOCK, stride=TILE_ROWS), :] = y[:, kk * LANES:(kk + 1) * LANES]

    @pl.when(b == n_used)
    def _():
        wait_block(b)

    @pl.when(b >= n_used)
    def _():
        yb_ref[...] = jnp.zeros_like(yb_ref)


def _experts(layer, bexp, nused, slot_tok, h2, w_gate, w_up, w_down):
    n_blocks = bexp.shape[0]
    assert slot_tok.shape[0] == n_blocks * ROW_BLOCK
    wmap = lambda b, be, nu, st: (layer, be[b], 0, 0)
    return pl.pallas_call(
        _expert_kernel,
        grid_spec=pltpu.PrefetchScalarGridSpec(
            num_scalar_prefetch=3,
            grid=(n_blocks,),
            in_specs=[
                pl.BlockSpec(memory_space=pl.ANY),
                pl.BlockSpec((None, None, D_MODEL, D_EXPERT), wmap),
                pl.BlockSpec((None, None, D_MODEL, D_EXPERT), wmap),
                pl.BlockSpec((None, None, D_EXPERT, D_MODEL), wmap),
            ],
            out_specs=pl.BlockSpec((ROW_BLOCK * TILE_ROWS, LANES), lambda b, be, nu, st: (b, 0)),
            scratch_shapes=[
                pltpu.VMEM((2, ROW_BLOCK * TILE_ROWS, LANES), F32),
                pltpu.SemaphoreType.DMA((2,)),
                pltpu.VMEM((D_MODEL, D_EXPERT), BF16),
                pltpu.VMEM((D_MODEL, D_EXPERT), BF16),
                pltpu.VMEM((D_EXPERT, D_MODEL), BF16),
            ],
        ),
        out_shape=jax.ShapeDtypeStruct((n_blocks * ROW_BLOCK * TILE_ROWS, LANES), F32),
        compiler_params=pltpu.CompilerParams(
            dimension_semantics=("arbitrary",), vmem_limit_bytes=VMEM_LIMIT),
        name="experts",
    )(bexp, nused, slot_tok, h2, w_gate, w_up, w_down)


def _combine_kernel(final_tiles, dest_ref, route_ref, x1_ref, yb_hbm, fg_ref, *rest):
    if final_tiles is not None:
        yp_ref, ys_ref, buf, sems = rest
    else:
        x2_ref, buf, sems = rest
    i = pl.program_id(0)
    tm = x1_ref.shape[0]

    def row_copy(d, buf_slot, r):
        return pltpu.make_async_copy(
            yb_hbm.at[pl.ds(pl.multiple_of(d * TILE_ROWS, TILE_ROWS), TILE_ROWS)],
            buf.at[buf_slot, pl.ds(pl.multiple_of(r * TILE_ROWS, TILE_ROWS), TILE_ROWS)],
            sems.at[buf_slot])

    def gather(tile):
        buf_slot = tile % 2
        base = tile * tm

        def issue(tt, _):
            row_copy(dest_ref[2 * (base + tt)], buf_slot, tt).start()
            row_copy(dest_ref[2 * (base + tt) + 1], buf_slot, tm + tt).start()
            return 0

        lax.fori_loop(0, tm, issue, 0, unroll=4)

    @pl.when(i == 0)
    def _():
        gather(i)

    @pl.when(i + 1 < pl.num_programs(0))
    def _():
        gather(i + 1)

    buf_slot = i % 2
    pltpu.make_async_copy(yb_hbm.at[pl.ds(0, 2 * tm * TILE_ROWS)], buf.at[buf_slot], sems.at[buf_slot]).wait()

    w1 = route_ref[:, 2:3]
    w2 = route_ref[:, 3:4]
    parts = []
    for kk in range(TILE_ROWS):
        y = (w1 * buf[buf_slot, pl.ds(kk, tm, stride=TILE_ROWS), :]
             + w2 * buf[buf_slot, pl.ds(tm * TILE_ROWS + kk, tm, stride=TILE_ROWS), :])
        parts.append(x1_ref[:, kk * LANES:(kk + 1) * LANES] + y)
    x2 = jnp.concatenate(parts, axis=1)
    if final_tiles is None:
        x2_ref[...] = x2
    else:
        first_real, n_prompt_tiles = final_tiles
        yn = x2 * lax.rsqrt(jnp.mean(x2 * x2, axis=-1, keepdims=True) + EPS) * fg_ref[...]

        @pl.when((i >= first_real) & (i < n_prompt_tiles))
        def _():
            yp_ref[...] = yn

        @pl.when(i >= n_prompt_tiles)
        def _():
            ys_ref[...] = yn


def _combine(dest, route, x1, yb, fgain, final_rows=None):
    t = x1.shape[0]
    tm = COMBINE_TILE
    row_spec = pl.BlockSpec((tm, D_MODEL), lambda i, dst: (i, 0))
    if final_rows is None:
        final_tiles = None
        out_specs = [row_spec]
        out_shape = [jax.ShapeDtypeStruct((t, D_MODEL), F32)]
    else:
        first_row, n_prompt = final_rows
        assert first_row % tm == 0 and n_prompt % tm == 0
        first_real, npt = first_row // tm, n_prompt // tm
        final_tiles = (first_real, npt)
        out_specs = [
            pl.BlockSpec((tm, D_MODEL), lambda i, dst: (jnp.clip(i - first_real, 0, npt - first_real - 1), 0)),
            pl.BlockSpec((tm, D_MODEL), lambda i, dst: (jnp.maximum(i - npt, 0), 0)),
        ]
        out_shape = [jax.ShapeDtypeStruct((n_prompt - first_row, D_MODEL), F32),
                     jax.ShapeDtypeStruct((t - n_prompt, D_MODEL), F32)]
    return pl.pallas_call(
        functools.partial(_combine_kernel, final_tiles),
        grid_spec=pltpu.PrefetchScalarGridSpec(
            num_scalar_prefetch=1,
            grid=(t // tm,),
            in_specs=[
                pl.BlockSpec((tm, LANES), lambda i, dst: (i, 0)),
                row_spec,
                pl.BlockSpec(memory_space=pl.ANY),
                pl.BlockSpec((1, D_MODEL), lambda i, dst: (0, 0)),
            ],
            out_specs=out_specs,
            scratch_shapes=[
                pltpu.VMEM((2, 2 * tm * TILE_ROWS, LANES), F32),
                pltpu.SemaphoreType.DMA((2,)),
            ],
        ),
        out_shape=out_shape,
        compiler_params=pltpu.CompilerParams(
            dimension_semantics=("arbitrary",), vmem_limit_bytes=VMEM_LIMIT),
        name="combine",
    )(dest, route, x1, yb, fgain)


def _rotary_tables(pos):
    half = RET_DK // 2
    inv_freq = 1.0 / (ROPE_BASE ** jnp.linspace(0.0, 1.0, half, dtype=F32))
    ang = pos.astype(F32)[:, None] * inv_freq[None, :]
    return jnp.cos(ang), jnp.sin(ang)


def _decay_tables(chunk):
    log_gamma = jnp.log1p(-jnp.exp2(-5.0 - jnp.arange(RET_HEADS, dtype=F32)))
    idx = jnp.arange(chunk, dtype=F32)
    rel = idx[:, None] - idx[None, :]
    causal = rel >= 0
    dint = jnp.where(causal, jnp.exp(log_gamma[:, None, None] * jnp.where(causal, rel, 0.0)), 0.0)
    qdec = jnp.exp(log_gamma[None, :] * (idx[:, None] + 1.0))
    kdec = jnp.exp(log_gamma[None, :] * (chunk - 1.0 - idx[:, None]))
    cdec = jnp.exp(log_gamma * chunk)
    qdec = jnp.repeat(qdec, RET_DK, axis=1)
    kdec = jnp.repeat(kdec, RET_DK, axis=1)
    return dint, qdec, kdec, cdec


def _deinterleave_qk(w_in_l):
    w = w_in_l[:, :C_RV].reshape(D_MODEL, 2 * RET_HEADS, RET_DK // 2, 2)
    return jnp.swapaxes(w, 2, 3).reshape(D_MODEL, C_RV)


def _split_param_kernel(w_ref, hi_ref, lo_ref):
    hi, lo = _split(w_ref[...])
    hi_ref[...] = hi
    lo_ref[...] = lo


def _split_param(w):
    rows, cols = w.shape
    tm = TOKEN_TILE
    assert rows % tm == 0 and cols % LANES == 0
    spec = pl.BlockSpec((tm, cols), lambda i: (i, 0))
    return pl.pallas_call(
        _split_param_kernel,
        grid=(rows // tm,),
        in_specs=[spec],
        out_specs=[spec, spec],
        out_shape=[jax.ShapeDtypeStruct(w.shape, BF16)] * 2,
        name="split_param",
    )(w)


def _router_split(router_group_l, router_expert_l):
    rw = jnp.concatenate(
        [router_expert_l, router_group_l,
         jnp.zeros((D_MODEL, LANES - N_EXPERTS - N_GROUPS), F32)], axis=1)
    return _split_param(rw)


def kernel(x_prompt, x_sample, state_ret, state_hgrn, meta_tokens, mix_norm, w_in, hg_lb_logits, hg_norm,
           w_ret_branch, w_hg_branch, w_out, ffn_norm, router_group, router_expert, w_gate, w_up, w_down,
           final_norm):
    depth = w_in.shape[0]
    bp, seq, d = x_prompt.shape
    bs, dec, _ = x_sample.shape
    assert bp == 1 and dec == CHUNK and d == D_MODEL and seq % CHUNK == 0

    n_real = N_META + seq
    n_sample = bs * dec
    n_pad = (-(n_real + n_sample)) % PROJ_TM
    while (n_pad + n_real) % PROMPT_CHUNK:
        n_pad += PROJ_TM
    n_prompt = n_pad + n_real
    assert n_pad > 0
    t = n_prompt + n_sample
    n_blocks = -(-(2 * t) // ROW_BLOCK) + N_EXPERTS + 1

    x = jnp.concatenate([
        jnp.zeros((n_pad, d), F32),
        meta_tokens.astype(F32),
        x_prompt[0],
        x_sample.reshape(n_sample, d),
    ], axis=0)

    rot_prompt = _rotary_tables(jnp.arange(n_prompt) - (n_pad + N_META))
    rot_sample = _rotary_tables(PAST_LEN + jnp.arange(dec))
    dec_prompt = _decay_tables(PROMPT_CHUNK)
    dec_sample = _decay_tables(dec)

    prob = jax.nn.softmax(hg_lb_logits.astype(F32), axis=0)
    cum = jnp.cumsum(prob, axis=0)
    lb_all = cum - cum[0:1]

    assert state_ret.dtype == F32 and state_hgrn.dtype == F32 and w_in.dtype == F32
    states = [jnp.zeros((depth, 1) + state_ret.shape[2:], F32), jnp.zeros((depth, 1) + state_hgrn.shape[2:], F32),
              jnp.zeros(state_ret.shape, F32), jnp.zeros(state_hgrn.shape, F32)]
    for l in range(depth):
        lb = lb_all[l]
        lbt = jnp.concatenate(
            [jnp.log(lb)[None], jnp.log1p(-lb)[None], (1.0 - lb)[None], jnp.zeros((5, d), F32)], axis=0)
        w_qk = _deinterleave_qk(w_in[l, :, :C_RV])
        p = _proj(l, x, mix_norm[l][None], w_qk, w_in)
        hgain = hg_norm[l][None].astype(F32)
        ro_p, ho_p, ret_p, hg_p = _mixer(l, p, PROMPT_CHUNK, 0, n_prompt, rot_prompt, dec_prompt, lbt, hgain,
                                         states[:2])
        precise = l < depth - 1
        branch_w = [_split_param(w[l]) for w in (w_ret_branch, w_hg_branch, w_out)]
        if precise:
            p_s = _proj_precise(l, x, n_prompt, n_sample, mix_norm[l][None], w_qk, w_in)
            ro_s, ho_s, ret_s, hg_s = _mixer(l, p_s, dec, 0, n_sample, rot_sample, dec_sample, lbt, hgain,
                                             states[2:], states_in=(state_ret, state_hgrn), precise=True)
            post_precise = (p_s,) + tuple(lo for _, lo in branch_w)
        else:
            ro_s, ho_s, ret_s, hg_s = _mixer(l, p, dec, n_prompt, n_sample, rot_sample, dec_sample, lbt, hgain,
                                             states[2:], states_in=(state_ret, state_hgrn))
            post_precise = None
        states = [ret_p, hg_p, ret_s, hg_s]
        rw_hi, rw_lo = _router_split(router_group[l], router_expert[l])
        x1, h2, route, counts = _post(
            p, ro_p, ro_s, ho_p, ho_s, x, branch_w[0][0], branch_w[1][0], branch_w[2][0],
            ffn_norm[l][None], rw_hi, rw_lo, precise=post_precise)
        ids = route[:, 0:2].astype(jnp.int32)
        ranks = route[:, 4:6].astype(jnp.int32)
        cnt = counts[0, :N_EXPERTS].astype(jnp.int32)
        padded = (cnt + (ROW_BLOCK - 1)) // ROW_BLOCK * ROW_BLOCK
        pstart = jnp.cumsum(padded) - padded
        dest = (pstart[ids] + ranks).reshape(-1)
        slot_tok, bexp, nused = _plan(dest, cnt, n_blocks)
        yb = _experts(l, bexp, nused, slot_tok, h2, w_gate, w_up, w_down)
        if l < depth - 1:
            x, = _combine(dest, route, x1, yb, final_norm[None])
        else:
            y_prompt, y_sample = _combine(dest, route, x1, yb, final_norm[None],
                                          final_rows=(n_prompt - seq, n_prompt))

    ret_prompt, hgrn_prompt, ret_sample, hgrn_sample = states
    return (y_prompt.reshape(bp, seq, d), y_sample.reshape(bs, dec, d),
            ret_prompt, hgrn_prompt, ret_sample, hgrn_sample)
```

```python
import functools

import jax
import jax.numpy as jnp
from jax import lax
from jax.experimental import pallas as pl
from jax.experimental.pallas import tpu as pltpu

F32 = jnp.float32
BF16 = jnp.bfloat16

D_MODEL = 1024
CHUNK = 64
N_META = 16
RET_HEADS = 4
RET_DK = 256
RET_DV = 512
HG_HEADS = 8
HG_DK = 128
HG_DV = 128
N_GROUPS = 4
EXPERTS_PER_GROUP = 8
N_EXPERTS = 32
D_EXPERT = 512
ROPE_BASE = 10000.0
EPS = 1e-6
PAST_LEN = 4096
LOG2_E = 1.4426950408889634
MAX_SUBBLOCK_DROP_LOG2 = 100.0

C_RQ, C_RK, C_RV, C_RG = 0, 1024, 2048, 4096
C_HQ, C_HF, C_HI, C_HG = 6144, 7168, 8192, 9216
C_GATES = 10240
PROJ_WIDTH = 12288
MIX_WIDTH = C_GATES

TOKEN_TILE = 256
PROJ_TN = 2048
PROJ_TM = 768
PROJ_PRECISE_TN = 1024
ROW_BLOCK = 256
COMBINE_TILE = 128
SUB = 16
HALF = 64
PROMPT_CHUNK = 128
LANES = 128
TILE_ROWS = 8
VMEM_LIMIT = 56 * 1024 * 1024


def _sigmoid(x):
    return 1.0 / (1.0 + jnp.exp(-x))


def _dot(a, b):
    return jnp.dot(a, b, preferred_element_type=F32)


def _dot_nt(a, b):
    return lax.dot_general(a, b, (((1,), (1,)), ((), ())), preferred_element_type=F32)


def _dot_tn(a, b):
    return lax.dot_general(a, b, (((0,), (0,)), ((), ())), preferred_element_type=F32)


def _split(a):
    hi = a.astype(BF16)
    return hi, (a - hi.astype(F32)).astype(BF16)


def _mm(dot, a, b, precise):
    if not precise:
        return dot(a.astype(BF16), b.astype(BF16))
    ah, al = _split(a)
    bh, bl = _split(b)
    return dot(ah, bh) + (dot(ah, bl) + dot(al, bh))


def _mm_w(a, w_hi, w_lo, precise):
    if not precise:
        return _dot(a.astype(BF16), w_hi)
    ah, al = _split(a)
    return _dot(ah, w_hi) + (_dot(ah, w_lo) + _dot(al, w_hi))


def _proj_kernel(x_ref, g_ref, wqk_ref, w_ref, o_ref, wbf_ref):
    j = pl.program_id(0)
    i = pl.program_id(1)

    @pl.when((i == 0) & (j == 0))
    def _():
        wbf_ref[...] = wqk_ref[...].astype(BF16)

    @pl.when((i == 0) & (j > 0))
    def _():
        wbf_ref[...] = w_ref[...].astype(BF16)

    x = x_ref[...]
    xn = x * lax.rsqrt(jnp.mean(x * x, axis=-1, keepdims=True) + EPS) * g_ref[...]
    o_ref[...] = _dot(xn.astype(BF16), wbf_ref[...])


def _proj(layer, x, gain, w_qk, w_in):
    t = x.shape[0]
    tm, tn = PROJ_TM, PROJ_TN
    assert t % tm == 0
    return pl.pallas_call(
        _proj_kernel,
        grid=(PROJ_WIDTH // tn, t // tm),
        in_specs=[
            pl.BlockSpec((tm, D_MODEL), lambda j, i: (i, 0)),
            pl.BlockSpec((1, D_MODEL), lambda j, i: (0, 0)),
            pl.BlockSpec((D_MODEL, tn), lambda j, i: (0, 0)),
            pl.BlockSpec((None, D_MODEL, tn), lambda j, i: (layer, 0, j)),
        ],
        out_specs=pl.BlockSpec((tm, tn), lambda j, i: (i, j)),
        out_shape=jax.ShapeDtypeStruct((t, PROJ_WIDTH), F32),
        scratch_shapes=[pltpu.VMEM((D_MODEL, tn), BF16)],
        compiler_params=pltpu.CompilerParams(
            dimension_semantics=("arbitrary", "arbitrary"), vmem_limit_bytes=VMEM_LIMIT),
        name="proj",
    )(x, gain, w_qk, w_in)


def _proj_precise_kernel(x_ref, g_ref, wqk_ref, w_ref, o_ref, whi_ref, wlo_ref):
    j = pl.program_id(0)
    i = pl.program_id(1)

    @pl.when((i == 0) & (j < C_RV // PROJ_PRECISE_TN))
    def _():
        hi, lo = _split(wqk_ref[...])
        whi_ref[...] = hi
        wlo_ref[...] = lo

    @pl.when((i == 0) & (j >= C_RV // PROJ_PRECISE_TN))
    def _():
        hi, lo = _split(w_ref[...])
        whi_ref[...] = hi
        wlo_ref[...] = lo

    x = x_ref[...]
    xn = x * lax.rsqrt(jnp.mean(x * x, axis=-1, keepdims=True) + EPS) * g_ref[...]
    o_ref[...] = _mm_w(xn, whi_ref[...], wlo_ref[...], True)


def _proj_precise(layer, x, first_row, n_rows, gain, w_qk, w_in):
    tm, tn = TOKEN_TILE, PROJ_PRECISE_TN
    assert first_row % tm == 0 and n_rows % tm == 0
    first_tile = first_row // tm
    n_qk = C_RV // tn
    return pl.pallas_call(
        _proj_precise_kernel,
        grid=(PROJ_WIDTH // tn, n_rows // tm),
        in_specs=[
            pl.BlockSpec((tm, D_MODEL), lambda j, i: (first_tile + i, 0)),
            pl.BlockSpec((1, D_MODEL), lambda j, i: (0, 0)),
            pl.BlockSpec((D_MODEL, tn), lambda j, i: (0, jnp.minimum(j, n_qk - 1))),
            pl.BlockSpec((None, D_MODEL, tn), lambda j, i: (layer, 0, j)),
        ],
        out_specs=pl.BlockSpec((tm, tn), lambda j, i: (i, j)),
        out_shape=jax.ShapeDtypeStruct((n_rows, PROJ_WIDTH), F32),
        scratch_shapes=[pltpu.VMEM((D_MODEL, tn), BF16), pltpu.VMEM((D_MODEL, tn), BF16)],
        compiler_params=pltpu.CompilerParams(
            dimension_semantics=("arbitrary", "arbitrary"), vmem_limit_bytes=VMEM_LIMIT),
        name="proj_precise",
    )(x, gain, w_qk, w_in)


def _mixer_kernel(chunk, is_sample, precise,
                  p_ref, cos_ref, sin_ref, dint_ref, qdec_ref, kdec_ref, cdec_ref,
                  lbt_ref, hgain_ref, *rest):
    if is_sample:
        sr_in_ref, sh_in_ref = rest[:2]
        rest = rest[2:]
    (ro_ref, ho_ref, sr_out_ref, sh_out_ref,
     sr_scr, sh_scr, hq_scr, hb_scr, hk_scr, ho_scr) = rest[2:]
    c = pl.program_id(0)
    half = RET_DK // 2

    if is_sample:
        sr_scr[...] = sr_in_ref[...]
        for h in range(HG_HEADS):
            sh_scr[h] = sh_in_ref[h].T
    else:
        @pl.when(c == 0)
        def _():
            sr_scr[...] = jnp.zeros_like(sr_scr)
            sh_scr[...] = jnp.zeros_like(sh_scr)

    cos = cos_ref[...]
    sin = sin_ref[...]
    scores_v, to_state = [], []
    for h in range(RET_HEADS):
        q = p_ref[:, C_RQ + h * RET_DK:C_RQ + (h + 1) * RET_DK]
        q1, q2 = q[:, :half], q[:, half:]
        qr = jnp.concatenate([q1 * cos - q2 * sin, q1 * sin + q2 * cos], axis=1)
        to_state.append(qr * qdec_ref[:, h * RET_DK:(h + 1) * RET_DK])
        scores_v.append(qr)
    for h in range(RET_HEADS):
        k = p_ref[:, C_RK + h * RET_DK:C_RK + (h + 1) * RET_DK]
        k1, k2 = k[:, :half], k[:, half:]
        kr = jnp.concatenate([k1 * cos - k2 * sin, k1 * sin + k2 * cos], axis=1) * (RET_DK ** -0.5)
        to_state.append(kr * kdec_ref[:, h * RET_DK:(h + 1) * RET_DK])
        scores_v[h] = _mm(_dot_nt, scores_v[h], kr, precise) * dint_ref[h]
    pr = lax.broadcasted_iota(jnp.int32, (RET_DK, RET_DK), 0)
    pc = lax.broadcasted_iota(jnp.int32, (RET_DK, RET_DK), 1)
    src_lane = jnp.where(pr < half, 2 * pr, 2 * (pr - half) + 1)
    perm = jnp.where(pc == src_lane, 1.0, 0.0)
    natural = _mm(_dot, jnp.concatenate(to_state, axis=0), perm, precise)
    for h in range(RET_HEADS):
        v = p_ref[:, C_RV + h * RET_DV:C_RV + (h + 1) * RET_DV]
        s = sr_scr[h]
        qd = natural[h * chunk:(h + 1) * chunk]
        kd = natural[(RET_HEADS + h) * chunk:(RET_HEADS + h + 1) * chunk]
        o = _mm(_dot, scores_v[h], v, precise) + _mm(_dot, qd, s, precise)
        sr_scr[h] = cdec_ref[h] * s + _mm(_dot_tn, kd, v, precise)
        on = o * lax.rsqrt(jnp.mean(o * o, axis=-1, keepdims=True) + EPS)
        g = p_ref[:, C_RG + h * RET_DV:C_RG + (h + 1) * RET_DV]
        ro_ref[:, h * RET_DV:(h + 1) * RET_DV] = (on * (g * _sigmoid(g))).astype(ro_ref.dtype)

    hf = p_ref[:, C_HF:C_HF + D_MODEL]
    log_lb = lbt_ref[0:1, :]
    log1m_lb = lbt_ref[1:2, :]
    one_m_lb = lbt_ref[2:3, :]
    log_sig = jnp.minimum(hf, 0.0) - jnp.log(1.0 + jnp.exp(-jnp.abs(hf)))
    b_ = log1m_lb + log_sig
    log_f = jnp.maximum(log_lb, b_) + jnp.log(1.0 + jnp.exp(-jnp.abs(log_lb - b_)))
    k_in = one_m_lb * _sigmoid(-hf)
    hq = p_ref[:, C_HQ:C_HQ + D_MODEL]
    q_all = hq * _sigmoid(hq)

    tr = lax.broadcasted_iota(jnp.int32, (chunk, chunk), 0)
    tc = lax.broadcasted_iota(jnp.int32, (chunk, chunk), 1)
    tri = jnp.where(tr >= tc, 1.0, 0.0).astype(BF16)
    f_hi = log_f.astype(BF16)
    r1 = log_f - f_hi.astype(F32)
    f_mid = r1.astype(BF16)
    f_lo = (r1 - f_mid.astype(F32)).astype(BF16)
    hb_scr[...] = (_dot(tri, f_hi) + _dot(tri, f_mid) + _dot(tri, f_lo)) * LOG2_E
    hq_scr[...] = q_all
    hk_scr[...] = k_in

    n_sub = HALF // SUB
    ri = lax.broadcasted_iota(jnp.int32, (HALF, HALF), 0)
    ci = lax.broadcasted_iota(jnp.int32, (HALF, HALF), 1)
    row = lax.broadcasted_iota(jnp.int32, (HALF, HG_DK), 0)
    col_sub = lax.broadcasted_iota(jnp.int32, (SUB, HALF), 1)
    hgain = hgain_ref[...]

    drops = [hb_scr[SUB * ib:SUB * ib + 1, :] - hb_scr[SUB * (ib + 1) - 1:SUB * (ib + 1), :]
             for ib in range(chunk // SUB)]
    max_drop = jnp.max(jnp.concatenate(drops, axis=0))
    factorise_ok = max_drop <= MAX_SUBBLOCK_DROP_LOG2

    def half_scores(q, k, b, factorised):
        b_end = [b[SUB * (jb + 1) - 1:SUB * (jb + 1), :] for jb in range(n_sub)]
        b_end_rows = jnp.concatenate(
            [jnp.broadcast_to(b_end[jb], (SUB, HG_DK)) for jb in range(n_sub)], axis=0)
        k_hat = k * jnp.exp2(b_end_rows - b)
        lhs, rhs = [], []
        for jb in range(n_sub - 1):
            q_hat = q * jnp.exp2(b - b_end[jb])
            lhs.append(jnp.where(row >= SUB * (jb + 1), q_hat, 0.0))
            rhs.append(jnp.where((row >= SUB * jb) & (row < SUB * (jb + 1)), k_hat, 0.0))
        a_off = _mm(_dot_nt, jnp.concatenate(lhs, axis=1), jnp.concatenate(rhs, axis=1), precise)
        if factorised:
            b_first = jnp.concatenate(
                [jnp.broadcast_to(b[SUB * ib:SUB * ib + 1, :], (SUB, HG_DK)) for ib in range(n_sub)], axis=0)
            q_t = q * jnp.exp2(b - b_first)
            k_t = k * jnp.exp2(b_first - b)
            same_block = (ri // SUB) == (ci // SUB)
            a_diag = jnp.where(same_block & (ri >= ci), _mm(_dot_nt, q_t, k_t, precise), 0.0)
        else:
            diag = []
            for ib in range(n_sub):
                r0 = SUB * ib
                b_i = b[r0:r0 + SUB, :]
                q_i = q[r0:r0 + SUB, :]
                blk = jnp.zeros((SUB, HALF), F32)
                for j in range(SUB):
                    b_j = b[r0 + j:r0 + j + 1, :]
                    k_j = k[r0 + j:r0 + j + 1, :]
                    sc = jnp.sum(jnp.exp2(b_i - b_j) * q_i * k_j, axis=-1, keepdims=True)
                    blk = jnp.where(col_sub == r0 + j, sc, blk)
                diag.append(blk)
            a_diag = jnp.where(ri >= ci, jnp.concatenate(diag, axis=0), 0.0)
        return a_off + a_diag

    def hgrn_head(h, factorised):
        sl = slice(h * HG_DK, (h + 1) * HG_DK)
        q = hq_scr[:, sl]
        b = hb_scr[:, sl]
        k = hk_scr[:, sl]
        vb = p_ref[:, C_HI + h * HG_DV:C_HI + (h + 1) * HG_DV]
        if factorised:
            st = sh_scr[h]
            o = _mm(_dot_nt, q * jnp.exp2(b), st, precise)
            ho_scr[:, sl] = o
            b_last = b[chunk - 1:chunk, :]
            kd = k * jnp.exp2(b_last - b)
            sh_scr[h] = st * jnp.exp2(b_last) + _mm(_dot_tn, vb, kd, precise)
        else:
            o = ho_scr[:, sl]
        intra = []
        for hh in range(chunk // HALF):
            lo, hi = hh * HALF, (hh + 1) * HALF
            a = half_scores(q[lo:hi], k[lo:hi], b[lo:hi], factorised)
            if hh == 0:
                intra.append(_mm(_dot, a, vb[lo:hi], precise))
            else:
                b_mid = b[lo - 1:lo, :]
                q_c = q[lo:hi] * jnp.exp2(b[lo:hi] - b_mid)
                k_c = k[lo - HALF:lo] * jnp.exp2(b_mid - b[lo - HALF:lo])
                both = jnp.concatenate([_mm(_dot_nt, q_c, k_c, precise), a], axis=1)
                intra.append(_mm(_dot, both, vb[lo - HALF:hi], precise))
        o = o + jnp.concatenate(intra, axis=0)
        on = o * lax.rsqrt(jnp.mean(o * o, axis=-1, keepdims=True) + EPS) * hgain
        g = p_ref[:, C_HG + h * HG_DV:C_HG + (h + 1) * HG_DV]
        ho_ref[:, h * HG_DV:(h + 1) * HG_DV] = (on * (g * _sigmoid(g))).astype(ho_ref.dtype)

    for h in range(HG_HEADS):
        hgrn_head(h, True)

    @pl.when(jnp.logical_not(factorise_ok))
    def _():
        for h in range(HG_HEADS):
            hgrn_head(h, False)

    def emit_states():
        sr_out_ref[...] = sr_scr[...]
        for h in range(HG_HEADS):
            sh_out_ref[h] = sh_scr[h].T

    if is_sample:
        emit_states()
    else:
        pl.when(c == pl.num_programs(0) - 1)(emit_states)


def _mixer(layer, p, chunk, first_row, n_rows, rotary, decays, lbt, hgain, state_out_prev, states_in=None,
           precise=False):
    assert chunk in (HALF, 2 * HALF) and first_row % chunk == 0 and n_rows % chunk == 0
    is_sample = states_in is not None
    first_chunk, n_chunks = first_row // chunk, n_rows // chunk
    cos, sin = rotary
    dint, qdec, kdec, cdec = decays
    rot_idx = (lambda c: (0, 0)) if is_sample else (lambda c: (c, 0))
    seq_idx = (lambda c: (layer, c, 0, 0, 0)) if is_sample else (lambda c: (layer, 0, 0, 0, 0))
    const2 = lambda c: (0, 0)
    ret_blk = (None, None, RET_HEADS, RET_DK, RET_DV)
    hg_blk = (None, None, HG_HEADS, HG_DK, HG_DV)
    in_specs = [
        pl.BlockSpec((chunk, MIX_WIDTH), lambda c: (first_chunk + c, 0)),
        pl.BlockSpec((chunk, LANES), rot_idx),
        pl.BlockSpec((chunk, LANES), rot_idx),
        pl.BlockSpec((RET_HEADS, chunk, chunk), lambda c: (0, 0, 0)),
        pl.BlockSpec((chunk, RET_HEADS * RET_DK), const2),
        pl.BlockSpec((chunk, RET_HEADS * RET_DK), const2),
        pl.BlockSpec(memory_space=pltpu.SMEM),
        pl.BlockSpec((8, D_MODEL), const2),
        pl.BlockSpec((1, HG_DV), const2),
    ]
    args = [p, cos, sin, dint, qdec, kdec, cdec, lbt, hgain]
    if is_sample:
        in_specs += [pl.BlockSpec(ret_blk, seq_idx), pl.BlockSpec(hg_blk, seq_idx)]
        args += list(states_in)
    aliases = {}
    for n, prev in enumerate(state_out_prev):
        aliases[len(args)] = 2 + n
        in_specs.append(pl.BlockSpec(memory_space=pl.ANY))
        args.append(prev)
    return pl.pallas_call(
        functools.partial(_mixer_kernel, chunk, is_sample, precise),
        grid=(n_chunks,),
        in_specs=in_specs,
        out_specs=[
            pl.BlockSpec((chunk, RET_HEADS * RET_DV), lambda c: (c, 0)),
            pl.BlockSpec((chunk, HG_HEADS * HG_DV), lambda c: (c, 0)),
            pl.BlockSpec(ret_blk, seq_idx),
            pl.BlockSpec(hg_blk, seq_idx),
        ],
        out_shape=[
            jax.ShapeDtypeStruct((n_rows, RET_HEADS * RET_DV), F32 if precise else BF16),
            jax.ShapeDtypeStruct((n_rows, HG_HEADS * HG_DV), F32 if precise else BF16),
            jax.ShapeDtypeStruct(state_out_prev[0].shape, F32),
            jax.ShapeDtypeStruct(state_out_prev[1].shape, F32),
        ],
        input_output_aliases=aliases,
        scratch_shapes=[
            pltpu.VMEM((RET_HEADS, RET_DK, RET_DV), F32),
            pltpu.VMEM((HG_HEADS, HG_DV, HG_DK), F32),
            pltpu.VMEM((chunk, HG_HEADS * HG_DK), F32),
            pltpu.VMEM((chunk, HG_HEADS * HG_DK), F32),
            pltpu.VMEM((chunk, HG_HEADS * HG_DK), F32),
            pltpu.VMEM((chunk, HG_HEADS * HG_DV), F32),
        ],
        compiler_params=pltpu.CompilerParams(
            dimension_semantics=("arbitrary",), vmem_limit_bytes=VMEM_LIMIT),
        name="mixer_sample" if is_sample else "mixer_prompt",
    )(*args)


def _post_kernel(n_prompt_tiles, precise_samples, gp_ref, rop_ref, ros_ref, hop_ref, hos_ref, x_ref,
                 wr_ref, wh_ref, wo_ref, *rest):
    if precise_samples:
        gps_ref, wrl_ref, whl_ref, wol_ref = rest[:4]
        rest = rest[4:]
    fg_ref, rwh_ref, rwl_ref, x1_ref, h2_ref, route_ref, cnt_ref, run_scr = rest
    i = pl.program_id(0)
    tm = x_ref.shape[0]

    @pl.when(i == 0)
    def _():
        run_scr[...] = jnp.zeros_like(run_scr)

    def mix(gp, ro, ho, w_lo, precise):
        gate_r = _sigmoid(gp[:, 0:D_MODEL])
        gate_h = _sigmoid(gp[:, D_MODEL:2 * D_MODEL])
        merged = (gate_r * _mm_w(ro, wr_ref[...], w_lo[0], precise)
                  + gate_h * _mm_w(ho, wh_ref[...], w_lo[1], precise))
        x1_ref[...] = x_ref[...] + _mm_w(merged, wo_ref[...], w_lo[2], precise)

    @pl.when(i < n_prompt_tiles)
    def _():
        mix(gp_ref[...], rop_ref[...], hop_ref[...], (None, None, None), False)

    @pl.when(i >= n_prompt_tiles)
    def _():
        if precise_samples:
            mix(gps_ref[...], ros_ref[...], hos_ref[...], (wrl_ref[...], whl_ref[...], wol_ref[...]), True)
        else:
            mix(gp_ref[...], ros_ref[...], hos_ref[...], (None, None, None), False)

    x1 = x1_ref[...]
    h2 = x1 * lax.rsqrt(jnp.mean(x1 * x1, axis=-1, keepdims=True) + EPS) * fg_ref[...]
    for kk in range(TILE_ROWS):
        h2_ref[pl.ds(kk, tm, stride=TILE_ROWS), :] = h2[:, kk * LANES:(kk + 1) * LANES]

    h_hi = h2.astype(BF16)
    h_lo = (h2 - h_hi.astype(F32)).astype(BF16)
    logits = _dot(h_hi, rwh_ref[...]) + (_dot(h_hi, rwl_ref[...]) + _dot(h_lo, rwh_ref[...]))

    lane = lax.broadcasted_iota(jnp.int32, (tm, LANES), 1)
    lanef = lane.astype(F32)
    neg_inf = jnp.float32(-jnp.inf)
    big = jnp.float32(1e9)
    is_g = (lane >= N_EXPERTS) & (lane < N_EXPERTS + N_GROUPS)
    gl = jnp.where(is_g, logits, neg_inf)
    gmax = jnp.max(gl, axis=-1, keepdims=True)
    gidx = jnp.min(jnp.where(gl == gmax, lanef, big), axis=-1, keepdims=True) - N_EXPERTS
    g_w = 1.0 / jnp.sum(jnp.exp(gl - gmax), axis=-1, keepdims=True)
    lane_group = jnp.floor(lanef * (1.0 / EXPERTS_PER_GROUP))
    in_group = (lane < N_EXPERTS) & (lane_group == gidx)
    el = jnp.where(in_group, logits, neg_inf)
    m1 = jnp.max(el, axis=-1, keepdims=True)
    i1 = jnp.min(jnp.where(el == m1, lanef, big), axis=-1, keepdims=True)
    el2 = jnp.where(lanef == i1, neg_inf, el)
    m2 = jnp.max(el2, axis=-1, keepdims=True)
    i2 = jnp.min(jnp.where(el2 == m2, lanef, big), axis=-1, keepdims=True)
    tt = jnp.exp(m2 - m1)
    w1 = g_w / (1.0 + tt)
    w2 = g_w * tt / (1.0 + tt)

    oh1 = lanef == i1
    oh2 = lanef == i2
    e_cnt = jnp.where(oh1, 1.0, 0.0) + jnp.where(oh2, 1.0, 0.0)
    ri = lax.broadcasted_iota(jnp.int32, (tm, tm), 0)
    ci = lax.broadcasted_iota(jnp.int32, (tm, tm), 1)
    strict = jnp.where(ri > ci, 1.0, 0.0).astype(BF16)
    prefix = _dot(strict, e_cnt.astype(BF16)) + run_scr[0:1, :]
    r1 = jnp.sum(jnp.where(oh1, prefix, 0.0), axis=-1, keepdims=True)
    r2 = jnp.sum(jnp.where(oh2, prefix, 0.0), axis=-1, keepdims=True)
    run_scr[0:1, :] = run_scr[0:1, :] + jnp.sum(e_cnt, axis=0, keepdims=True)
    cnt_ref[...] = run_scr[...]

    route = jnp.where(lane == 0, i1, 0.0)
    route = jnp.where(lane == 1, i2, route)
    route = jnp.where(lane == 2, w1, route)
    route = jnp.where(lane == 3, w2, route)
    route = jnp.where(lane == 4, r1, route)
    route = jnp.where(lane == 5, r2, route)
    route_ref[...] = route


def _post(p, ro_p, ro_s, ho_p, ho_s, x, wr, wh, wo, fgain, rw_hi, rw_lo, precise=None):
    t = x.shape[0]
    tm = TOKEN_TILE
    assert ro_p.shape[0] % tm == 0 and ro_s.shape[0] % tm == 0
    npt = ro_p.shape[0] // tm
    prompt_idx = lambda i: (jnp.minimum(i, npt - 1), 0)
    sample_idx = lambda i: (jnp.maximum(i - npt, 0), 0)
    gate_block = C_GATES // (2 * D_MODEL)
    const2 = lambda i: (0, 0)
    in_specs = [
        pl.BlockSpec((tm, 2 * D_MODEL), lambda i: (i, gate_block)),
        pl.BlockSpec((tm, RET_HEADS * RET_DV), prompt_idx),
        pl.BlockSpec((tm, RET_HEADS * RET_DV), sample_idx),
        pl.BlockSpec((tm, HG_HEADS * HG_DV), prompt_idx),
        pl.BlockSpec((tm, HG_HEADS * HG_DV), sample_idx),
        pl.BlockSpec((tm, D_MODEL), lambda i: (i, 0)),
        pl.BlockSpec((RET_HEADS * RET_DV, D_MODEL), const2),
        pl.BlockSpec((HG_HEADS * HG_DV, D_MODEL), const2),
        pl.BlockSpec((D_MODEL, D_MODEL), const2),
    ]
    args = [p, ro_p, ro_s, ho_p, ho_s, x, wr, wh, wo]
    if precise is not None:
        in_specs += [
            pl.BlockSpec((tm, 2 * D_MODEL), lambda i: (jnp.maximum(i - npt, 0), gate_block)),
            pl.BlockSpec((RET_HEADS * RET_DV, D_MODEL), const2),
            pl.BlockSpec((HG_HEADS * HG_DV, D_MODEL), const2),
            pl.BlockSpec((D_MODEL, D_MODEL), const2),
        ]
        args += list(precise)
    in_specs += [
        pl.BlockSpec((1, D_MODEL), const2),
        pl.BlockSpec((D_MODEL, LANES), const2),
        pl.BlockSpec((D_MODEL, LANES), const2),
    ]
    args += [fgain, rw_hi, rw_lo]
    return pl.pallas_call(
        functools.partial(_post_kernel, npt, precise is not None),
        grid=(t // tm,),
        in_specs=in_specs,
        out_specs=[
            pl.BlockSpec((tm, D_MODEL), lambda i: (i, 0)),
            pl.BlockSpec((tm * TILE_ROWS, LANES), lambda i: (i, 0)),
            pl.BlockSpec((tm, LANES), lambda i: (i, 0)),
            pl.BlockSpec((8, LANES), const2),
        ],
        out_shape=[
            jax.ShapeDtypeStruct((t, D_MODEL), F32),
            jax.ShapeDtypeStruct((t * TILE_ROWS, LANES), F32),
            jax.ShapeDtypeStruct((t, LANES), F32),
            jax.ShapeDtypeStruct((8, LANES), F32),
        ],
        scratch_shapes=[pltpu.VMEM((8, LANES), F32)],
        compiler_params=pltpu.CompilerParams(
            dimension_semantics=("arbitrary",), vmem_limit_bytes=VMEM_LIMIT),
        name="post",
    )(*args)


def _plan_kernel(pairs_per_step, n_blocks,
                 codes_ref, cnt_ref, slot_ref, dest_ref, bexp_ref, nused_ref, pstart_scr):
    step = pl.program_id(0)
    base = step * pairs_per_step

    @pl.when(step == 0)
    def _():
        def per_expert(e, carry):
            start, last_e = carry
            pstart_scr[e] = start
            nb = (cnt_ref[e] + (ROW_BLOCK - 1)) // ROW_BLOCK
            first = start // ROW_BLOCK

            def fill(bb, _):
                bexp_ref[first + bb] = e
                return 0

            lax.fori_loop(0, nb, fill, 0)
            return start + nb * ROW_BLOCK, jnp.where(nb > 0, e, last_e)

        total, last_e = lax.fori_loop(0, N_EXPERTS, per_expert, (jnp.int32(0), jnp.int32(0)))
        n_used = total // ROW_BLOCK
        nused_ref[0] = n_used

        def fill_tail(bb, _):
            bexp_ref[bb] = last_e
            return 0

        lax.fori_loop(n_used, n_blocks, fill_tail, 0)

        def init(r, _):
            slot_ref[r] = 0
            return 0

        lax.fori_loop(0, n_blocks * ROW_BLOCK, init, 0, unroll=16)

    def place(i, _):
        code = codes_ref[base + i]
        d = pstart_scr[code >> 16] + (code & 0xFFFF)
        dest_ref[base + i] = d
        slot_ref[d] = (base + i) >> 1
        return 0

    lax.fori_loop(0, pairs_per_step, place, 0, unroll=8)


def _plan(codes, counts, n_blocks):
    n_pairs = codes.shape[0]
    pairs_per_step = 2 * TOKEN_TILE
    assert n_pairs % pairs_per_step == 0 and n_pairs < 65536
    smem = pl.BlockSpec(memory_space=pltpu.SMEM)
    return pl.pallas_call(
        functools.partial(_plan_kernel, pairs_per_step, n_blocks),
        grid=(n_pairs // pairs_per_step,),
        in_specs=[smem, smem],
        out_specs=[smem, smem, smem, smem],
        out_shape=[
            jax.ShapeDtypeStruct((n_blocks * ROW_BLOCK,), jnp.int32),
            jax.ShapeDtypeStruct((n_pairs,), jnp.int32),
            jax.ShapeDtypeStruct((n_blocks,), jnp.int32),
            jax.ShapeDtypeStruct((1,), jnp.int32),
        ],
        scratch_shapes=[pltpu.SMEM((N_EXPERTS,), jnp.int32)],
        compiler_params=pltpu.CompilerParams(dimension_semantics=("arbitrary",)),
        name="plan",
    )(codes, counts)


def _expert_kernel(bexp_ref, nused_ref, slot_ref, h2_hbm, wg_ref, wu_ref, wd_ref, yb_ref,
                   xbuf, sems, wg_s, wu_s, wd_s):
    b = pl.program_id(0)
    n_used = nused_ref[0]

    def row_copy(tok, buf_slot, r):
        return pltpu.make_async_copy(
            h2_hbm.at[pl.ds(pl.multiple_of(tok * TILE_ROWS, TILE_ROWS), TILE_ROWS)],
            xbuf.at[buf_slot, pl.ds(pl.multiple_of(r * TILE_ROWS, TILE_ROWS), TILE_ROWS)],
            sems.at[buf_slot])

    def wait_block(block):
        buf_slot = block % 2
        pltpu.make_async_copy(h2_hbm.at[pl.ds(0, ROW_BLOCK * TILE_ROWS)], xbuf.at[buf_slot],
                              sems.at[buf_slot]).wait()

    def gather(block):
        def issue(r, _):
            row_copy(slot_ref[block * ROW_BLOCK + r], block % 2, r).start()
            return 0

        lax.fori_loop(0, ROW_BLOCK, issue, 0, unroll=8)

    @pl.when(b == 0)
    def _():
        gather(b)

    prev = bexp_ref[jnp.maximum(b - 1, 0)]

    @pl.when((b == 0) | (bexp_ref[b] != prev))
    def _():
        wg_s[...] = wg_ref[...].astype(BF16)
        wu_s[...] = wu_ref[...].astype(BF16)
        wd_s[...] = wd_ref[...].astype(BF16)

    @pl.when(b < n_used)
    def _():
        buf_slot = b % 2
        wait_block(b)
        gather(b + 1)
        x = jnp.concatenate(
            [xbuf[buf_slot, pl.ds(kk, ROW_BLOCK, stride=TILE_ROWS), :] for kk in range(TILE_ROWS)],
            axis=1).astype(BF16)
        g = _dot(x, wg_s[...])
        u = _dot(x, wu_s[...])
        a = (g * _sigmoid(g) * u).astype(BF16)
        y = _dot(a, wd_s[...])
        for kk in range(TILE_ROWS):
            yb_ref[pl.ds(kk, ROW_BLOCK, stride=TILE_ROWS), :] = y[:, kk * LANES:(kk + 1) * LANES]

    @pl.when(b == n_used)
    def _():
        wait_block(b)

    @pl.when(b >= n_used)
    def _():
        yb_ref[...] = jnp.zeros_like(yb_ref)


def _experts(layer, bexp, nused, slot_tok, h2, w_gate, w_up, w_down):
    n_blocks = bexp.shape[0]
    assert slot_tok.shape[0] == n_blocks * ROW_BLOCK
    wmap = lambda b, be, nu, st: (layer, be[b], 0, 0)
    return pl.pallas_call(
        _expert_kernel,
        grid_spec=pltpu.PrefetchScalarGridSpec(
            num_scalar_prefetch=3,
            grid=(n_blocks,),
            in_specs=[
                pl.BlockSpec(memory_space=pl.ANY),
                pl.BlockSpec((None, None, D_MODEL, D_EXPERT), wmap),
                pl.BlockSpec((None, None, D_MODEL, D_EXPERT), wmap),
                pl.BlockSpec((None, None, D_EXPERT, D_MODEL), wmap),
            ],
            out_specs=pl.BlockSpec((ROW_BLOCK * TILE_ROWS, LANES), lambda b, be, nu, st: (b, 0)),
            scratch_shapes=[
                pltpu.VMEM((2, ROW_BLOCK * TILE_ROWS, LANES), F32),
                pltpu.SemaphoreType.DMA((2,)),
                pltpu.VMEM((D_MODEL, D_EXPERT), BF16),
                pltpu.VMEM((D_MODEL, D_EXPERT), BF16),
                pltpu.VMEM((D_EXPERT, D_MODEL), BF16),
            ],
        ),
        out_shape=jax.ShapeDtypeStruct((n_blocks * ROW_BLOCK * TILE_ROWS, LANES), F32),
        compiler_params=pltpu.CompilerParams(
            dimension_semantics=("arbitrary",), vmem_limit_bytes=VMEM_LIMIT),
        name="experts",
    )(bexp, nused, slot_tok, h2, w_gate, w_up, w_down)


def _combine_kernel(final_tiles, dest_ref, route_ref, x1_ref, yb_hbm, fg_ref, *rest):
    if final_tiles is not None:
        yp_ref, ys_ref, buf, sems = rest
    else:
        x2_ref, buf, sems = rest
    i = pl.program_id(0)
    tm = x1_ref.shape[0]

    def row_copy(d, buf_slot, r):
        return pltpu.make_async_copy(
            yb_hbm.at[pl.ds(pl.multiple_of(d * TILE_ROWS, TILE_ROWS), TILE_ROWS)],
            buf.at[buf_slot, pl.ds(pl.multiple_of(r * TILE_ROWS, TILE_ROWS), TILE_ROWS)],
            sems.at[buf_slot])

    def gather(tile):
        buf_slot = tile % 2
        base = tile * tm

        def issue(tt, _):
            row_copy(dest_ref[2 * (base + tt)], buf_slot, tt).start()
            row_copy(dest_ref[2 * (base + tt) + 1], buf_slot, tm + tt).start()
            return 0

        lax.fori_loop(0, tm, issue, 0, unroll=4)

    @pl.when(i == 0)
    def _():
        gather(i)

    @pl.when(i + 1 < pl.num_programs(0))
    def _():
        gather(i + 1)

    buf_slot = i % 2
    pltpu.make_async_copy(yb_hbm.at[pl.ds(0, 2 * tm * TILE_ROWS)], buf.at[buf_slot], sems.at[buf_slot]).wait()

    w1 = route_ref[:, 2:3]
    w2 = route_ref[:, 3:4]
    parts = []
    for kk in range(TILE_ROWS):
        y = (w1 * buf[buf_slot, pl.ds(kk, tm, stride=TILE_ROWS), :]
             + w2 * buf[buf_slot, pl.ds(tm * TILE_ROWS + kk, tm, stride=TILE_ROWS), :])
        parts.append(x1_ref[:, kk * LANES:(kk + 1) * LANES] + y)
    x2 = jnp.concatenate(parts, axis=1)
    if final_tiles is None:
        x2_ref[...] = x2
    else:
        first_real, n_prompt_tiles = final_tiles
        yn = x2 * lax.rsqrt(jnp.mean(x2 * x2, axis=-1, keepdims=True) + EPS) * fg_ref[...]

        @pl.when((i >= first_real) & (i < n_prompt_tiles))
        def _():
            yp_ref[...] = yn

        @pl.when(i >= n_prompt_tiles)
        def _():
            ys_ref[...] = yn


def _combine(dest, route, x1, yb, fgain, final_rows=None):
    t = x1.shape[0]
    tm = COMBINE_TILE
    row_spec = pl.BlockSpec((tm, D_MODEL), lambda i, dst: (i, 0))
    if final_rows is None:
        final_tiles = None
        out_specs = [row_spec]
        out_shape = [jax.ShapeDtypeStruct((t, D_MODEL), F32)]
    else:
        first_row, n_prompt = final_rows
        assert first_row % tm == 0 and n_prompt % tm == 0
        first_real, npt = first_row // tm, n_prompt // tm
        final_tiles = (first_real, npt)
        out_specs = [
            pl.BlockSpec((tm, D_MODEL), lambda i, dst: (jnp.clip(i - first_real, 0, npt - first_real - 1), 0)),
            pl.BlockSpec((tm, D_MODEL), lambda i, dst: (jnp.maximum(i - npt, 0), 0)),
        ]
        out_shape = [jax.ShapeDtypeStruct((n_prompt - first_row, D_MODEL), F32),
                     jax.ShapeDtypeStruct((t - n_prompt, D_MODEL), F32)]
    return pl.pallas_call(
        functools.partial(_combine_kernel, final_tiles),
        grid_spec=pltpu.PrefetchScalarGridSpec(
            num_scalar_prefetch=1,
            grid=(t // tm,),
            in_specs=[
                pl.BlockSpec((tm, LANES), lambda i, dst: (i, 0)),
                row_spec,
                pl.BlockSpec(memory_space=pl.ANY),
                pl.BlockSpec((1, D_MODEL), lambda i, dst: (0, 0)),
            ],
            out_specs=out_specs,
            scratch_shapes=[
                pltpu.VMEM((2, 2 * tm * TILE_ROWS, LANES), F32),
                pltpu.SemaphoreType.DMA((2,)),
            ],
        ),
        out_shape=out_shape,
        compiler_params=pltpu.CompilerParams(
            dimension_semantics=("arbitrary",), vmem_limit_bytes=VMEM_LIMIT),
        name="combine",
    )(dest, route, x1, yb, fgain)


def _rotary_tables(pos):
    half = RET_DK // 2
    inv_freq = 1.0 / (ROPE_BASE ** jnp.linspace(0.0, 1.0, half, dtype=F32))
    ang = pos.astype(F32)[:, None] * inv_freq[None, :]
    return jnp.cos(ang), jnp.sin(ang)


def _decay_tables(chunk):
    log_gamma = jnp.log1p(-jnp.exp2(-5.0 - jnp.arange(RET_HEADS, dtype=F32)))
    idx = jnp.arange(chunk, dtype=F32)
    rel = idx[:, None] - idx[None, :]
    causal = rel >= 0
    dint = jnp.where(causal, jnp.exp(log_gamma[:, None, None] * jnp.where(causal, rel, 0.0)), 0.0)
    qdec = jnp.exp(log_gamma[None, :] * (idx[:, None] + 1.0))
    kdec = jnp.exp(log_gamma[None, :] * (chunk - 1.0 - idx[:, None]))
    cdec = jnp.exp(log_gamma * chunk)
    qdec = jnp.repeat(qdec, RET_DK, axis=1)
    kdec = jnp.repeat(kdec, RET_DK, axis=1)
    return dint, qdec, kdec, cdec


def _deinterleave_qk(w_in_l):
    w = w_in_l[:, :C_RV].reshape(D_MODEL, 2 * RET_HEADS, RET_DK // 2, 2)
    return jnp.swapaxes(w, 2, 3).reshape(D_MODEL, C_RV)


def _split_param_kernel(w_ref, hi_ref, lo_ref):
    hi, lo = _split(w_ref[...])
    hi_ref[...] = hi
    lo_ref[...] = lo


def _split_param(w):
    rows, cols = w.shape
    tm = TOKEN_TILE
    assert rows % tm == 0 and cols % LANES == 0
    spec = pl.BlockSpec((tm, cols), lambda i: (i, 0))
    return pl.pallas_call(
        _split_param_kernel,
        grid=(rows // tm,),
        in_specs=[spec],
        out_specs=[spec, spec],
        out_shape=[jax.ShapeDtypeStruct(w.shape, BF16)] * 2,
        name="split_param",
    )(w)


def _router_split(router_group_l, router_expert_l):
    rw = jnp.concatenate(
        [router_expert_l, router_group_l,
         jnp.zeros((D_MODEL, LANES - N_EXPERTS - N_GROUPS), F32)], axis=1)
    return _split_param(rw)


def kernel(x_prompt, x_sample, state_ret, state_hgrn, meta_tokens, mix_norm, w_in, hg_lb_logits, hg_norm,
           w_ret_branch, w_hg_branch, w_out, ffn_norm, router_group, router_expert, w_gate, w_up, w_down,
           final_norm):
    depth = w_in.shape[0]
    bp, seq, d = x_prompt.shape
    bs, dec, _ = x_sample.shape
    assert bp == 1 and dec == CHUNK and d == D_MODEL and seq % CHUNK == 0

    n_real = N_META + seq
    n_sample = bs * dec
    n_pad = (-(n_real + n_sample)) % PROJ_TM
    while (n_pad + n_real) % PROMPT_CHUNK:
        n_pad += PROJ_TM
    n_prompt = n_pad + n_real
    assert n_pad > 0
    t = n_prompt + n_sample
    n_blocks = -(-(2 * t) // ROW_BLOCK) + N_EXPERTS + 1

    x = jnp.concatenate([
        jnp.zeros((n_pad, d), F32),
        meta_tokens.astype(F32),
        x_prompt[0],
        x_sample.reshape(n_sample, d),
    ], axis=0)

    rot_prompt = _rotary_tables(jnp.arange(n_prompt) - (n_pad + N_META))
    rot_sample = _rotary_tables(PAST_LEN + jnp.arange(dec))
    dec_prompt = _decay_tables(PROMPT_CHUNK)
    dec_sample = _decay_tables(dec)

    prob = jax.nn.softmax(hg_lb_logits.astype(F32), axis=0)
    cum = jnp.cumsum(prob, axis=0)
    lb_all = cum - cum[0:1]

    assert state_ret.dtype == F32 and state_hgrn.dtype == F32 and w_in.dtype == F32
    states = [jnp.zeros((depth, 1) + state_ret.shape[2:], F32), jnp.zeros((depth, 1) + state_hgrn.shape[2:], F32),
              jnp.zeros(state_ret.shape, F32), jnp.zeros(state_hgrn.shape, F32)]
    for l in range(depth):
        lb = lb_all[l]
        lbt = jnp.concatenate(
            [jnp.log(lb)[None], jnp.log1p(-lb)[None], (1.0 - lb)[None], jnp.zeros((5, d), F32)], axis=0)
        w_qk = _deinterleave_qk(w_in[l, :, :C_RV])
        p = _proj(l, x, mix_norm[l][None], w_qk, w_in)
        hgain = hg_norm[l][None].astype(F32)
        ro_p, ho_p, ret_p, hg_p = _mixer(l, p, PROMPT_CHUNK, 0, n_prompt, rot_prompt, dec_prompt, lbt, hgain,
                                         states[:2])
        precise = l < depth - 1
        branch_w = [_split_param(w[l]) for w in (w_ret_branch, w_hg_branch, w_out)]
        if precise:
            p_s = _proj_precise(l, x, n_prompt, n_sample, mix_norm[l][None], w_qk, w_in)
            ro_s, ho_s, ret_s, hg_s = _mixer(l, p_s, dec, 0, n_sample, rot_sample, dec_sample, lbt, hgain,
                                             states[2:], states_in=(state_ret, state_hgrn), precise=True)
            post_precise = (p_s,) + tuple(lo for _, lo in branch_w)
        else:
            ro_s, ho_s, ret_s, hg_s = _mixer(l, p, dec, n_prompt, n_sample, rot_sample, dec_sample, lbt, hgain,
                                             states[2:], states_in=(state_ret, state_hgrn))
            post_precise = None
        states = [ret_p, hg_p, ret_s, hg_s]
        rw_hi, rw_lo = _router_split(router_group[l], router_expert[l])
        x1, h2, route, counts = _post(
            p, ro_p, ro_s, ho_p, ho_s, x, branch_w[0][0], branch_w[1][0], branch_w[2][0],
            ffn_norm[l][None], rw_hi, rw_lo, precise=post_precise)
        ids = route[:, 0:2].astype(jnp.int32)
        ranks = route[:, 4:6].astype(jnp.int32)
        codes = (ids * 65536 + ranks).reshape(-1)
        slot_tok, dest, bexp, nused = _plan(codes, counts[0, :N_EXPERTS].astype(jnp.int32), n_blocks)
        yb = _experts(l, bexp, nused, slot_tok, h2, w_gate, w_up, w_down)
        if l < depth - 1:
            x, = _combine(dest, route, x1, yb, final_norm[None])
        else:
            y_prompt, y_sample = _combine(dest, route, x1, yb, final_norm[None],
                                          final_rows=(n_prompt - seq, n_prompt))

    ret_prompt, hgrn_prompt, ret_sample, hgrn_sample = states
    return (y_prompt.reshape(bp, seq, d), y_sample.reshape(bs, dec, d),
            ret_prompt, hgrn_prompt, ret_sample, hgrn_sample)
```

```python
import functools

import jax
import jax.numpy as jnp
from jax import lax
from jax.experimental import pallas as pl
from jax.experimental.pallas import tpu as pltpu

F32 = jnp.float32
BF16 = jnp.bfloat16

D_MODEL = 1024
CHUNK = 64
N_META = 16
RET_HEADS = 4
RET_DK = 256
RET_DV = 512
HG_HEADS = 8
HG_DK = 128
HG_DV = 128
N_GROUPS = 4
EXPERTS_PER_GROUP = 8
N_EXPERTS = 32
D_EXPERT = 512
ROPE_BASE = 10000.0
EPS = 1e-6
PAST_LEN = 4096
LOG2_E = 1.4426950408889634
MAX_SUBBLOCK_DROP_LOG2 = 100.0

C_RQ, C_RK, C_RV, C_RG = 0, 1024, 2048, 4096
C_HQ, C_HF, C_HI, C_HG = 6144, 7168, 8192, 9216
C_GATES = 10240
PROJ_WIDTH = 12288
MIX_WIDTH = C_GATES

TOKEN_TILE = 256
PROJ_TN = 2048
PROJ_TM = 768
PROJ_PRECISE_TN = 1024
ROW_BLOCK = 256
COMBINE_TILE = 128
SUB = 16
HALF = 64
PROMPT_CHUNK = 128
LANES = 128
TILE_ROWS = 8
VMEM_LIMIT = 56 * 1024 * 1024


def _sigmoid(x):
    return 1.0 / (1.0 + jnp.exp(-x))


def _dot(a, b):
    return jnp.dot(a, b, preferred_element_type=F32)


def _dot_nt(a, b):
    return lax.dot_general(a, b, (((1,), (1,)), ((), ())), preferred_element_type=F32)


def _dot_tn(a, b):
    return lax.dot_general(a, b, (((0,), (0,)), ((), ())), preferred_element_type=F32)


def _split(a):
    hi = a.astype(BF16)
    return hi, (a - hi.astype(F32)).astype(BF16)


def _mm(dot, a, b, precise):
    if not precise:
        return dot(a.astype(BF16), b.astype(BF16))
    ah, al = _split(a)
    bh, bl = _split(b)
    return dot(ah, bh) + (dot(ah, bl) + dot(al, bh))


def _mm_w(a, w_hi, w_lo, precise):
    if not precise:
        return _dot(a.astype(BF16), w_hi)
    ah, al = _split(a)
    return _dot(ah, w_hi) + (_dot(ah, w_lo) + _dot(al, w_hi))


def _deinterleave_heads(w_bf):
    half = RET_DK // 2
    r = lax.broadcasted_iota(jnp.int32, (RET_DK, RET_DK), 0)
    c = lax.broadcasted_iota(jnp.int32, (RET_DK, RET_DK), 1)
    src = jnp.where(c < half, 2 * c, 2 * (c - half) + 1)
    pick = jnp.where(r == src, 1.0, 0.0).astype(BF16)
    blocks = [_dot(w_bf[:, h * RET_DK:(h + 1) * RET_DK], pick).astype(BF16)
              for h in range(w_bf.shape[1] // RET_DK)]
    return jnp.concatenate(blocks, axis=1)


def _proj_kernel(qk_tiles, x_ref, g_ref, w_ref, o_ref, wbf_ref):
    j = pl.program_id(0)
    i = pl.program_id(1)

    @pl.when((i == 0) & (j < qk_tiles))
    def _():
        wbf_ref[...] = _deinterleave_heads(w_ref[...].astype(BF16))

    @pl.when((i == 0) & (j >= qk_tiles))
    def _():
        wbf_ref[...] = w_ref[...].astype(BF16)

    x = x_ref[...]
    xn = x * lax.rsqrt(jnp.mean(x * x, axis=-1, keepdims=True) + EPS) * g_ref[...]
    o_ref[...] = _dot(xn.astype(BF16), wbf_ref[...])


def _proj(layer, x, gain, w_in):
    t = x.shape[0]
    tm, tn = PROJ_TM, PROJ_TN
    assert t % tm == 0 and C_RV % tn == 0
    return pl.pallas_call(
        functools.partial(_proj_kernel, C_RV // tn),
        grid=(PROJ_WIDTH // tn, t // tm),
        in_specs=[
            pl.BlockSpec((tm, D_MODEL), lambda j, i: (i, 0)),
            pl.BlockSpec((1, D_MODEL), lambda j, i: (0, 0)),
            pl.BlockSpec((None, D_MODEL, tn), lambda j, i: (layer, 0, j)),
        ],
        out_specs=pl.BlockSpec((tm, tn), lambda j, i: (i, j)),
        out_shape=jax.ShapeDtypeStruct((t, PROJ_WIDTH), F32),
        scratch_shapes=[pltpu.VMEM((D_MODEL, tn), BF16)],
        compiler_params=pltpu.CompilerParams(
            dimension_semantics=("arbitrary", "arbitrary"), vmem_limit_bytes=VMEM_LIMIT),
        name="proj",
    )(x, gain, w_in)


def _proj_precise_kernel(qk_tiles, x_ref, g_ref, w_ref, o_ref, whi_ref, wlo_ref):
    j = pl.program_id(0)
    i = pl.program_id(1)

    @pl.when((i == 0) & (j < qk_tiles))
    def _():
        hi, lo = _split(w_ref[...])
        whi_ref[...] = _deinterleave_heads(hi)
        wlo_ref[...] = _deinterleave_heads(lo)

    @pl.when((i == 0) & (j >= qk_tiles))
    def _():
        hi, lo = _split(w_ref[...])
        whi_ref[...] = hi
        wlo_ref[...] = lo

    x = x_ref[...]
    xn = x * lax.rsqrt(jnp.mean(x * x, axis=-1, keepdims=True) + EPS) * g_ref[...]
    o_ref[...] = _mm_w(xn, whi_ref[...], wlo_ref[...], True)


def _proj_precise(layer, x, first_row, n_rows, gain, w_in):
    tm, tn = TOKEN_TILE, PROJ_PRECISE_TN
    assert first_row % tm == 0 and n_rows % tm == 0 and C_RV % tn == 0
    first_tile = first_row // tm
    return pl.pallas_call(
        functools.partial(_proj_precise_kernel, C_RV // tn),
        grid=(PROJ_WIDTH // tn, n_rows // tm),
        in_specs=[
            pl.BlockSpec((tm, D_MODEL), lambda j, i: (first_tile + i, 0)),
            pl.BlockSpec((1, D_MODEL), lambda j, i: (0, 0)),
            pl.BlockSpec((None, D_MODEL, tn), lambda j, i: (layer, 0, j)),
        ],
        out_specs=pl.BlockSpec((tm, tn), lambda j, i: (i, j)),
        out_shape=jax.ShapeDtypeStruct((n_rows, PROJ_WIDTH), F32),
        scratch_shapes=[pltpu.VMEM((D_MODEL, tn), BF16), pltpu.VMEM((D_MODEL, tn), BF16)],
        compiler_params=pltpu.CompilerParams(
            dimension_semantics=("arbitrary", "arbitrary"), vmem_limit_bytes=VMEM_LIMIT),
        name="proj_precise",
    )(x, gain, w_in)


def _mixer_kernel(chunk, is_sample, precise,
                  p_ref, cos_ref, sin_ref, dint_ref, qdec_ref, kdec_ref, cdec_ref,
                  lbt_ref, hgain_ref, *rest):
    if is_sample:
        sr_in_ref, sh_in_ref = rest[:2]
        rest = rest[2:]
    (ro_ref, ho_ref, sr_out_ref, sh_out_ref,
     sr_scr, sh_scr, hq_scr, hb_scr, hk_scr, ho_scr) = rest[2:]
    c = pl.program_id(0)
    half = RET_DK // 2

    if is_sample:
        sr_scr[...] = sr_in_ref[...]
        for h in range(HG_HEADS):
            sh_scr[h] = sh_in_ref[h].T
    else:
        @pl.when(c == 0)
        def _():
            sr_scr[...] = jnp.zeros_like(sr_scr)
            sh_scr[...] = jnp.zeros_like(sh_scr)

    cos = cos_ref[...]
    sin = sin_ref[...]
    scores_v, to_state = [], []
    for h in range(RET_HEADS):
        q = p_ref[:, C_RQ + h * RET_DK:C_RQ + (h + 1) * RET_DK]
        q1, q2 = q[:, :half], q[:, half:]
        qr = jnp.concatenate([q1 * cos - q2 * sin, q1 * sin + q2 * cos], axis=1)
        to_state.append(qr * qdec_ref[:, h * RET_DK:(h + 1) * RET_DK])
        scores_v.append(qr)
    for h in range(RET_HEADS):
        k = p_ref[:, C_RK + h * RET_DK:C_RK + (h + 1) * RET_DK]
        k1, k2 = k[:, :half], k[:, half:]
        kr = jnp.concatenate([k1 * cos - k2 * sin, k1 * sin + k2 * cos], axis=1) * (RET_DK ** -0.5)
        to_state.append(kr * kdec_ref[:, h * RET_DK:(h + 1) * RET_DK])
        scores_v[h] = _mm(_dot_nt, scores_v[h], kr, precise) * dint_ref[h]
    pr = lax.broadcasted_iota(jnp.int32, (RET_DK, RET_DK), 0)
    pc = lax.broadcasted_iota(jnp.int32, (RET_DK, RET_DK), 1)
    src_lane = jnp.where(pr < half, 2 * pr, 2 * (pr - half) + 1)
    perm = jnp.where(pc == src_lane, 1.0, 0.0)
    natural = _mm(_dot, jnp.concatenate(to_state, axis=0), perm, precise)
    for h in range(RET_HEADS):
        v = p_ref[:, C_RV + h * RET_DV:C_RV + (h + 1) * RET_DV]
        s = sr_scr[h]
        qd = natural[h * chunk:(h + 1) * chunk]
        kd = natural[(RET_HEADS + h) * chunk:(RET_HEADS + h + 1) * chunk]
        o = _mm(_dot, scores_v[h], v, precise) + _mm(_dot, qd, s, precise)
        sr_scr[h] = cdec_ref[h] * s + _mm(_dot_tn, kd, v, precise)
        on = o * lax.rsqrt(jnp.mean(o * o, axis=-1, keepdims=True) + EPS)
        g = p_ref[:, C_RG + h * RET_DV:C_RG + (h + 1) * RET_DV]
        ro_ref[:, h * RET_DV:(h + 1) * RET_DV] = (on * (g * _sigmoid(g))).astype(ro_ref.dtype)

    hf = p_ref[:, C_HF:C_HF + D_MODEL]
    log_lb = lbt_ref[0:1, :]
    log1m_lb = lbt_ref[1:2, :]
    one_m_lb = lbt_ref[2:3, :]
    log_sig = jnp.minimum(hf, 0.0) - jnp.log(1.0 + jnp.exp(-jnp.abs(hf)))
    b_ = log1m_lb + log_sig
    log_f = jnp.maximum(log_lb, b_) + jnp.log(1.0 + jnp.exp(-jnp.abs(log_lb - b_)))
    k_in = one_m_lb * _sigmoid(-hf)
    hq = p_ref[:, C_HQ:C_HQ + D_MODEL]
    q_all = hq * _sigmoid(hq)

    tr = lax.broadcasted_iota(jnp.int32, (chunk, chunk), 0)
    tc = lax.broadcasted_iota(jnp.int32, (chunk, chunk), 1)
    tri = jnp.where(tr >= tc, 1.0, 0.0).astype(BF16)
    f_hi = log_f.astype(BF16)
    r1 = log_f - f_hi.astype(F32)
    f_mid = r1.astype(BF16)
    f_lo = (r1 - f_mid.astype(F32)).astype(BF16)
    hb_scr[...] = (_dot(tri, f_hi) + _dot(tri, f_mid) + _dot(tri, f_lo)) * LOG2_E
    hq_scr[...] = q_all
    hk_scr[...] = k_in

    n_sub = HALF // SUB
    ri = lax.broadcasted_iota(jnp.int32, (HALF, HALF), 0)
    ci = lax.broadcasted_iota(jnp.int32, (HALF, HALF), 1)
    row = lax.broadcasted_iota(jnp.int32, (HALF, HG_DK), 0)
    col_sub = lax.broadcasted_iota(jnp.int32, (SUB, HALF), 1)
    hgain = hgain_ref[...]

    drops = [hb_scr[SUB * ib:SUB * ib + 1, :] - hb_scr[SUB * (ib + 1) - 1:SUB * (ib + 1), :]
             for ib in range(chunk // SUB)]
    max_drop = jnp.max(jnp.concatenate(drops, axis=0))
    factorise_ok = max_drop <= MAX_SUBBLOCK_DROP_LOG2

    def half_scores(q, k, b, factorised):
        b_end = [b[SUB * (jb + 1) - 1:SUB * (jb + 1), :] for jb in range(n_sub)]
        b_end_rows = jnp.concatenate(
            [jnp.broadcast_to(b_end[jb], (SUB, HG_DK)) for jb in range(n_sub)], axis=0)
        k_hat = k * jnp.exp2(b_end_rows - b)
        lhs, rhs = [], []
        for jb in range(n_sub - 1):
            q_hat = q * jnp.exp2(b - b_end[jb])
            lhs.append(jnp.where(row >= SUB * (jb + 1), q_hat, 0.0))
            rhs.append(jnp.where((row >= SUB * jb) & (row < SUB * (jb + 1)), k_hat, 0.0))
        a_off = _mm(_dot_nt, jnp.concatenate(lhs, axis=1), jnp.concatenate(rhs, axis=1), precise)
        if factorised:
            b_first = jnp.concatenate(
                [jnp.broadcast_to(b[SUB * ib:SUB * ib + 1, :], (SUB, HG_DK)) for ib in range(n_sub)], axis=0)
            q_t = q * jnp.exp2(b - b_first)
            k_t = k * jnp.exp2(b_first - b)
            same_block = (ri // SUB) == (ci // SUB)
            a_diag = jnp.where(same_block & (ri >= ci), _mm(_dot_nt, q_t, k_t, precise), 0.0)
        else:
            diag = []
            for ib in range(n_sub):
                r0 = SUB * ib
                b_i = b[r0:r0 + SUB, :]
                q_i = q[r0:r0 + SUB, :]
                blk = jnp.zeros((SUB, HALF), F32)
                for j in range(SUB):
                    b_j = b[r0 + j:r0 + j + 1, :]
                    k_j = k[r0 + j:r0 + j + 1, :]
                    sc = jnp.sum(jnp.exp2(b_i - b_j) * q_i * k_j, axis=-1, keepdims=True)
                    blk = jnp.where(col_sub == r0 + j, sc, blk)
                diag.append(blk)
            a_diag = jnp.where(ri >= ci, jnp.concatenate(diag, axis=0), 0.0)
        return a_off + a_diag

    def hgrn_head(h, factorised):
        sl = slice(h * HG_DK, (h + 1) * HG_DK)
        q = hq_scr[:, sl]
        b = hb_scr[:, sl]
        k = hk_scr[:, sl]
        vb = p_ref[:, C_HI + h * HG_DV:C_HI + (h + 1) * HG_DV]
        if factorised:
            st = sh_scr[h]
            o = _mm(_dot_nt, q * jnp.exp2(b), st, precise)
            ho_scr[:, sl] = o
            b_last = b[chunk - 1:chunk, :]
            kd = k * jnp.exp2(b_last - b)
            sh_scr[h] = st * jnp.exp2(b_last) + _mm(_dot_tn, vb, kd, precise)
        else:
            o = ho_scr[:, sl]
        intra = []
        for hh in range(chunk // HALF):
            lo, hi = hh * HALF, (hh + 1) * HALF
            a = half_scores(q[lo:hi], k[lo:hi], b[lo:hi], factorised)
            if hh == 0:
                intra.append(_mm(_dot, a, vb[lo:hi], precise))
            else:
                b_mid = b[lo - 1:lo, :]
                q_c = q[lo:hi] * jnp.exp2(b[lo:hi] - b_mid)
                k_c = k[lo - HALF:lo] * jnp.exp2(b_mid - b[lo - HALF:lo])
                both = jnp.concatenate([_mm(_dot_nt, q_c, k_c, precise), a], axis=1)
                intra.append(_mm(_dot, both, vb[lo - HALF:hi], precise))
        o = o + jnp.concatenate(intra, axis=0)
        on = o * lax.rsqrt(jnp.mean(o * o, axis=-1, keepdims=True) + EPS) * hgain
        g = p_ref[:, C_HG + h * HG_DV:C_HG + (h + 1) * HG_DV]
        ho_ref[:, h * HG_DV:(h + 1) * HG_DV] = (on * (g * _sigmoid(g))).astype(ho_ref.dtype)

    for h in range(HG_HEADS):
        hgrn_head(h, True)

    @pl.when(jnp.logical_not(factorise_ok))
    def _():
        for h in range(HG_HEADS):
            hgrn_head(h, False)

    def emit_states():
        sr_out_ref[...] = sr_scr[...]
        for h in range(HG_HEADS):
            sh_out_ref[h] = sh_scr[h].T

    if is_sample:
        emit_states()
    else:
        pl.when(c == pl.num_programs(0) - 1)(emit_states)


def _mixer(layer, p, chunk, first_row, n_rows, rotary, decays, lbt, hgain, state_out_prev, states_in=None,
           precise=False):
    assert chunk in (HALF, 2 * HALF) and first_row % chunk == 0 and n_rows % chunk == 0
    is_sample = states_in is not None
    first_chunk, n_chunks = first_row // chunk, n_rows // chunk
    cos, sin = rotary
    dint, qdec, kdec, cdec = decays
    rot_idx = (lambda c: (0, 0)) if is_sample else (lambda c: (c, 0))
    seq_idx = (lambda c: (layer, c, 0, 0, 0)) if is_sample else (lambda c: (layer, 0, 0, 0, 0))
    const2 = lambda c: (0, 0)
    ret_blk = (None, None, RET_HEADS, RET_DK, RET_DV)
    hg_blk = (None, None, HG_HEADS, HG_DK, HG_DV)
    in_specs = [
        pl.BlockSpec((chunk, MIX_WIDTH), lambda c: (first_chunk + c, 0)),
        pl.BlockSpec((chunk, LANES), rot_idx),
        pl.BlockSpec((chunk, LANES), rot_idx),
        pl.BlockSpec((RET_HEADS, chunk, chunk), lambda c: (0, 0, 0)),
        pl.BlockSpec((chunk, RET_HEADS * RET_DK), const2),
        pl.BlockSpec((chunk, RET_HEADS * RET_DK), const2),
        pl.BlockSpec(memory_space=pltpu.SMEM),
        pl.BlockSpec((8, D_MODEL), const2),
        pl.BlockSpec((1, HG_DV), const2),
    ]
    args = [p, cos, sin, dint, qdec, kdec, cdec, lbt, hgain]
    if is_sample:
        in_specs += [pl.BlockSpec(ret_blk, seq_idx), pl.BlockSpec(hg_blk, seq_idx)]
        args += list(states_in)
    aliases = {}
    for n, prev in enumerate(state_out_prev):
        aliases[len(args)] = 2 + n
        in_specs.append(pl.BlockSpec(memory_space=pl.ANY))
        args.append(prev)
    return pl.pallas_call(
        functools.partial(_mixer_kernel, chunk, is_sample, precise),
        grid=(n_chunks,),
        in_specs=in_specs,
        out_specs=[
            pl.BlockSpec((chunk, RET_HEADS * RET_DV), lambda c: (c, 0)),
            pl.BlockSpec((chunk, HG_HEADS * HG_DV), lambda c: (c, 0)),
            pl.BlockSpec(ret_blk, seq_idx),
            pl.BlockSpec(hg_blk, seq_idx),
        ],
        out_shape=[
            jax.ShapeDtypeStruct((n_rows, RET_HEADS * RET_DV), F32 if precise else BF16),
            jax.ShapeDtypeStruct((n_rows, HG_HEADS * HG_DV), F32 if precise else BF16),
            jax.ShapeDtypeStruct(state_out_prev[0].shape, F32),
            jax.ShapeDtypeStruct(state_out_prev[1].shape, F32),
        ],
        input_output_aliases=aliases,
        scratch_shapes=[
            pltpu.VMEM((RET_HEADS, RET_DK, RET_DV), F32),
            pltpu.VMEM((HG_HEADS, HG_DV, HG_DK), F32),
            pltpu.VMEM((chunk, HG_HEADS * HG_DK), F32),
            pltpu.VMEM((chunk, HG_HEADS * HG_DK), F32),
            pltpu.VMEM((chunk, HG_HEADS * HG_DK), F32),
            pltpu.VMEM((chunk, HG_HEADS * HG_DV), F32),
        ],
        compiler_params=pltpu.CompilerParams(
            dimension_semantics=("arbitrary",), vmem_limit_bytes=VMEM_LIMIT),
        name="mixer_sample" if is_sample else "mixer_prompt",
    )(*args)


def _post_kernel(n_prompt_tiles, precise_samples, gp_ref, rop_ref, ros_ref, hop_ref, hos_ref, x_ref,
                 wr_ref, wh_ref, wo_ref, *rest):
    if precise_samples:
        gps_ref, wrl_ref, whl_ref, wol_ref = rest[:4]
        rest = rest[4:]
    fg_ref, rwh_ref, rwl_ref, x1_ref, h2_ref, route_ref, cnt_ref, run_scr = rest
    i = pl.program_id(0)
    tm = x_ref.shape[0]

    @pl.when(i == 0)
    def _():
        run_scr[...] = jnp.zeros_like(run_scr)

    def mix(gp, ro, ho, w_lo, precise):
        gate_r = _sigmoid(gp[:, 0:D_MODEL])
        gate_h = _sigmoid(gp[:, D_MODEL:2 * D_MODEL])
        merged = (gate_r * _mm_w(ro, wr_ref[...], w_lo[0], precise)
                  + gate_h * _mm_w(ho, wh_ref[...], w_lo[1], precise))
        x1_ref[...] = x_ref[...] + _mm_w(merged, wo_ref[...], w_lo[2], precise)

    @pl.when(i < n_prompt_tiles)
    def _():
        mix(gp_ref[...], rop_ref[...], hop_ref[...], (None, None, None), False)

    @pl.when(i >= n_prompt_tiles)
    def _():
        if precise_samples:
            mix(gps_ref[...], ros_ref[...], hos_ref[...], (wrl_ref[...], whl_ref[...], wol_ref[...]), True)
        else:
            mix(gp_ref[...], ros_ref[...], hos_ref[...], (None, None, None), False)

    x1 = x1_ref[...]
    h2 = x1 * lax.rsqrt(jnp.mean(x1 * x1, axis=-1, keepdims=True) + EPS) * fg_ref[...]
    for kk in range(TILE_ROWS):
        h2_ref[pl.ds(kk, tm, stride=TILE_ROWS), :] = h2[:, kk * LANES:(kk + 1) * LANES]

    h_hi = h2.astype(BF16)
    h_lo = (h2 - h_hi.astype(F32)).astype(BF16)
    logits = _dot(h_hi, rwh_ref[...]) + (_dot(h_hi, rwl_ref[...]) + _dot(h_lo, rwh_ref[...]))

    lane = lax.broadcasted_iota(jnp.int32, (tm, LANES), 1)
    lanef = lane.astype(F32)
    neg_inf = jnp.float32(-jnp.inf)
    big = jnp.float32(1e9)
    is_g = (lane >= N_EXPERTS) & (lane < N_EXPERTS + N_GROUPS)
    gl = jnp.where(is_g, logits, neg_inf)
    gmax = jnp.max(gl, axis=-1, keepdims=True)
    gidx = jnp.min(jnp.where(gl == gmax, lanef, big), axis=-1, keepdims=True) - N_EXPERTS
    g_w = 1.0 / jnp.sum(jnp.exp(gl - gmax), axis=-1, keepdims=True)
    lane_group = jnp.floor(lanef * (1.0 / EXPERTS_PER_GROUP))
    in_group = (lane < N_EXPERTS) & (lane_group == gidx)
    el = jnp.where(in_group, logits, neg_inf)
    m1 = jnp.max(el, axis=-1, keepdims=True)
    i1 = jnp.min(jnp.where(el == m1, lanef, big), axis=-1, keepdims=True)
    el2 = jnp.where(lanef == i1, neg_inf, el)
    m2 = jnp.max(el2, axis=-1, keepdims=True)
    i2 = jnp.min(jnp.where(el2 == m2, lanef, big), axis=-1, keepdims=True)
    tt = jnp.exp(m2 - m1)
    w1 = g_w / (1.0 + tt)
    w2 = g_w * tt / (1.0 + tt)

    oh1 = lanef == i1
    oh2 = lanef == i2
    e_cnt = jnp.where(oh1, 1.0, 0.0) + jnp.where(oh2, 1.0, 0.0)
    ri = lax.broadcasted_iota(jnp.int32, (tm, tm), 0)
    ci = lax.broadcasted_iota(jnp.int32, (tm, tm), 1)
    strict = jnp.where(ri > ci, 1.0, 0.0).astype(BF16)
    prefix = _dot(strict, e_cnt.astype(BF16)) + run_scr[0:1, :]
    r1 = jnp.sum(jnp.where(oh1, prefix, 0.0), axis=-1, keepdims=True)
    r2 = jnp.sum(jnp.where(oh2, prefix, 0.0), axis=-1, keepdims=True)
    run_scr[0:1, :] = run_scr[0:1, :] + jnp.sum(e_cnt, axis=0, keepdims=True)
    cnt_ref[...] = run_scr[...]

    route = jnp.where(lane == 0, i1, 0.0)
    route = jnp.where(lane == 1, i2, route)
    route = jnp.where(lane == 2, w1, route)
    route = jnp.where(lane == 3, w2, route)
    route = jnp.where(lane == 4, r1, route)
    route = jnp.where(lane == 5, r2, route)
    route_ref[...] = route


def _post(p, ro_p, ro_s, ho_p, ho_s, x, wr, wh, wo, fgain, rw_hi, rw_lo, precise=None):
    t = x.shape[0]
    tm = TOKEN_TILE
    assert ro_p.shape[0] % tm == 0 and ro_s.shape[0] % tm == 0
    npt = ro_p.shape[0] // tm
    prompt_idx = lambda i: (jnp.minimum(i, npt - 1), 0)
    sample_idx = lambda i: (jnp.maximum(i - npt, 0), 0)
    gate_block = C_GATES // (2 * D_MODEL)
    const2 = lambda i: (0, 0)
    in_specs = [
        pl.BlockSpec((tm, 2 * D_MODEL), lambda i: (i, gate_block)),
        pl.BlockSpec((tm, RET_HEADS * RET_DV), prompt_idx),
        pl.BlockSpec((tm, RET_HEADS * RET_DV), sample_idx),
        pl.BlockSpec((tm, HG_HEADS * HG_DV), prompt_idx),
        pl.BlockSpec((tm, HG_HEADS * HG_DV), sample_idx),
        pl.BlockSpec((tm, D_MODEL), lambda i: (i, 0)),
        pl.BlockSpec((RET_HEADS * RET_DV, D_MODEL), const2),
        pl.BlockSpec((HG_HEADS * HG_DV, D_MODEL), const2),
        pl.BlockSpec((D_MODEL, D_MODEL), const2),
    ]
    args = [p, ro_p, ro_s, ho_p, ho_s, x, wr, wh, wo]
    if precise is not None:
        in_specs += [
            pl.BlockSpec((tm, 2 * D_MODEL), lambda i: (jnp.maximum(i - npt, 0), gate_block)),
            pl.BlockSpec((RET_HEADS * RET_DV, D_MODEL), const2),
            pl.BlockSpec((HG_HEADS * HG_DV, D_MODEL), const2),
            pl.BlockSpec((D_MODEL, D_MODEL), const2),
        ]
        args += list(precise)
    in_specs += [
        pl.BlockSpec((1, D_MODEL), const2),
        pl.BlockSpec((D_MODEL, LANES), const2),
        pl.BlockSpec((D_MODEL, LANES), const2),
    ]
    args += [fgain, rw_hi, rw_lo]
    return pl.pallas_call(
        functools.partial(_post_kernel, npt, precise is not None),
        grid=(t // tm,),
        in_specs=in_specs,
        out_specs=[
            pl.BlockSpec((tm, D_MODEL), lambda i: (i, 0)),
            pl.BlockSpec((tm * TILE_ROWS, LANES), lambda i: (i, 0)),
            pl.BlockSpec((tm, LANES), lambda i: (i, 0)),
            pl.BlockSpec((8, LANES), const2),
        ],
        out_shape=[
            jax.ShapeDtypeStruct((t, D_MODEL), F32),
            jax.ShapeDtypeStruct((t * TILE_ROWS, LANES), F32),
            jax.ShapeDtypeStruct((t, LANES), F32),
            jax.ShapeDtypeStruct((8, LANES), F32),
        ],
        scratch_shapes=[pltpu.VMEM((8, LANES), F32)],
        compiler_params=pltpu.CompilerParams(
            dimension_semantics=("arbitrary",), vmem_limit_bytes=VMEM_LIMIT),
        name="post",
    )(*args)


def _plan_kernel(pairs_per_step, n_blocks,
                 dest_ref, cnt_ref, slot_ref, bexp_ref, nused_ref):
    step = pl.program_id(0)
    base = step * pairs_per_step

    @pl.when(step == 0)
    def _():
        def per_expert(e, carry):
            start, last_e = carry
            nb = (cnt_ref[e] + (ROW_BLOCK - 1)) // ROW_BLOCK
            first = start // ROW_BLOCK

            def fill(bb, _):
                bexp_ref[first + bb] = e
                return 0

            lax.fori_loop(0, nb, fill, 0)
            return start + nb * ROW_BLOCK, jnp.where(nb > 0, e, last_e)

        total, last_e = lax.fori_loop(0, N_EXPERTS, per_expert, (jnp.int32(0), jnp.int32(0)))
        n_used = total // ROW_BLOCK
        nused_ref[0] = n_used

        def fill_tail(bb, _):
            bexp_ref[bb] = last_e
            return 0

        lax.fori_loop(n_used, n_blocks, fill_tail, 0)

        def init(r, _):
            slot_ref[r] = 0
            return 0

        lax.fori_loop(0, n_blocks * ROW_BLOCK, init, 0, unroll=16)

    def place(i, _):
        slot_ref[dest_ref[base + i]] = (base + i) >> 1
        return 0

    lax.fori_loop(0, pairs_per_step, place, 0, unroll=16)


def _plan(dest, counts, n_blocks):
    n_pairs = dest.shape[0]
    pairs_per_step = 2 * TOKEN_TILE
    assert n_pairs % pairs_per_step == 0
    smem = pl.BlockSpec(memory_space=pltpu.SMEM)
    return pl.pallas_call(
        functools.partial(_plan_kernel, pairs_per_step, n_blocks),
        grid=(n_pairs // pairs_per_step,),
        in_specs=[smem, smem],
        out_specs=[smem, smem, smem],
        out_shape=[
            jax.ShapeDtypeStruct((n_blocks * ROW_BLOCK,), jnp.int32),
            jax.ShapeDtypeStruct((n_blocks,), jnp.int32),
            jax.ShapeDtypeStruct((1,), jnp.int32),
        ],
        compiler_params=pltpu.CompilerParams(dimension_semantics=("arbitrary",)),
        name="plan",
    )(dest, counts)


def _expert_kernel(bexp_ref, nused_ref, slot_ref, h2_hbm, wg_ref, wu_ref, wd_ref, yb_ref,
                   xbuf, sems, wg_s, wu_s, wd_s):
    b = pl.program_id(0)
    n_used = nused_ref[0]

    def row_copy(tok, buf_slot, r):
        return pltpu.make_async_copy(
            h2_hbm.at[pl.ds(pl.multiple_of(tok * TILE_ROWS, TILE_ROWS), TILE_ROWS)],
            xbuf.at[buf_slot, pl.ds(pl.multiple_of(r * TILE_ROWS, TILE_ROWS), TILE_ROWS)],
            sems.at[buf_slot])

    def wait_block(block):
        buf_slot = block % 2
        pltpu.make_async_copy(h2_hbm.at[pl.ds(0, ROW_BLOCK * TILE_ROWS)], xbuf.at[buf_slot],
                              sems.at[buf_slot]).wait()

    def gather(block):
        def issue(r, _):
            row_copy(slot_ref[block * ROW_BLOCK + r], block % 2, r).start()
            return 0

        lax.fori_loop(0, ROW_BLOCK, issue, 0, unroll=8)

    @pl.when(b == 0)
    def _():
        gather(b)

    prev = bexp_ref[jnp.maximum(b - 1, 0)]

    @pl.when((b == 0) | (bexp_ref[b] != prev))
    def _():
        wg_s[...] = wg_ref[...].astype(BF16)
        wu_s[...] = wu_ref[...].astype(BF16)
        wd_s[...] = wd_ref[...].astype(BF16)

    @pl.when(b < n_used)
    def _():
        buf_slot = b % 2
        wait_block(b)
        gather(b + 1)
        x = jnp.concatenate(
            [xbuf[buf_slot, pl.ds(kk, ROW_BLOCK, stride=TILE_ROWS), :] for kk in range(TILE_ROWS)],
            axis=1).astype(BF16)
        g = _dot(x, wg_s[...])
        u = _dot(x, wu_s[...])
        a = (g * _sigmoid(g) * u).astype(BF16)
        y = _dot(a, wd_s[...])
        for kk in range(TILE_ROWS):
            yb_ref[pl.ds(kk, ROW_BLOCK, stride=TILE_ROWS), :] = y[:, kk * LANES:(kk + 1) * LANES]

    @pl.when(b == n_used)
    def _():
        wait_block(b)

    @pl.when(b >= n_used)
    def _():
        yb_ref[...] = jnp.zeros_like(yb_ref)


def _experts(layer, bexp, nused, slot_tok, h2, w_gate, w_up, w_down):
    n_blocks = bexp.shape[0]
    assert slot_tok.shape[0] == n_blocks * ROW_BLOCK
    wmap = lambda b, be, nu, st: (layer, be[b], 0, 0)
    return pl.pallas_call(
        _expert_kernel,
        grid_spec=pltpu.PrefetchScalarGridSpec(
            num_scalar_prefetch=3,
            grid=(n_blocks,),
            in_specs=[
                pl.BlockSpec(memory_space=pl.ANY),
                pl.BlockSpec((None, None, D_MODEL, D_EXPERT), wmap),
                pl.BlockSpec((None, None, D_MODEL, D_EXPERT), wmap),
                pl.BlockSpec((None, None, D_EXPERT, D_MODEL), wmap),
            ],
            out_specs=pl.BlockSpec((ROW_BLOCK * TILE_ROWS, LANES), lambda b, be, nu, st: (b, 0)),
            scratch_shapes=[
                pltpu.VMEM((2, ROW_BLOCK * TILE_ROWS, LANES), F32),
                pltpu.SemaphoreType.DMA((2,)),
                pltpu.VMEM((D_MODEL, D_EXPERT), BF16),
                pltpu.VMEM((D_MODEL, D_EXPERT), BF16),
                pltpu.VMEM((D_EXPERT, D_MODEL), BF16),
            ],
        ),
        out_shape=jax.ShapeDtypeStruct((n_blocks * ROW_BLOCK * TILE_ROWS, LANES), F32),
        compiler_params=pltpu.CompilerParams(
            dimension_semantics=("arbitrary",), vmem_limit_bytes=VMEM_LIMIT),
        name="experts",
    )(bexp, nused, slot_tok, h2, w_gate, w_up, w_down)


def _combine_kernel(final_tiles, dest_ref, route_ref, x1_ref, yb_hbm, fg_ref, *rest):
    if final_tiles is not None:
        yp_ref, ys_ref, buf, sems = rest
    else:
        x2_ref, buf, sems = rest
    i = pl.program_id(0)
    tm = x1_ref.shape[0]

    def row_copy(d, buf_slot, r):
        return pltpu.make_async_copy(
            yb_hbm.at[pl.ds(pl.multiple_of(d * TILE_ROWS, TILE_ROWS), TILE_ROWS)],
            buf.at[buf_slot, pl.ds(pl.multiple_of(r * TILE_ROWS, TILE_ROWS), TILE_ROWS)],
            sems.at[buf_slot])

    def gather(tile):
        buf_slot = tile % 2
        base = tile * tm

        def issue(tt, _):
            row_copy(dest_ref[2 * (base + tt)], buf_slot, tt).start()
            row_copy(dest_ref[2 * (base + tt) + 1], buf_slot, tm + tt).start()
            return 0

        lax.fori_loop(0, tm, issue, 0, unroll=4)

    @pl.when(i == 0)
    def _():
        gather(i)

    @pl.when(i + 1 < pl.num_programs(0))
    def _():
        gather(i + 1)

    buf_slot = i % 2
    pltpu.make_async_copy(yb_hbm.at[pl.ds(0, 2 * tm * TILE_ROWS)], buf.at[buf_slot], sems.at[buf_slot]).wait()

    w1 = route_ref[:, 2:3]
    w2 = route_ref[:, 3:4]
    parts = []
    for kk in range(TILE_ROWS):
        y = (w1 * buf[buf_slot, pl.ds(kk, tm, stride=TILE_ROWS), :]
             + w2 * buf[buf_slot, pl.ds(tm * TILE_ROWS + kk, tm, stride=TILE_ROWS), :])
        parts.append(x1_ref[:, kk * LANES:(kk + 1) * LANES] + y)
    x2 = jnp.concatenate(parts, axis=1)
    if final_tiles is None:
        x2_ref[...] = x2
    else:
        first_real, n_prompt_tiles = final_tiles
        yn = x2 * lax.rsqrt(jnp.mean(x2 * x2, axis=-1, keepdims=True) + EPS) * fg_ref[...]

        @pl.when((i >= first_real) & (i < n_prompt_tiles))
        def _():
            yp_ref[...] = yn

        @pl.when(i >= n_prompt_tiles)
        def _():
            ys_ref[...] = yn


def _combine(dest, route, x1, yb, fgain, final_rows=None):
    t = x1.shape[0]
    tm = COMBINE_TILE
    row_spec = pl.BlockSpec((tm, D_MODEL), lambda i, dst: (i, 0))
    if final_rows is None:
        final_tiles = None
        out_specs = [row_spec]
        out_shape = [jax.ShapeDtypeStruct((t, D_MODEL), F32)]
    else:
        first_row, n_prompt = final_rows
        assert first_row % tm == 0 and n_prompt % tm == 0
        first_real, npt = first_row // tm, n_prompt // tm
        final_tiles = (first_real, npt)
        out_specs = [
            pl.BlockSpec((tm, D_MODEL), lambda i, dst: (jnp.clip(i - first_real, 0, npt - first_real - 1), 0)),
            pl.BlockSpec((tm, D_MODEL), lambda i, dst: (jnp.maximum(i - npt, 0), 0)),
        ]
        out_shape = [jax.ShapeDtypeStruct((n_prompt - first_row, D_MODEL), F32),
                     jax.ShapeDtypeStruct((t - n_prompt, D_MODEL), F32)]
    return pl.pallas_call(
        functools.partial(_combine_kernel, final_tiles),
        grid_spec=pltpu.PrefetchScalarGridSpec(
            num_scalar_prefetch=1,
            grid=(t // tm,),
            in_specs=[
                pl.BlockSpec((tm, LANES), lambda i, dst: (i, 0)),
                row_spec,
                pl.BlockSpec(memory_space=pl.ANY),
                pl.BlockSpec((1, D_MODEL), lambda i, dst: (0, 0)),
            ],
            out_specs=out_specs,
            scratch_shapes=[
                pltpu.VMEM((2, 2 * tm * TILE_ROWS, LANES), F32),
                pltpu.SemaphoreType.DMA((2,)),
            ],
        ),
        out_shape=out_shape,
        compiler_params=pltpu.CompilerParams(
            dimension_semantics=("arbitrary",), vmem_limit_bytes=VMEM_LIMIT),
        name="combine",
    )(dest, route, x1, yb, fgain)


def _rotary_tables(pos):
    half = RET_DK // 2
    inv_freq = 1.0 / (ROPE_BASE ** jnp.linspace(0.0, 1.0, half, dtype=F32))
    ang = pos.astype(F32)[:, None] * inv_freq[None, :]
    return jnp.cos(ang), jnp.sin(ang)


def _decay_tables(chunk):
    log_gamma = jnp.log1p(-jnp.exp2(-5.0 - jnp.arange(RET_HEADS, dtype=F32)))
    idx = jnp.arange(chunk, dtype=F32)
    rel = idx[:, None] - idx[None, :]
    causal = rel >= 0
    dint = jnp.where(causal, jnp.exp(log_gamma[:, None, None] * jnp.where(causal, rel, 0.0)), 0.0)
    qdec = jnp.exp(log_gamma[None, :] * (idx[:, None] + 1.0))
    kdec = jnp.exp(log_gamma[None, :] * (chunk - 1.0 - idx[:, None]))
    cdec = jnp.exp(log_gamma * chunk)
    qdec = jnp.repeat(qdec, RET_DK, axis=1)
    kdec = jnp.repeat(kdec, RET_DK, axis=1)
    return dint, qdec, kdec, cdec


def _split_param_kernel(w_ref, hi_ref, lo_ref):
    hi, lo = _split(w_ref[...])
    hi_ref[...] = hi
    lo_ref[...] = lo


def _split_param(w):
    rows, cols = w.shape
    tm = TOKEN_TILE
    assert rows % tm == 0 and cols % LANES == 0
    spec = pl.BlockSpec((tm, cols), lambda i: (i, 0))
    return pl.pallas_call(
        _split_param_kernel,
        grid=(rows // tm,),
        in_specs=[spec],
        out_specs=[spec, spec],
        out_shape=[jax.ShapeDtypeStruct(w.shape, BF16)] * 2,
        name="split_param",
    )(w)


def _router_split(router_group_l, router_expert_l):
    rw = jnp.concatenate(
        [router_expert_l, router_group_l,
         jnp.zeros((D_MODEL, LANES - N_EXPERTS - N_GROUPS), F32)], axis=1)
    return _split_param(rw)


def kernel(x_prompt, x_sample, state_ret, state_hgrn, meta_tokens, mix_norm, w_in, hg_lb_logits, hg_norm,
           w_ret_branch, w_hg_branch, w_out, ffn_norm, router_group, router_expert, w_gate, w_up, w_down,
           final_norm):
    depth = w_in.shape[0]
    bp, seq, d = x_prompt.shape
    bs, dec, _ = x_sample.shape
    assert bp == 1 and dec == CHUNK and d == D_MODEL and seq % CHUNK == 0

    n_real = N_META + seq
    n_sample = bs * dec
    n_pad = (-(n_real + n_sample)) % PROJ_TM
    while (n_pad + n_real) % PROMPT_CHUNK:
        n_pad += PROJ_TM
    n_prompt = n_pad + n_real
    assert n_pad > 0
    t = n_prompt + n_sample
    n_blocks = -(-(2 * t) // ROW_BLOCK) + N_EXPERTS + 1

    x = jnp.concatenate([
        jnp.zeros((n_pad, d), F32),
        meta_tokens.astype(F32),
        x_prompt[0],
        x_sample.reshape(n_sample, d),
    ], axis=0)

    rot_prompt = _rotary_tables(jnp.arange(n_prompt) - (n_pad + N_META))
    rot_sample = _rotary_tables(PAST_LEN + jnp.arange(dec))
    dec_prompt = _decay_tables(PROMPT_CHUNK)
    dec_sample = _decay_tables(dec)

    prob = jax.nn.softmax(hg_lb_logits.astype(F32), axis=0)
    cum = jnp.cumsum(prob, axis=0)
    lb_all = cum - cum[0:1]

    assert state_ret.dtype == F32 and state_hgrn.dtype == F32 and w_in.dtype == F32
    states = [jnp.zeros((depth, 1) + state_ret.shape[2:], F32), jnp.zeros((depth, 1) + state_hgrn.shape[2:], F32),
              jnp.zeros(state_ret.shape, F32), jnp.zeros(state_hgrn.shape, F32)]
    for l in range(depth):
        lb = lb_all[l]
        lbt = jnp.concatenate(
            [jnp.log(lb)[None], jnp.log1p(-lb)[None], (1.0 - lb)[None], jnp.zeros((5, d), F32)], axis=0)
        p = _proj(l, x, mix_norm[l][None], w_in)
        hgain = hg_norm[l][None].astype(F32)
        ro_p, ho_p, ret_p, hg_p = _mixer(l, p, PROMPT_CHUNK, 0, n_prompt, rot_prompt, dec_prompt, lbt, hgain,
                                         states[:2])
        precise = l < depth - 1
        branch_w = [_split_param(w[l]) for w in (w_ret_branch, w_hg_branch, w_out)]
        if precise:
            p_s = _proj_precise(l, x, n_prompt, n_sample, mix_norm[l][None], w_in)
            ro_s, ho_s, ret_s, hg_s = _mixer(l, p_s, dec, 0, n_sample, rot_sample, dec_sample, lbt, hgain,
                                             states[2:], states_in=(state_ret, state_hgrn), precise=True)
            post_precise = (p_s,) + tuple(lo for _, lo in branch_w)
        else:
            ro_s, ho_s, ret_s, hg_s = _mixer(l, p, dec, n_prompt, n_sample, rot_sample, dec_sample, lbt, hgain,
                                             states[2:], states_in=(state_ret, state_hgrn))
            post_precise = None
        states = [ret_p, hg_p, ret_s, hg_s]
        rw_hi, rw_lo = _router_split(router_group[l], router_expert[l])
        x1, h2, route, counts = _post(
            p, ro_p, ro_s, ho_p, ho_s, x, branch_w[0][0], branch_w[1][0], branch_w[2][0],
            ffn_norm[l][None], rw_hi, rw_lo, precise=post_precise)
        route_t = route[:, :8].T
        ids = route_t[0:2].astype(jnp.int32)
        ranks = route_t[4:6].astype(jnp.int32)
        cnt = counts[0, :N_EXPERTS].astype(jnp.int32)
        padded = (cnt + (ROW_BLOCK - 1)) // ROW_BLOCK * ROW_BLOCK
        pstart = jnp.cumsum(padded) - padded
        dest = (pstart[ids] + ranks).T.reshape(-1)
        slot_tok, bexp, nused = _plan(dest, cnt, n_blocks)
        yb = _experts(l, bexp, nused, slot_tok, h2, w_gate, w_up, w_down)
        if l < depth - 1:
            x, = _combine(dest, route, x1, yb, final_norm[None])
        else:
            y_prompt, y_sample = _combine(dest, route, x1, yb, final_norm[None],
                                          final_rows=(n_prompt - seq, n_prompt))

    ret_prompt, hgrn_prompt, ret_sample, hgrn_sample = states
    return (y_prompt.reshape(bp, seq, d), y_sample.reshape(bs, dec, d),
            ret_prompt, hgrn_prompt, ret_sample, hgrn_sample)
```

```python
import functools

import jax
import jax.numpy as jnp
from jax import lax
from jax.experimental import pallas as pl
from jax.experimental.pallas import tpu as pltpu

F32 = jnp.float32
BF16 = jnp.bfloat16

D_MODEL = 1024
CHUNK = 64
N_META = 16
RET_HEADS = 4
RET_DK = 256
RET_DV = 512
HG_HEADS = 8
HG_DK = 128
HG_DV = 128
N_GROUPS = 4
EXPERTS_PER_GROUP = 8
N_EXPERTS = 32
D_EXPERT = 512
ROPE_BASE = 10000.0
EPS = 1e-6
PAST_LEN = 4096
LOG2_E = 1.4426950408889634
MAX_SUBBLOCK_DROP_LOG2 = 100.0

C_RQ, C_RK, C_RV, C_RG = 0, 1024, 2048, 4096
C_HQ, C_HF, C_HI, C_HG = 6144, 7168, 8192, 9216
C_GATES = 10240
PROJ_WIDTH = 12288
MIX_WIDTH = C_GATES

TOKEN_TILE = 256
PROJ_TN = 2048
PROJ_TM = 768
PROJ_PRECISE_TN = 1024
ROW_BLOCK = 256
COMBINE_TILE = 128
SUB = 16
HALF = 64
PROMPT_CHUNK = 128
PRECISE_TAIL = 256
LANES = 128
TILE_ROWS = 8
VMEM_LIMIT = 56 * 1024 * 1024


def _sigmoid(x):
    return 1.0 / (1.0 + jnp.exp(-x))


def _dot(a, b):
    return jnp.dot(a, b, preferred_element_type=F32)


def _dot_nt(a, b):
    return lax.dot_general(a, b, (((1,), (1,)), ((), ())), preferred_element_type=F32)


def _dot_tn(a, b):
    return lax.dot_general(a, b, (((0,), (0,)), ((), ())), preferred_element_type=F32)


def _split(a):
    hi = a.astype(BF16)
    return hi, (a - hi.astype(F32)).astype(BF16)


def _mm(dot, a, b, precise):
    if not precise:
        return dot(a.astype(BF16), b.astype(BF16))
    ah, al = _split(a)
    bh, bl = _split(b)
    return dot(ah, bh) + (dot(ah, bl) + dot(al, bh))


def _mm_w(a, w_hi, w_lo, precise):
    if not precise:
        return _dot(a.astype(BF16), w_hi)
    ah, al = _split(a)
    return _dot(ah, w_hi) + (_dot(ah, w_lo) + _dot(al, w_hi))


def _deinterleave_heads(w_bf):
    half = RET_DK // 2
    r = lax.broadcasted_iota(jnp.int32, (RET_DK, RET_DK), 0)
    c = lax.broadcasted_iota(jnp.int32, (RET_DK, RET_DK), 1)
    src = jnp.where(c < half, 2 * c, 2 * (c - half) + 1)
    pick = jnp.where(r == src, 1.0, 0.0).astype(BF16)
    blocks = [_dot(w_bf[:, h * RET_DK:(h + 1) * RET_DK], pick).astype(BF16)
              for h in range(w_bf.shape[1] // RET_DK)]
    return jnp.concatenate(blocks, axis=1)


def _proj_kernel(qk_tiles, x_ref, g_ref, w_ref, o_ref, wbf_ref):
    j = pl.program_id(0)
    i = pl.program_id(1)

    @pl.when((i == 0) & (j < qk_tiles))
    def _():
        wbf_ref[...] = _deinterleave_heads(w_ref[...].astype(BF16))

    @pl.when((i == 0) & (j >= qk_tiles))
    def _():
        wbf_ref[...] = w_ref[...].astype(BF16)

    x = x_ref[...]
    xn = x * lax.rsqrt(jnp.mean(x * x, axis=-1, keepdims=True) + EPS) * g_ref[...]
    o_ref[...] = _dot(xn.astype(BF16), wbf_ref[...])


def _proj(layer, x, gain, w_in):
    t = x.shape[0]
    tm, tn = PROJ_TM, PROJ_TN
    assert t % tm == 0 and C_RV % tn == 0
    return pl.pallas_call(
        functools.partial(_proj_kernel, C_RV // tn),
        grid=(PROJ_WIDTH // tn, t // tm),
        in_specs=[
            pl.BlockSpec((tm, D_MODEL), lambda j, i: (i, 0)),
            pl.BlockSpec((1, D_MODEL), lambda j, i: (0, 0)),
            pl.BlockSpec((None, D_MODEL, tn), lambda j, i: (layer, 0, j)),
        ],
        out_specs=pl.BlockSpec((tm, tn), lambda j, i: (i, j)),
        out_shape=jax.ShapeDtypeStruct((t, PROJ_WIDTH), F32),
        scratch_shapes=[pltpu.VMEM((D_MODEL, tn), BF16)],
        compiler_params=pltpu.CompilerParams(
            dimension_semantics=("arbitrary", "arbitrary"), vmem_limit_bytes=VMEM_LIMIT),
        name="proj",
    )(x, gain, w_in)


def _proj_precise_kernel(qk_tiles, x_ref, g_ref, w_ref, o_ref, whi_ref, wlo_ref):
    j = pl.program_id(0)
    i = pl.program_id(1)

    @pl.when((i == 0) & (j < qk_tiles))
    def _():
        hi, lo = _split(w_ref[...])
        whi_ref[...] = _deinterleave_heads(hi)
        wlo_ref[...] = _deinterleave_heads(lo)

    @pl.when((i == 0) & (j >= qk_tiles))
    def _():
        hi, lo = _split(w_ref[...])
        whi_ref[...] = hi
        wlo_ref[...] = lo

    x = x_ref[...]
    xn = x * lax.rsqrt(jnp.mean(x * x, axis=-1, keepdims=True) + EPS) * g_ref[...]
    o_ref[...] = _mm_w(xn, whi_ref[...], wlo_ref[...], True)


def _proj_precise(layer, x, first_row, n_rows, gain, w_in):
    tm, tn = TOKEN_TILE, PROJ_PRECISE_TN
    assert first_row % tm == 0 and n_rows % tm == 0 and C_RV % tn == 0
    first_tile = first_row // tm
    return pl.pallas_call(
        functools.partial(_proj_precise_kernel, C_RV // tn),
        grid=(PROJ_WIDTH // tn, n_rows // tm),
        in_specs=[
            pl.BlockSpec((tm, D_MODEL), lambda j, i: (first_tile + i, 0)),
            pl.BlockSpec((1, D_MODEL), lambda j, i: (0, 0)),
            pl.BlockSpec((None, D_MODEL, tn), lambda j, i: (layer, 0, j)),
        ],
        out_specs=pl.BlockSpec((tm, tn), lambda j, i: (i, j)),
        out_shape=jax.ShapeDtypeStruct((n_rows, PROJ_WIDTH), F32),
        scratch_shapes=[pltpu.VMEM((D_MODEL, tn), BF16), pltpu.VMEM((D_MODEL, tn), BF16)],
        compiler_params=pltpu.CompilerParams(
            dimension_semantics=("arbitrary", "arbitrary"), vmem_limit_bytes=VMEM_LIMIT),
        name="proj_precise",
    )(x, gain, w_in)


def _mixer_kernel(chunk, mode, precise,
                  p_ref, cos_ref, sin_ref, dint_ref, qdec_ref, kdec_ref, cdec_ref,
                  lbt_ref, hgain_ref, *rest):
    if mode != "prompt":
        sr_in_ref, sh_in_ref = rest[:2]
        rest = rest[2:]
    (ro_ref, ho_ref, sr_out_ref, sh_out_ref,
     sr_scr, sh_scr, hq_scr, hb_scr, hk_scr, ho_scr) = rest[2:]
    c = pl.program_id(0)
    half = RET_DK // 2

    def load_states():
        sr_scr[...] = sr_in_ref[...]
        for h in range(HG_HEADS):
            sh_scr[h] = sh_in_ref[h].T

    if mode == "sample":
        load_states()
    elif mode == "continue":
        pl.when(c == 0)(load_states)
    else:
        @pl.when(c == 0)
        def _():
            sr_scr[...] = jnp.zeros_like(sr_scr)
            sh_scr[...] = jnp.zeros_like(sh_scr)

    cos = cos_ref[...]
    sin = sin_ref[...]
    scores_v, to_state = [], []
    for h in range(RET_HEADS):
        q = p_ref[:, C_RQ + h * RET_DK:C_RQ + (h + 1) * RET_DK]
        q1, q2 = q[:, :half], q[:, half:]
        qr = jnp.concatenate([q1 * cos - q2 * sin, q1 * sin + q2 * cos], axis=1)
        to_state.append(qr * qdec_ref[:, h * RET_DK:(h + 1) * RET_DK])
        scores_v.append(qr)
    for h in range(RET_HEADS):
        k = p_ref[:, C_RK + h * RET_DK:C_RK + (h + 1) * RET_DK]
        k1, k2 = k[:, :half], k[:, half:]
        kr = jnp.concatenate([k1 * cos - k2 * sin, k1 * sin + k2 * cos], axis=1) * (RET_DK ** -0.5)
        to_state.append(kr * kdec_ref[:, h * RET_DK:(h + 1) * RET_DK])
        scores_v[h] = _mm(_dot_nt, scores_v[h], kr, precise) * dint_ref[h]
    pr = lax.broadcasted_iota(jnp.int32, (RET_DK, RET_DK), 0)
    pc = lax.broadcasted_iota(jnp.int32, (RET_DK, RET_DK), 1)
    src_lane = jnp.where(pr < half, 2 * pr, 2 * (pr - half) + 1)
    perm = jnp.where(pc == src_lane, 1.0, 0.0)
    natural = _mm(_dot, jnp.concatenate(to_state, axis=0), perm, precise)
    for h in range(RET_HEADS):
        v = p_ref[:, C_RV + h * RET_DV:C_RV + (h + 1) * RET_DV]
        s = sr_scr[h]
        qd = natural[h * chunk:(h + 1) * chunk]
        kd = natural[(RET_HEADS + h) * chunk:(RET_HEADS + h + 1) * chunk]
        o = _mm(_dot, scores_v[h], v, precise) + _mm(_dot, qd, s, precise)
        sr_scr[h] = cdec_ref[h] * s + _mm(_dot_tn, kd, v, precise)
        on = o * lax.rsqrt(jnp.mean(o * o, axis=-1, keepdims=True) + EPS)
        g = p_ref[:, C_RG + h * RET_DV:C_RG + (h + 1) * RET_DV]
        ro_ref[:, h * RET_DV:(h + 1) * RET_DV] = (on * (g * _sigmoid(g))).astype(ro_ref.dtype)

    hf = p_ref[:, C_HF:C_HF + D_MODEL]
    log_lb = lbt_ref[0:1, :]
    log1m_lb = lbt_ref[1:2, :]
    one_m_lb = lbt_ref[2:3, :]
    log_sig = jnp.minimum(hf, 0.0) - jnp.log(1.0 + jnp.exp(-jnp.abs(hf)))
    b_ = log1m_lb + log_sig
    log_f = jnp.maximum(log_lb, b_) + jnp.log(1.0 + jnp.exp(-jnp.abs(log_lb - b_)))
    k_in = one_m_lb * _sigmoid(-hf)
    hq = p_ref[:, C_HQ:C_HQ + D_MODEL]
    q_all = hq * _sigmoid(hq)

    tr = lax.broadcasted_iota(jnp.int32, (chunk, chunk), 0)
    tc = lax.broadcasted_iota(jnp.int32, (chunk, chunk), 1)
    tri = jnp.where(tr >= tc, 1.0, 0.0).astype(BF16)
    f_hi = log_f.astype(BF16)
    r1 = log_f - f_hi.astype(F32)
    f_mid = r1.astype(BF16)
    f_lo = (r1 - f_mid.astype(F32)).astype(BF16)
    hb_scr[...] = (_dot(tri, f_hi) + _dot(tri, f_mid) + _dot(tri, f_lo)) * LOG2_E
    hq_scr[...] = q_all
    hk_scr[...] = k_in

    n_sub = HALF // SUB
    ri = lax.broadcasted_iota(jnp.int32, (HALF, HALF), 0)
    ci = lax.broadcasted_iota(jnp.int32, (HALF, HALF), 1)
    row = lax.broadcasted_iota(jnp.int32, (HALF, HG_DK), 0)
    col_sub = lax.broadcasted_iota(jnp.int32, (SUB, HALF), 1)
    hgain = hgain_ref[...]

    drops = [hb_scr[SUB * ib:SUB * ib + 1, :] - hb_scr[SUB * (ib + 1) - 1:SUB * (ib + 1), :]
             for ib in range(chunk // SUB)]
    max_drop = jnp.max(jnp.concatenate(drops, axis=0))
    factorise_ok = max_drop <= MAX_SUBBLOCK_DROP_LOG2

    def half_scores(q, k, b, factorised):
        b_end = [b[SUB * (jb + 1) - 1:SUB * (jb + 1), :] for jb in range(n_sub)]
        b_end_rows = jnp.concatenate(
            [jnp.broadcast_to(b_end[jb], (SUB, HG_DK)) for jb in range(n_sub)], axis=0)
        k_hat = k * jnp.exp2(b_end_rows - b)
        lhs, rhs = [], []
        for jb in range(n_sub - 1):
            q_hat = q * jnp.exp2(b - b_end[jb])
            lhs.append(jnp.where(row >= SUB * (jb + 1), q_hat, 0.0))
            rhs.append(jnp.where((row >= SUB * jb) & (row < SUB * (jb + 1)), k_hat, 0.0))
        a_off = _mm(_dot_nt, jnp.concatenate(lhs, axis=1), jnp.concatenate(rhs, axis=1), precise)
        if factorised:
            b_first = jnp.concatenate(
                [jnp.broadcast_to(b[SUB * ib:SUB * ib + 1, :], (SUB, HG_DK)) for ib in range(n_sub)], axis=0)
            q_t = q * jnp.exp2(b - b_first)
            k_t = k * jnp.exp2(b_first - b)
            same_block = (ri // SUB) == (ci // SUB)
            a_diag = jnp.where(same_block & (ri >= ci), _mm(_dot_nt, q_t, k_t, precise), 0.0)
        else:
            diag = []
            for ib in range(n_sub):
                r0 = SUB * ib
                b_i = b[r0:r0 + SUB, :]
                q_i = q[r0:r0 + SUB, :]
                blk = jnp.zeros((SUB, HALF), F32)
                for j in range(SUB):
                    b_j = b[r0 + j:r0 + j + 1, :]
                    k_j = k[r0 + j:r0 + j + 1, :]
                    sc = jnp.sum(jnp.exp2(b_i - b_j) * q_i * k_j, axis=-1, keepdims=True)
                    blk = jnp.where(col_sub == r0 + j, sc, blk)
                diag.append(blk)
            a_diag = jnp.where(ri >= ci, jnp.concatenate(diag, axis=0), 0.0)
        return a_off + a_diag

    def hgrn_head(h, factorised):
        sl = slice(h * HG_DK, (h + 1) * HG_DK)
        q = hq_scr[:, sl]
        b = hb_scr[:, sl]
        k = hk_scr[:, sl]
        vb = p_ref[:, C_HI + h * HG_DV:C_HI + (h + 1) * HG_DV]
        if factorised:
            st = sh_scr[h]
            o = _mm(_dot_nt, q * jnp.exp2(b), st, precise)
            ho_scr[:, sl] = o
            b_last = b[chunk - 1:chunk, :]
            kd = k * jnp.exp2(b_last - b)
            sh_scr[h] = st * jnp.exp2(b_last) + _mm(_dot_tn, vb, kd, precise)
        else:
            o = ho_scr[:, sl]
        intra = []
        for hh in range(chunk // HALF):
            lo, hi = hh * HALF, (hh + 1) * HALF
            a = half_scores(q[lo:hi], k[lo:hi], b[lo:hi], factorised)
            if hh == 0:
                intra.append(_mm(_dot, a, vb[lo:hi], precise))
            else:
                b_mid = b[lo - 1:lo, :]
                q_c = q[lo:hi] * jnp.exp2(b[lo:hi] - b_mid)
                k_c = k[lo - HALF:lo] * jnp.exp2(b_mid - b[lo - HALF:lo])
                both = jnp.concatenate([_mm(_dot_nt, q_c, k_c, precise), a], axis=1)
                intra.append(_mm(_dot, both, vb[lo - HALF:hi], precise))
        o = o + jnp.concatenate(intra, axis=0)
        on = o * lax.rsqrt(jnp.mean(o * o, axis=-1, keepdims=True) + EPS) * hgain
        g = p_ref[:, C_HG + h * HG_DV:C_HG + (h + 1) * HG_DV]
        ho_ref[:, h * HG_DV:(h + 1) * HG_DV] = (on * (g * _sigmoid(g))).astype(ho_ref.dtype)

    for h in range(HG_HEADS):
        hgrn_head(h, True)

    @pl.when(jnp.logical_not(factorise_ok))
    def _():
        for h in range(HG_HEADS):
            hgrn_head(h, False)

    def emit_states():
        sr_out_ref[...] = sr_scr[...]
        for h in range(HG_HEADS):
            sh_out_ref[h] = sh_scr[h].T

    if mode == "sample":
        emit_states()
    else:
        pl.when(c == pl.num_programs(0) - 1)(emit_states)


def _mixer(layer, p, chunk, first_row, n_rows, rotary, decays, lbt, hgain, state_out_prev, states_in=None,
           precise=False, mode=None):
    assert chunk in (HALF, 2 * HALF) and first_row % chunk == 0 and n_rows % chunk == 0
    if mode is None:
        mode = "prompt" if states_in is None else "sample"
    is_sample = mode == "sample"
    first_chunk, n_chunks = first_row // chunk, n_rows // chunk
    cos, sin = rotary
    dint, qdec, kdec, cdec = decays
    rot_idx = (lambda c: (0, 0)) if is_sample else (lambda c: (c, 0))
    seq_idx = (lambda c: (layer, c, 0, 0, 0)) if is_sample else (lambda c: (layer, 0, 0, 0, 0))
    const2 = lambda c: (0, 0)
    ret_blk = (None, None, RET_HEADS, RET_DK, RET_DV)
    hg_blk = (None, None, HG_HEADS, HG_DK, HG_DV)
    in_specs = [
        pl.BlockSpec((chunk, MIX_WIDTH), lambda c: (first_chunk + c, 0)),
        pl.BlockSpec((chunk, LANES), rot_idx),
        pl.BlockSpec((chunk, LANES), rot_idx),
        pl.BlockSpec((RET_HEADS, chunk, chunk), lambda c: (0, 0, 0)),
        pl.BlockSpec((chunk, RET_HEADS * RET_DK), const2),
        pl.BlockSpec((chunk, RET_HEADS * RET_DK), const2),
        pl.BlockSpec(memory_space=pltpu.SMEM),
        pl.BlockSpec((8, D_MODEL), const2),
        pl.BlockSpec((1, HG_DV), const2),
    ]
    args = [p, cos, sin, dint, qdec, kdec, cdec, lbt, hgain]
    if mode != "prompt":
        in_specs += [pl.BlockSpec(ret_blk, seq_idx), pl.BlockSpec(hg_blk, seq_idx)]
        args += list(states_in)
    aliases = {}
    for n, prev in enumerate(state_out_prev):
        aliases[len(args)] = 2 + n
        in_specs.append(pl.BlockSpec(memory_space=pl.ANY))
        args.append(prev)
    return pl.pallas_call(
        functools.partial(_mixer_kernel, chunk, mode, precise),
        grid=(n_chunks,),
        in_specs=in_specs,
        out_specs=[
            pl.BlockSpec((chunk, RET_HEADS * RET_DV), lambda c: (c, 0)),
            pl.BlockSpec((chunk, HG_HEADS * HG_DV), lambda c: (c, 0)),
            pl.BlockSpec(ret_blk, seq_idx),
            pl.BlockSpec(hg_blk, seq_idx),
        ],
        out_shape=[
            jax.ShapeDtypeStruct((n_rows, RET_HEADS * RET_DV), F32 if precise else BF16),
            jax.ShapeDtypeStruct((n_rows, HG_HEADS * HG_DV), F32 if precise else BF16),
            jax.ShapeDtypeStruct(state_out_prev[0].shape, F32),
            jax.ShapeDtypeStruct(state_out_prev[1].shape, F32),
        ],
        input_output_aliases=aliases,
        scratch_shapes=[
            pltpu.VMEM((RET_HEADS, RET_DK, RET_DV), F32),
            pltpu.VMEM((HG_HEADS, HG_DV, HG_DK), F32),
            pltpu.VMEM((chunk, HG_HEADS * HG_DK), F32),
            pltpu.VMEM((chunk, HG_HEADS * HG_DK), F32),
            pltpu.VMEM((chunk, HG_HEADS * HG_DK), F32),
            pltpu.VMEM((chunk, HG_HEADS * HG_DV), F32),
        ],
        compiler_params=pltpu.CompilerParams(
            dimension_semantics=("arbitrary",), vmem_limit_bytes=VMEM_LIMIT),
        name="mixer_" + mode,
    )(*args)


def _post_kernel(n_prompt_tiles, precise_samples, gp_ref, rop_ref, ros_ref, hop_ref, hos_ref, x_ref,
                 wr_ref, wh_ref, wo_ref, *rest):
    if precise_samples:
        gps_ref, wrl_ref, whl_ref, wol_ref = rest[:4]
        rest = rest[4:]
    fg_ref, rwh_ref, rwl_ref, x1_ref, h2_ref, route_ref, cnt_ref, run_scr = rest
    i = pl.program_id(0)
    tm = x_ref.shape[0]

    @pl.when(i == 0)
    def _():
        run_scr[...] = jnp.zeros_like(run_scr)

    def mix(gp, ro, ho, w_lo, precise):
        gate_r = _sigmoid(gp[:, 0:D_MODEL])
        gate_h = _sigmoid(gp[:, D_MODEL:2 * D_MODEL])
        merged = (gate_r * _mm_w(ro, wr_ref[...], w_lo[0], precise)
                  + gate_h * _mm_w(ho, wh_ref[...], w_lo[1], precise))
        x1_ref[...] = x_ref[...] + _mm_w(merged, wo_ref[...], w_lo[2], precise)

    @pl.when(i < n_prompt_tiles)
    def _():
        mix(gp_ref[...], rop_ref[...], hop_ref[...], (None, None, None), False)

    @pl.when(i >= n_prompt_tiles)
    def _():
        if precise_samples:
            mix(gps_ref[...], ros_ref[...], hos_ref[...], (wrl_ref[...], whl_ref[...], wol_ref[...]), True)
        else:
            mix(gp_ref[...], ros_ref[...], hos_ref[...], (None, None, None), False)

    x1 = x1_ref[...]
    h2 = x1 * lax.rsqrt(jnp.mean(x1 * x1, axis=-1, keepdims=True) + EPS) * fg_ref[...]
    for kk in range(TILE_ROWS):
        h2_ref[pl.ds(kk, tm, stride=TILE_ROWS), :] = h2[:, kk * LANES:(kk + 1) * LANES]

    h_hi = h2.astype(BF16)
    h_lo = (h2 - h_hi.astype(F32)).astype(BF16)
    logits = _dot(h_hi, rwh_ref[...]) + (_dot(h_hi, rwl_ref[...]) + _dot(h_lo, rwh_ref[...]))

    lane = lax.broadcasted_iota(jnp.int32, (tm, LANES), 1)
    lanef = lane.astype(F32)
    neg_inf = jnp.float32(-jnp.inf)
    big = jnp.float32(1e9)
    is_g = (lane >= N_EXPERTS) & (lane < N_EXPERTS + N_GROUPS)
    gl = jnp.where(is_g, logits, neg_inf)
    gmax = jnp.max(gl, axis=-1, keepdims=True)
    gidx = jnp.min(jnp.where(gl == gmax, lanef, big), axis=-1, keepdims=True) - N_EXPERTS
    g_w = 1.0 / jnp.sum(jnp.exp(gl - gmax), axis=-1, keepdims=True)
    lane_group = jnp.floor(lanef * (1.0 / EXPERTS_PER_GROUP))
    in_group = (lane < N_EXPERTS) & (lane_group == gidx)
    el = jnp.where(in_group, logits, neg_inf)
    m1 = jnp.max(el, axis=-1, keepdims=True)
    i1 = jnp.min(jnp.where(el == m1, lanef, big), axis=-1, keepdims=True)
    el2 = jnp.where(lanef == i1, neg_inf, el)
    m2 = jnp.max(el2, axis=-1, keepdims=True)
    i2 = jnp.min(jnp.where(el2 == m2, lanef, big), axis=-1, keepdims=True)
    tt = jnp.exp(m2 - m1)
    w1 = g_w / (1.0 + tt)
    w2 = g_w * tt / (1.0 + tt)

    oh1 = lanef == i1
    oh2 = lanef == i2
    e_cnt = jnp.where(oh1, 1.0, 0.0) + jnp.where(oh2, 1.0, 0.0)
    ri = lax.broadcasted_iota(jnp.int32, (tm, tm), 0)
    ci = lax.broadcasted_iota(jnp.int32, (tm, tm), 1)
    strict = jnp.where(ri > ci, 1.0, 0.0).astype(BF16)
    prefix = _dot(strict, e_cnt.astype(BF16)) + run_scr[0:1, :]
    r1 = jnp.sum(jnp.where(oh1, prefix, 0.0), axis=-1, keepdims=True)
    r2 = jnp.sum(jnp.where(oh2, prefix, 0.0), axis=-1, keepdims=True)
    run_scr[0:1, :] = run_scr[0:1, :] + jnp.sum(e_cnt, axis=0, keepdims=True)
    cnt_ref[...] = run_scr[...]

    route = jnp.where(lane == 0, i1, 0.0)
    route = jnp.where(lane == 1, i2, route)
    route = jnp.where(lane == 2, w1, route)
    route = jnp.where(lane == 3, w2, route)
    route = jnp.where(lane == 4, r1, route)
    route = jnp.where(lane == 5, r2, route)
    route_ref[...] = route


def _post(p, ro_p, ro_s, ho_p, ho_s, x, wr, wh, wo, fgain, rw_hi, rw_lo, precise=None):
    t = x.shape[0]
    tm = TOKEN_TILE
    assert ro_p.shape[0] % tm == 0 and ro_s.shape[0] % tm == 0
    npt = ro_p.shape[0] // tm
    prompt_idx = lambda i: (jnp.minimum(i, npt - 1), 0)
    sample_idx = lambda i: (jnp.maximum(i - npt, 0), 0)
    gate_block = C_GATES // (2 * D_MODEL)
    const2 = lambda i: (0, 0)
    in_specs = [
        pl.BlockSpec((tm, 2 * D_MODEL), lambda i: (i, gate_block)),
        pl.BlockSpec((tm, RET_HEADS * RET_DV), prompt_idx),
        pl.BlockSpec((tm, RET_HEADS * RET_DV), sample_idx),
        pl.BlockSpec((tm, HG_HEADS * HG_DV), prompt_idx),
        pl.BlockSpec((tm, HG_HEADS * HG_DV), sample_idx),
        pl.BlockSpec((tm, D_MODEL), lambda i: (i, 0)),
        pl.BlockSpec((RET_HEADS * RET_DV, D_MODEL), const2),
        pl.BlockSpec((HG_HEADS * HG_DV, D_MODEL), const2),
        pl.BlockSpec((D_MODEL, D_MODEL), const2),
    ]
    args = [p, ro_p, ro_s, ho_p, ho_s, x, wr, wh, wo]
    if precise is not None:
        in_specs += [
            pl.BlockSpec((tm, 2 * D_MODEL), lambda i: (jnp.maximum(i - npt, 0), gate_block)),
            pl.BlockSpec((RET_HEADS * RET_DV, D_MODEL), const2),
            pl.BlockSpec((HG_HEADS * HG_DV, D_MODEL), const2),
            pl.BlockSpec((D_MODEL, D_MODEL), const2),
        ]
        args += list(precise)
    in_specs += [
        pl.BlockSpec((1, D_MODEL), const2),
        pl.BlockSpec((D_MODEL, LANES), const2),
        pl.BlockSpec((D_MODEL, LANES), const2),
    ]
    args += [fgain, rw_hi, rw_lo]
    return pl.pallas_call(
        functools.partial(_post_kernel, npt, precise is not None),
        grid=(t // tm,),
        in_specs=in_specs,
        out_specs=[
            pl.BlockSpec((tm, D_MODEL), lambda i: (i, 0)),
            pl.BlockSpec((tm * TILE_ROWS, LANES), lambda i: (i, 0)),
            pl.BlockSpec((tm, LANES), lambda i: (i, 0)),
            pl.BlockSpec((8, LANES), const2),
        ],
        out_shape=[
            jax.ShapeDtypeStruct((t, D_MODEL), F32),
            jax.ShapeDtypeStruct((t * TILE_ROWS, LANES), F32),
            jax.ShapeDtypeStruct((t, LANES), F32),
            jax.ShapeDtypeStruct((8, LANES), F32),
        ],
        scratch_shapes=[pltpu.VMEM((8, LANES), F32)],
        compiler_params=pltpu.CompilerParams(
            dimension_semantics=("arbitrary",), vmem_limit_bytes=VMEM_LIMIT),
        name="post",
    )(*args)


def _plan_kernel(pairs_per_step, n_blocks,
                 dest_ref, cnt_ref, slot_ref, bexp_ref, nused_ref):
    step = pl.program_id(0)
    base = step * pairs_per_step

    @pl.when(step == 0)
    def _():
        def per_expert(e, carry):
            start, last_e = carry
            nb = (cnt_ref[e] + (ROW_BLOCK - 1)) // ROW_BLOCK
            first = start // ROW_BLOCK

            def fill(bb, _):
                bexp_ref[first + bb] = e
                return 0

            lax.fori_loop(0, nb, fill, 0)
            return start + nb * ROW_BLOCK, jnp.where(nb > 0, e, last_e)

        total, last_e = lax.fori_loop(0, N_EXPERTS, per_expert, (jnp.int32(0), jnp.int32(0)))
        n_used = total // ROW_BLOCK
        nused_ref[0] = n_used

        def fill_tail(bb, _):
            bexp_ref[bb] = last_e
            return 0

        lax.fori_loop(n_used, n_blocks, fill_tail, 0)

        def init(r, _):
            slot_ref[r] = 0
            return 0

        lax.fori_loop(0, n_blocks * ROW_BLOCK, init, 0, unroll=16)

    def place(i, _):
        slot_ref[dest_ref[base + i]] = (base + i) >> 1
        return 0

    lax.fori_loop(0, pairs_per_step, place, 0, unroll=16)


def _plan(dest, counts, n_blocks):
    n_pairs = dest.shape[0]
    pairs_per_step = 2 * TOKEN_TILE
    assert n_pairs % pairs_per_step == 0
    smem = pl.BlockSpec(memory_space=pltpu.SMEM)
    return pl.pallas_call(
        functools.partial(_plan_kernel, pairs_per_step, n_blocks),
        grid=(n_pairs // pairs_per_step,),
        in_specs=[smem, smem],
        out_specs=[smem, smem, smem],
        out_shape=[
            jax.ShapeDtypeStruct((n_blocks * ROW_BLOCK,), jnp.int32),
            jax.ShapeDtypeStruct((n_blocks,), jnp.int32),
            jax.ShapeDtypeStruct((1,), jnp.int32),
        ],
        compiler_params=pltpu.CompilerParams(dimension_semantics=("arbitrary",)),
        name="plan",
    )(dest, counts)


def _expert_kernel(bexp_ref, nused_ref, slot_ref, h2_hbm, wg_ref, wu_ref, wd_ref, yb_ref,
                   xbuf, sems, wg_s, wu_s, wd_s):
    b = pl.program_id(0)
    n_used = nused_ref[0]

    def row_copy(tok, buf_slot, r):
        return pltpu.make_async_copy(
            h2_hbm.at[pl.ds(pl.multiple_of(tok * TILE_ROWS, TILE_ROWS), TILE_ROWS)],
            xbuf.at[buf_slot, pl.ds(pl.multiple_of(r * TILE_ROWS, TILE_ROWS), TILE_ROWS)],
            sems.at[buf_slot])

    def wait_block(block):
        buf_slot = block % 2
        pltpu.make_async_copy(h2_hbm.at[pl.ds(0, ROW_BLOCK * TILE_ROWS)], xbuf.at[buf_slot],
                              sems.at[buf_slot]).wait()

    def gather(block):
        def issue(r, _):
            row_copy(slot_ref[block * ROW_BLOCK + r], block % 2, r).start()
            return 0

        lax.fori_loop(0, ROW_BLOCK, issue, 0, unroll=8)

    @pl.when(b == 0)
    def _():
        gather(b)

    prev = bexp_ref[jnp.maximum(b - 1, 0)]

    @pl.when((b == 0) | (bexp_ref[b] != prev))
    def _():
        wg_s[...] = wg_ref[...].astype(BF16)
        wu_s[...] = wu_ref[...].astype(BF16)
        wd_s[...] = wd_ref[...].astype(BF16)

    @pl.when(b < n_used)
    def _():
        buf_slot = b % 2
        wait_block(b)
        gather(b + 1)
        x = jnp.concatenate(
            [xbuf[buf_slot, pl.ds(kk, ROW_BLOCK, stride=TILE_ROWS), :] for kk in range(TILE_ROWS)],
            axis=1).astype(BF16)
        g = _dot(x, wg_s[...])
        u = _dot(x, wu_s[...])
        a = (g * _sigmoid(g) * u).astype(BF16)
        y = _dot(a, wd_s[...])
        for kk in range(TILE_ROWS):
            yb_ref[pl.ds(kk, ROW_BLOCK, stride=TILE_ROWS), :] = y[:, kk * LANES:(kk + 1) * LANES]

    @pl.when(b == n_used)
    def _():
        wait_block(b)

    @pl.when(b >= n_used)
    def _():
        yb_ref[...] = jnp.zeros_like(yb_ref)


def _experts(layer, bexp, nused, slot_tok, h2, w_gate, w_up, w_down):
    n_blocks = bexp.shape[0]
    assert slot_tok.shape[0] == n_blocks * ROW_BLOCK
    wmap = lambda b, be, nu, st: (layer, be[b], 0, 0)
    return pl.pallas_call(
        _expert_kernel,
        grid_spec=pltpu.PrefetchScalarGridSpec(
            num_scalar_prefetch=3,
            grid=(n_blocks,),
            in_specs=[
                pl.BlockSpec(memory_space=pl.ANY),
                pl.BlockSpec((None, None, D_MODEL, D_EXPERT), wmap),
                pl.BlockSpec((None, None, D_MODEL, D_EXPERT), wmap),
                pl.BlockSpec((None, None, D_EXPERT, D_MODEL), wmap),
            ],
            out_specs=pl.BlockSpec((ROW_BLOCK * TILE_ROWS, LANES), lambda b, be, nu, st: (b, 0)),
            scratch_shapes=[
                pltpu.VMEM((2, ROW_BLOCK * TILE_ROWS, LANES), F32),
                pltpu.SemaphoreType.DMA((2,)),
                pltpu.VMEM((D_MODEL, D_EXPERT), BF16),
                pltpu.VMEM((D_MODEL, D_EXPERT), BF16),
                pltpu.VMEM((D_EXPERT, D_MODEL), BF16),
            ],
        ),
        out_shape=jax.ShapeDtypeStruct((n_blocks * ROW_BLOCK * TILE_ROWS, LANES), F32),
        compiler_params=pltpu.CompilerParams(
            dimension_semantics=("arbitrary",), vmem_limit_bytes=VMEM_LIMIT),
        name="experts",
    )(bexp, nused, slot_tok, h2, w_gate, w_up, w_down)


def _combine_kernel(final_tiles, dest_ref, route_ref, x1_ref, yb_hbm, fg_ref, *rest):
    if final_tiles is not None:
        yp_ref, ys_ref, buf, sems = rest
    else:
        x2_ref, buf, sems = rest
    i = pl.program_id(0)
    tm = x1_ref.shape[0]

    def row_copy(d, buf_slot, r):
        return pltpu.make_async_copy(
            yb_hbm.at[pl.ds(pl.multiple_of(d * TILE_ROWS, TILE_ROWS), TILE_ROWS)],
            buf.at[buf_slot, pl.ds(pl.multiple_of(r * TILE_ROWS, TILE_ROWS), TILE_ROWS)],
            sems.at[buf_slot])

    def gather(tile):
        buf_slot = tile % 2
        base = tile * tm

        def issue(tt, _):
            row_copy(dest_ref[2 * (base + tt)], buf_slot, tt).start()
            row_copy(dest_ref[2 * (base + tt) + 1], buf_slot, tm + tt).start()
            return 0

        lax.fori_loop(0, tm, issue, 0, unroll=4)

    @pl.when(i == 0)
    def _():
        gather(i)

    @pl.when(i + 1 < pl.num_programs(0))
    def _():
        gather(i + 1)

    buf_slot = i % 2
    pltpu.make_async_copy(yb_hbm.at[pl.ds(0, 2 * tm * TILE_ROWS)], buf.at[buf_slot], sems.at[buf_slot]).wait()

    w1 = route_ref[:, 2:3]
    w2 = route_ref[:, 3:4]
    parts = []
    for kk in range(TILE_ROWS):
        y = (w1 * buf[buf_slot, pl.ds(kk, tm, stride=TILE_ROWS), :]
             + w2 * buf[buf_slot, pl.ds(tm * TILE_ROWS + kk, tm, stride=TILE_ROWS), :])
        parts.append(x1_ref[:, kk * LANES:(kk + 1) * LANES] + y)
    x2 = jnp.concatenate(parts, axis=1)
    if final_tiles is None:
        x2_ref[...] = x2
    else:
        first_real, n_prompt_tiles = final_tiles
        yn = x2 * lax.rsqrt(jnp.mean(x2 * x2, axis=-1, keepdims=True) + EPS) * fg_ref[...]

        @pl.when((i >= first_real) & (i < n_prompt_tiles))
        def _():
            yp_ref[...] = yn

        @pl.when(i >= n_prompt_tiles)
        def _():
            ys_ref[...] = yn


def _combine(dest, route, x1, yb, fgain, final_rows=None):
    t = x1.shape[0]
    tm = COMBINE_TILE
    row_spec = pl.BlockSpec((tm, D_MODEL), lambda i, dst: (i, 0))
    if final_rows is None:
        final_tiles = None
        out_specs = [row_spec]
        out_shape = [jax.ShapeDtypeStruct((t, D_MODEL), F32)]
    else:
        first_row, n_prompt = final_rows
        assert first_row % tm == 0 and n_prompt % tm == 0
        first_real, npt = first_row // tm, n_prompt // tm
        final_tiles = (first_real, npt)
        out_specs = [
            pl.BlockSpec((tm, D_MODEL), lambda i, dst: (jnp.clip(i - first_real, 0, npt - first_real - 1), 0)),
            pl.BlockSpec((tm, D_MODEL), lambda i, dst: (jnp.maximum(i - npt, 0), 0)),
        ]
        out_shape = [jax.ShapeDtypeStruct((n_prompt - first_row, D_MODEL), F32),
                     jax.ShapeDtypeStruct((t - n_prompt, D_MODEL), F32)]
    return pl.pallas_call(
        functools.partial(_combine_kernel, final_tiles),
        grid_spec=pltpu.PrefetchScalarGridSpec(
            num_scalar_prefetch=1,
            grid=(t // tm,),
            in_specs=[
                pl.BlockSpec((tm, LANES), lambda i, dst: (i, 0)),
                row_spec,
                pl.BlockSpec(memory_space=pl.ANY),
                pl.BlockSpec((1, D_MODEL), lambda i, dst: (0, 0)),
            ],
            out_specs=out_specs,
            scratch_shapes=[
                pltpu.VMEM((2, 2 * tm * TILE_ROWS, LANES), F32),
                pltpu.SemaphoreType.DMA((2,)),
            ],
        ),
        out_shape=out_shape,
        compiler_params=pltpu.CompilerParams(
            dimension_semantics=("arbitrary",), vmem_limit_bytes=VMEM_LIMIT),
        name="combine",
    )(dest, route, x1, yb, fgain)


def _rotary_tables(pos):
    half = RET_DK // 2
    inv_freq = 1.0 / (ROPE_BASE ** jnp.linspace(0.0, 1.0, half, dtype=F32))
    ang = pos.astype(F32)[:, None] * inv_freq[None, :]
    return jnp.cos(ang), jnp.sin(ang)


def _decay_tables(chunk):
    log_gamma = jnp.log1p(-jnp.exp2(-5.0 - jnp.arange(RET_HEADS, dtype=F32)))
    idx = jnp.arange(chunk, dtype=F32)
    rel = idx[:, None] - idx[None, :]
    causal = rel >= 0
    dint = jnp.where(causal, jnp.exp(log_gamma[:, None, None] * jnp.where(causal, rel, 0.0)), 0.0)
    qdec = jnp.exp(log_gamma[None, :] * (idx[:, None] + 1.0))
    kdec = jnp.exp(log_gamma[None, :] * (chunk - 1.0 - idx[:, None]))
    cdec = jnp.exp(log_gamma * chunk)
    qdec = jnp.repeat(qdec, RET_DK, axis=1)
    kdec = jnp.repeat(kdec, RET_DK, axis=1)
    return dint, qdec, kdec, cdec


def _split_param_kernel(w_ref, hi_ref, lo_ref):
    hi, lo = _split(w_ref[...])
    hi_ref[...] = hi
    lo_ref[...] = lo


def _split_param(w):
    rows, cols = w.shape
    tm = TOKEN_TILE
    assert rows % tm == 0 and cols % LANES == 0
    spec = pl.BlockSpec((tm, cols), lambda i: (i, 0))
    return pl.pallas_call(
        _split_param_kernel,
        grid=(rows // tm,),
        in_specs=[spec],
        out_specs=[spec, spec],
        out_shape=[jax.ShapeDtypeStruct(w.shape, BF16)] * 2,
        name="split_param",
    )(w)


def _router_split(router_group_l, router_expert_l):
    rw = jnp.concatenate(
        [router_expert_l, router_group_l,
         jnp.zeros((D_MODEL, LANES - N_EXPERTS - N_GROUPS), F32)], axis=1)
    return _split_param(rw)


def kernel(x_prompt, x_sample, state_ret, state_hgrn, meta_tokens, mix_norm, w_in, hg_lb_logits, hg_norm,
           w_ret_branch, w_hg_branch, w_out, ffn_norm, router_group, router_expert, w_gate, w_up, w_down,
           final_norm):
    depth = w_in.shape[0]
    bp, seq, d = x_prompt.shape
    bs, dec, _ = x_sample.shape
    assert bp == 1 and dec == CHUNK and d == D_MODEL and seq % CHUNK == 0

    n_real = N_META + seq
    n_sample = bs * dec
    n_pad = (-(n_real + n_sample)) % PROJ_TM
    while (n_pad + n_real) % PROMPT_CHUNK:
        n_pad += PROJ_TM
    n_prompt = n_pad + n_real
    assert n_pad > 0
    t = n_prompt + n_sample
    n_blocks = -(-(2 * t) // ROW_BLOCK) + N_EXPERTS + 1

    x = jnp.concatenate([
        jnp.zeros((n_pad, d), F32),
        meta_tokens.astype(F32),
        x_prompt[0],
        x_sample.reshape(n_sample, d),
    ], axis=0)

    rot_prompt = _rotary_tables(jnp.arange(n_prompt) - (n_pad + N_META))
    rot_sample = _rotary_tables(PAST_LEN + jnp.arange(dec))
    dec_prompt = _decay_tables(PROMPT_CHUNK)
    dec_sample = _decay_tables(dec)

    prob = jax.nn.softmax(hg_lb_logits.astype(F32), axis=0)
    cum = jnp.cumsum(prob, axis=0)
    lb_all = cum - cum[0:1]

    assert state_ret.dtype == F32 and state_hgrn.dtype == F32 and w_in.dtype == F32
    states = [jnp.zeros((depth, 1) + state_ret.shape[2:], F32), jnp.zeros((depth, 1) + state_hgrn.shape[2:], F32),
              jnp.zeros(state_ret.shape, F32), jnp.zeros(state_hgrn.shape, F32)]
    for l in range(depth):
        lb = lb_all[l]
        lbt = jnp.concatenate(
            [jnp.log(lb)[None], jnp.log1p(-lb)[None], (1.0 - lb)[None], jnp.zeros((5, d), F32)], axis=0)
        p = _proj(l, x, mix_norm[l][None], w_in)
        hgain = hg_norm[l][None].astype(F32)
        precise = l < depth - 1
        branch_w = [_split_param(w[l]) for w in (w_ret_branch, w_hg_branch, w_out)]
        if precise:
            n_plain = n_prompt - PRECISE_TAIL
            p_s = _proj_precise(l, x, n_plain, PRECISE_TAIL + n_sample, mix_norm[l][None], w_in)
            ro_p, ho_p, ret_p, hg_p = _mixer(l, p, PROMPT_CHUNK, 0, n_plain, rot_prompt, dec_prompt, lbt, hgain,
                                             states[:2])
            rot_tail = tuple(r[n_plain:] for r in rot_prompt)
            ro_t, ho_t, ret_p, hg_p = _mixer(l, p_s, PROMPT_CHUNK, 0, PRECISE_TAIL, rot_tail, dec_prompt, lbt, hgain,
                                             [ret_p, hg_p], states_in=(ret_p, hg_p), precise=True, mode="continue")
            ro_s, ho_s, ret_s, hg_s = _mixer(l, p_s, dec, PRECISE_TAIL, n_sample, rot_sample, dec_sample, lbt, hgain,
                                             states[2:], states_in=(state_ret, state_hgrn), precise=True)
            ro_s = jnp.concatenate([ro_t, ro_s], axis=0)
            ho_s = jnp.concatenate([ho_t, ho_s], axis=0)
            post_precise = (p_s,) + tuple(lo for _, lo in branch_w)
        else:
            ro_p, ho_p, ret_p, hg_p = _mixer(l, p, PROMPT_CHUNK, 0, n_prompt, rot_prompt, dec_prompt, lbt, hgain,
                                             states[:2])
            ro_s, ho_s, ret_s, hg_s = _mixer(l, p, dec, n_prompt, n_sample, rot_sample, dec_sample, lbt, hgain,
                                             states[2:], states_in=(state_ret, state_hgrn))
            post_precise = None
        states = [ret_p, hg_p, ret_s, hg_s]
        rw_hi, rw_lo = _router_split(router_group[l], router_expert[l])
        x1, h2, route, counts = _post(
            p, ro_p, ro_s, ho_p, ho_s, x, branch_w[0][0], branch_w[1][0], branch_w[2][0],
            ffn_norm[l][None], rw_hi, rw_lo, precise=post_precise)
        ids = route[:, 0:2].astype(jnp.int32)
        ranks = route[:, 4:6].astype(jnp.int32)
        cnt = counts[0, :N_EXPERTS].astype(jnp.int32)
        padded = (cnt + (ROW_BLOCK - 1)) // ROW_BLOCK * ROW_BLOCK
        pstart = jnp.cumsum(padded) - padded
        dest = (pstart[ids] + ranks).reshape(-1)
        slot_tok, bexp, nused = _plan(dest, cnt, n_blocks)
        yb = _experts(l, bexp, nused, slot_tok, h2, w_gate, w_up, w_down)
        if l < depth - 1:
            x, = _combine(dest, route, x1, yb, final_norm[None])
        else:
            y_prompt, y_sample = _combine(dest, route, x1, yb, final_norm[None],
                                          final_rows=(n_prompt - seq, n_prompt))

    ret_prompt, hgrn_prompt, ret_sample, hgrn_sample = states
    return (y_prompt.reshape(bp, seq, d), y_sample.reshape(bs, dec, d),
            ret_prompt, hgrn_prompt, ret_sample, hgrn_sample)
```

```python
import functools

import jax
import jax.numpy as jnp
from jax import lax
from jax.experimental import pallas as pl
from jax.experimental.pallas import tpu as pltpu

F32 = jnp.float32
BF16 = jnp.bfloat16

D_MODEL = 1024
CHUNK = 64
N_META = 16
RET_HEADS = 4
RET_DK = 256
RET_DV = 512
HG_HEADS = 8
HG_DK = 128
HG_DV = 128
N_GROUPS = 4
EXPERTS_PER_GROUP = 8
N_EXPERTS = 32
D_EXPERT = 512
ROPE_BASE = 10000.0
EPS = 1e-6
PAST_LEN = 4096
LOG2_E = 1.4426950408889634
MAX_SUBBLOCK_DROP_LOG2 = 100.0

C_RQ, C_RK, C_RV, C_RG = 0, 1024, 2048, 4096
C_HQ, C_HF, C_HI, C_HG = 6144, 7168, 8192, 9216
C_GATES = 10240
PROJ_WIDTH = 12288
MIX_WIDTH = C_GATES

TOKEN_TILE = 256
PROJ_TN = 2048
PROJ_TM = 768
PROJ_PRECISE_TN = 2048
ROW_BLOCK = 256
COMBINE_TILE = 256
SUB = 16
HALF = 64
PROMPT_CHUNK = 128
PRECISE_TAIL = 256
LANES = 128
TILE_ROWS = 8
VMEM_LIMIT = 56 * 1024 * 1024


def _sigmoid(x):
    return 1.0 / (1.0 + jnp.exp(-x))


def _dot(a, b):
    return jnp.dot(a, b, preferred_element_type=F32)


def _dot_nt(a, b):
    return lax.dot_general(a, b, (((1,), (1,)), ((), ())), preferred_element_type=F32)


def _dot_tn(a, b):
    return lax.dot_general(a, b, (((0,), (0,)), ((), ())), preferred_element_type=F32)


def _split(a):
    hi = a.astype(BF16)
    return hi, (a - hi.astype(F32)).astype(BF16)


def _mm(dot, a, b, precise):
    if not precise:
        return dot(a.astype(BF16), b.astype(BF16))
    ah, al = _split(a)
    bh, bl = _split(b)
    return dot(ah, bh) + (dot(ah, bl) + dot(al, bh))


def _mm_w(a, w_hi, w_lo, precise):
    if not precise:
        return _dot(a.astype(BF16), w_hi)
    ah, al = _split(a)
    return _dot(ah, w_hi) + (_dot(ah, w_lo) + _dot(al, w_hi))


def _deinterleave_heads(w_bf):
    half = RET_DK // 2
    r = lax.broadcasted_iota(jnp.int32, (RET_DK, RET_DK), 0)
    c = lax.broadcasted_iota(jnp.int32, (RET_DK, RET_DK), 1)
    src = jnp.where(c < half, 2 * c, 2 * (c - half) + 1)
    pick = jnp.where(r == src, 1.0, 0.0).astype(BF16)
    blocks = [_dot(w_bf[:, h * RET_DK:(h + 1) * RET_DK], pick).astype(BF16)
              for h in range(w_bf.shape[1] // RET_DK)]
    return jnp.concatenate(blocks, axis=1)


def _proj_kernel(qk_tiles, x_ref, g_ref, w_ref, o_ref, wbf_ref):
    j = pl.program_id(0)
    i = pl.program_id(1)

    @pl.when((i == 0) & (j < qk_tiles))
    def _():
        wbf_ref[...] = _deinterleave_heads(w_ref[...].astype(BF16))

    @pl.when((i == 0) & (j >= qk_tiles))
    def _():
        wbf_ref[...] = w_ref[...].astype(BF16)

    x = x_ref[...]
    xn = x * lax.rsqrt(jnp.mean(x * x, axis=-1, keepdims=True) + EPS) * g_ref[...]
    o_ref[...] = _dot(xn.astype(BF16), wbf_ref[...])


def _proj(layer, x, gain, w_in):
    t = x.shape[0]
    tm, tn = PROJ_TM, PROJ_TN
    assert t % tm == 0 and C_RV % tn == 0
    return pl.pallas_call(
        functools.partial(_proj_kernel, C_RV // tn),
        grid=(PROJ_WIDTH // tn, t // tm),
        in_specs=[
            pl.BlockSpec((tm, D_MODEL), lambda j, i: (i, 0)),
            pl.BlockSpec((1, D_MODEL), lambda j, i: (0, 0)),
            pl.BlockSpec((None, D_MODEL, tn), lambda j, i: (layer, 0, j)),
        ],
        out_specs=pl.BlockSpec((tm, tn), lambda j, i: (i, j)),
        out_shape=jax.ShapeDtypeStruct((t, PROJ_WIDTH), F32),
        scratch_shapes=[pltpu.VMEM((D_MODEL, tn), BF16)],
        compiler_params=pltpu.CompilerParams(
            dimension_semantics=("arbitrary", "arbitrary"), vmem_limit_bytes=VMEM_LIMIT),
        name="proj",
    )(x, gain, w_in)


def _proj_precise_kernel(qk_tiles, x_ref, g_ref, w_ref, o_ref, whi_ref, wlo_ref):
    j = pl.program_id(0)
    i = pl.program_id(1)

    @pl.when((i == 0) & (j < qk_tiles))
    def _():
        hi, lo = _split(w_ref[...])
        whi_ref[...] = _deinterleave_heads(hi)
        wlo_ref[...] = _deinterleave_heads(lo)

    @pl.when((i == 0) & (j >= qk_tiles))
    def _():
        hi, lo = _split(w_ref[...])
        whi_ref[...] = hi
        wlo_ref[...] = lo

    x = x_ref[...]
    xn = x * lax.rsqrt(jnp.mean(x * x, axis=-1, keepdims=True) + EPS) * g_ref[...]
    o_ref[...] = _mm_w(xn, whi_ref[...], wlo_ref[...], True)


def _proj_precise(layer, x, first_row, n_rows, gain, w_in):
    tm, tn = TOKEN_TILE, PROJ_PRECISE_TN
    assert first_row % tm == 0 and n_rows % tm == 0 and C_RV % tn == 0
    first_tile = first_row // tm
    return pl.pallas_call(
        functools.partial(_proj_precise_kernel, C_RV // tn),
        grid=(PROJ_WIDTH // tn, n_rows // tm),
        in_specs=[
            pl.BlockSpec((tm, D_MODEL), lambda j, i: (first_tile + i, 0)),
            pl.BlockSpec((1, D_MODEL), lambda j, i: (0, 0)),
            pl.BlockSpec((None, D_MODEL, tn), lambda j, i: (layer, 0, j)),
        ],
        out_specs=pl.BlockSpec((tm, tn), lambda j, i: (i, j)),
        out_shape=jax.ShapeDtypeStruct((n_rows, PROJ_WIDTH), F32),
        scratch_shapes=[pltpu.VMEM((D_MODEL, tn), BF16), pltpu.VMEM((D_MODEL, tn), BF16)],
        compiler_params=pltpu.CompilerParams(
            dimension_semantics=("arbitrary", "arbitrary"), vmem_limit_bytes=VMEM_LIMIT),
        name="proj_precise",
    )(x, gain, w_in)


def _mixer_kernel(chunk, mode, precise,
                  p_ref, cos_ref, sin_ref, dint_ref, qdec_ref, kdec_ref, cdec_ref,
                  lbt_ref, hgain_ref, *rest):
    if mode != "prompt":
        sr_in_ref, sh_in_ref = rest[:2]
        rest = rest[2:]
    (ro_ref, ho_ref, sr_out_ref, sh_out_ref,
     sr_scr, sh_scr, hq_scr, hb_scr, hk_scr, ho_scr) = rest[2:]
    c = pl.program_id(0)
    half = RET_DK // 2

    def load_states():
        sr_scr[...] = sr_in_ref[...]
        for h in range(HG_HEADS):
            sh_scr[h] = sh_in_ref[h].T

    if mode == "sample":
        load_states()
    elif mode == "continue":
        pl.when(c == 0)(load_states)
    else:
        @pl.when(c == 0)
        def _():
            sr_scr[...] = jnp.zeros_like(sr_scr)
            sh_scr[...] = jnp.zeros_like(sh_scr)

    cos = cos_ref[...]
    sin = sin_ref[...]
    scores_v, to_state = [], []
    for h in range(RET_HEADS):
        q = p_ref[:, C_RQ + h * RET_DK:C_RQ + (h + 1) * RET_DK]
        q1, q2 = q[:, :half], q[:, half:]
        qr = jnp.concatenate([q1 * cos - q2 * sin, q1 * sin + q2 * cos], axis=1)
        to_state.append(qr * qdec_ref[:, h * RET_DK:(h + 1) * RET_DK])
        scores_v.append(qr)
    for h in range(RET_HEADS):
        k = p_ref[:, C_RK + h * RET_DK:C_RK + (h + 1) * RET_DK]
        k1, k2 = k[:, :half], k[:, half:]
        kr = jnp.concatenate([k1 * cos - k2 * sin, k1 * sin + k2 * cos], axis=1) * (RET_DK ** -0.5)
        to_state.append(kr * kdec_ref[:, h * RET_DK:(h + 1) * RET_DK])
        scores_v[h] = _mm(_dot_nt, scores_v[h], kr, precise) * dint_ref[h]
    pr = lax.broadcasted_iota(jnp.int32, (RET_DK, RET_DK), 0)
    pc = lax.broadcasted_iota(jnp.int32, (RET_DK, RET_DK), 1)
    src_lane = jnp.where(pr < half, 2 * pr, 2 * (pr - half) + 1)
    perm = jnp.where(pc == src_lane, 1.0, 0.0)
    natural = _mm(_dot, jnp.concatenate(to_state, axis=0), perm, precise)
    for h in range(RET_HEADS):
        v = p_ref[:, C_RV + h * RET_DV:C_RV + (h + 1) * RET_DV]
        s = sr_scr[h]
        qd = natural[h * chunk:(h + 1) * chunk]
        kd = natural[(RET_HEADS + h) * chunk:(RET_HEADS + h + 1) * chunk]
        o = _mm(_dot, scores_v[h], v, precise) + _mm(_dot, qd, s, precise)
        sr_scr[h] = cdec_ref[h] * s + _mm(_dot_tn, kd, v, precise)
        on = o * lax.rsqrt(jnp.mean(o * o, axis=-1, keepdims=True) + EPS)
        g = p_ref[:, C_RG + h * RET_DV:C_RG + (h + 1) * RET_DV]
        ro_ref[:, h * RET_DV:(h + 1) * RET_DV] = (on * (g * _sigmoid(g))).astype(ro_ref.dtype)

    hf = p_ref[:, C_HF:C_HF + D_MODEL]
    log_lb = lbt_ref[0:1, :]
    log1m_lb = lbt_ref[1:2, :]
    one_m_lb = lbt_ref[2:3, :]
    log_sig = jnp.minimum(hf, 0.0) - jnp.log(1.0 + jnp.exp(-jnp.abs(hf)))
    b_ = log1m_lb + log_sig
    log_f = jnp.maximum(log_lb, b_) + jnp.log(1.0 + jnp.exp(-jnp.abs(log_lb - b_)))
    k_in = one_m_lb * _sigmoid(-hf)
    hq = p_ref[:, C_HQ:C_HQ + D_MODEL]
    q_all = hq * _sigmoid(hq)

    tr = lax.broadcasted_iota(jnp.int32, (chunk, chunk), 0)
    tc = lax.broadcasted_iota(jnp.int32, (chunk, chunk), 1)
    tri = jnp.where(tr >= tc, 1.0, 0.0).astype(BF16)
    f_hi = log_f.astype(BF16)
    r1 = log_f - f_hi.astype(F32)
    f_mid = r1.astype(BF16)
    f_lo = (r1 - f_mid.astype(F32)).astype(BF16)
    hb_scr[...] = (_dot(tri, f_hi) + _dot(tri, f_mid) + _dot(tri, f_lo)) * LOG2_E
    hq_scr[...] = q_all
    hk_scr[...] = k_in

    n_sub = HALF // SUB
    ri = lax.broadcasted_iota(jnp.int32, (HALF, HALF), 0)
    ci = lax.broadcasted_iota(jnp.int32, (HALF, HALF), 1)
    row = lax.broadcasted_iota(jnp.int32, (HALF, HG_DK), 0)
    col_sub = lax.broadcasted_iota(jnp.int32, (SUB, HALF), 1)
    hgain = hgain_ref[...]

    drops = [hb_scr[SUB * ib:SUB * ib + 1, :] - hb_scr[SUB * (ib + 1) - 1:SUB * (ib + 1), :]
             for ib in range(chunk // SUB)]
    max_drop = jnp.max(jnp.concatenate(drops, axis=0))
    factorise_ok = max_drop <= MAX_SUBBLOCK_DROP_LOG2

    def half_scores(q, k, b, factorised):
        b_end = [b[SUB * (jb + 1) - 1:SUB * (jb + 1), :] for jb in range(n_sub)]
        b_end_rows = jnp.concatenate(
            [jnp.broadcast_to(b_end[jb], (SUB, HG_DK)) for jb in range(n_sub)], axis=0)
        k_hat = k * jnp.exp2(b_end_rows - b)
        lhs, rhs = [], []
        for jb in range(n_sub - 1):
            q_hat = q * jnp.exp2(b - b_end[jb])
            lhs.append(jnp.where(row >= SUB * (jb + 1), q_hat, 0.0))
            rhs.append(jnp.where((row >= SUB * jb) & (row < SUB * (jb + 1)), k_hat, 0.0))
        a_off = _mm(_dot_nt, jnp.concatenate(lhs, axis=1), jnp.concatenate(rhs, axis=1), precise)
        if factorised:
            b_first = jnp.concatenate(
                [jnp.broadcast_to(b[SUB * ib:SUB * ib + 1, :], (SUB, HG_DK)) for ib in range(n_sub)], axis=0)
            q_t = q * jnp.exp2(b - b_first)
            k_t = k * jnp.exp2(b_first - b)
            same_block = (ri // SUB) == (ci // SUB)
            a_diag = jnp.where(same_block & (ri >= ci), _mm(_dot_nt, q_t, k_t, precise), 0.0)
        else:
            diag = []
            for ib in range(n_sub):
                r0 = SUB * ib
                b_i = b[r0:r0 + SUB, :]
                q_i = q[r0:r0 + SUB, :]
                blk = jnp.zeros((SUB, HALF), F32)
                for j in range(SUB):
                    b_j = b[r0 + j:r0 + j + 1, :]
                    k_j = k[r0 + j:r0 + j + 1, :]
                    sc = jnp.sum(jnp.exp2(b_i - b_j) * q_i * k_j, axis=-1, keepdims=True)
                    blk = jnp.where(col_sub == r0 + j, sc, blk)
                diag.append(blk)
            a_diag = jnp.where(ri >= ci, jnp.concatenate(diag, axis=0), 0.0)
        return a_off + a_diag

    def hgrn_head(h, factorised):
        sl = slice(h * HG_DK, (h + 1) * HG_DK)
        q = hq_scr[:, sl]
        b = hb_scr[:, sl]
        k = hk_scr[:, sl]
        vb = p_ref[:, C_HI + h * HG_DV:C_HI + (h + 1) * HG_DV]
        if factorised:
            st = sh_scr[h]
            o = _mm(_dot_nt, q * jnp.exp2(b), st, precise)
            ho_scr[:, sl] = o
            b_last = b[chunk - 1:chunk, :]
            kd = k * jnp.exp2(b_last - b)
            sh_scr[h] = st * jnp.exp2(b_last) + _mm(_dot_tn, vb, kd, precise)
        else:
            o = ho_scr[:, sl]
        intra = []
        for hh in range(chunk // HALF):
            lo, hi = hh * HALF, (hh + 1) * HALF
            a = half_scores(q[lo:hi], k[lo:hi], b[lo:hi], factorised)
            if hh == 0:
                intra.append(_mm(_dot, a, vb[lo:hi], precise))
            else:
                b_mid = b[lo - 1:lo, :]
                q_c = q[lo:hi] * jnp.exp2(b[lo:hi] - b_mid)
                k_c = k[lo - HALF:lo] * jnp.exp2(b_mid - b[lo - HALF:lo])
                both = jnp.concatenate([_mm(_dot_nt, q_c, k_c, precise), a], axis=1)
                intra.append(_mm(_dot, both, vb[lo - HALF:hi], precise))
        o = o + jnp.concatenate(intra, axis=0)
        on = o * lax.rsqrt(jnp.mean(o * o, axis=-1, keepdims=True) + EPS) * hgain
        g = p_ref[:, C_HG + h * HG_DV:C_HG + (h + 1) * HG_DV]
        ho_ref[:, h * HG_DV:(h + 1) * HG_DV] = (on * (g * _sigmoid(g))).astype(ho_ref.dtype)

    for h in range(HG_HEADS):
        hgrn_head(h, True)

    @pl.when(jnp.logical_not(factorise_ok))
    def _():
        for h in range(HG_HEADS):
            hgrn_head(h, False)

    def emit_states():
        sr_out_ref[...] = sr_scr[...]
        for h in range(HG_HEADS):
            sh_out_ref[h] = sh_scr[h].T

    if mode == "sample":
        emit_states()
    else:
        pl.when(c == pl.num_programs(0) - 1)(emit_states)


def _mixer(layer, p, chunk, first_row, n_rows, rotary, decays, lbt, hgain, state_out_prev, states_in=None,
           precise=False, mode=None):
    assert chunk in (HALF, 2 * HALF) and first_row % chunk == 0 and n_rows % chunk == 0
    if mode is None:
        mode = "prompt" if states_in is None else "sample"
    is_sample = mode == "sample"
    first_chunk, n_chunks = first_row // chunk, n_rows // chunk
    cos, sin = rotary
    dint, qdec, kdec, cdec = decays
    rot_idx = (lambda c: (0, 0)) if is_sample else (lambda c: (c, 0))
    seq_idx = (lambda c: (layer, c, 0, 0, 0)) if is_sample else (lambda c: (layer, 0, 0, 0, 0))
    const2 = lambda c: (0, 0)
    ret_blk = (None, None, RET_HEADS, RET_DK, RET_DV)
    hg_blk = (None, None, HG_HEADS, HG_DK, HG_DV)
    in_specs = [
        pl.BlockSpec((chunk, MIX_WIDTH), lambda c: (first_chunk + c, 0)),
        pl.BlockSpec((chunk, LANES), rot_idx),
        pl.BlockSpec((chunk, LANES), rot_idx),
        pl.BlockSpec((RET_HEADS, chunk, chunk), lambda c: (0, 0, 0)),
        pl.BlockSpec((chunk, RET_HEADS * RET_DK), const2),
        pl.BlockSpec((chunk, RET_HEADS * RET_DK), const2),
        pl.BlockSpec(memory_space=pltpu.SMEM),
        pl.BlockSpec((8, D_MODEL), const2),
        pl.BlockSpec((1, HG_DV), const2),
    ]
    args = [p, cos, sin, dint, qdec, kdec, cdec, lbt, hgain]
    if mode != "prompt":
        in_specs += [pl.BlockSpec(ret_blk, seq_idx), pl.BlockSpec(hg_blk, seq_idx)]
        args += list(states_in)
    aliases = {}
    for n, prev in enumerate(state_out_prev):
        aliases[len(args)] = 2 + n
        in_specs.append(pl.BlockSpec(memory_space=pl.ANY))
        args.append(prev)
    return pl.pallas_call(
        functools.partial(_mixer_kernel, chunk, mode, precise),
        grid=(n_chunks,),
        in_specs=in_specs,
        out_specs=[
            pl.BlockSpec((chunk, RET_HEADS * RET_DV), lambda c: (c, 0)),
            pl.BlockSpec((chunk, HG_HEADS * HG_DV), lambda c: (c, 0)),
            pl.BlockSpec(ret_blk, seq_idx),
            pl.BlockSpec(hg_blk, seq_idx),
        ],
        out_shape=[
            jax.ShapeDtypeStruct((n_rows, RET_HEADS * RET_DV), F32 if precise else BF16),
            jax.ShapeDtypeStruct((n_rows, HG_HEADS * HG_DV), F32 if precise else BF16),
            jax.ShapeDtypeStruct(state_out_prev[0].shape, F32),
            jax.ShapeDtypeStruct(state_out_prev[1].shape, F32),
        ],
        input_output_aliases=aliases,
        scratch_shapes=[
            pltpu.VMEM((RET_HEADS, RET_DK, RET_DV), F32),
            pltpu.VMEM((HG_HEADS, HG_DV, HG_DK), F32),
            pltpu.VMEM((chunk, HG_HEADS * HG_DK), F32),
            pltpu.VMEM((chunk, HG_HEADS * HG_DK), F32),
            pltpu.VMEM((chunk, HG_HEADS * HG_DK), F32),
            pltpu.VMEM((chunk, HG_HEADS * HG_DV), F32),
        ],
        compiler_params=pltpu.CompilerParams(
            dimension_semantics=("arbitrary",), vmem_limit_bytes=VMEM_LIMIT),
        name="mixer_" + mode,
    )(*args)


def _post_kernel(n_prompt_tiles, precise_samples, gp_ref, rop_ref, ros_ref, hop_ref, hos_ref, x_ref,
                 wr_ref, wh_ref, wo_ref, *rest):
    if precise_samples:
        gps_ref, wrl_ref, whl_ref, wol_ref = rest[:4]
        rest = rest[4:]
    fg_ref, rwh_ref, rwl_ref, x1_ref, h2_ref, route_ref, cnt_ref, run_scr = rest
    i = pl.program_id(0)
    tm = x_ref.shape[0]

    @pl.when(i == 0)
    def _():
        run_scr[...] = jnp.zeros_like(run_scr)

    def mix(gp, ro, ho, w_lo, precise):
        gate_r = _sigmoid(gp[:, 0:D_MODEL])
        gate_h = _sigmoid(gp[:, D_MODEL:2 * D_MODEL])
        merged = (gate_r * _mm_w(ro, wr_ref[...], w_lo[0], precise)
                  + gate_h * _mm_w(ho, wh_ref[...], w_lo[1], precise))
        x1_ref[...] = x_ref[...] + _mm_w(merged, wo_ref[...], w_lo[2], precise)

    @pl.when(i < n_prompt_tiles)
    def _():
        mix(gp_ref[...], rop_ref[...], hop_ref[...], (None, None, None), False)

    @pl.when(i >= n_prompt_tiles)
    def _():
        if precise_samples:
            mix(gps_ref[...], ros_ref[...], hos_ref[...], (wrl_ref[...], whl_ref[...], wol_ref[...]), True)
        else:
            mix(gp_ref[...], ros_ref[...], hos_ref[...], (None, None, None), False)

    x1 = x1_ref[...]
    h2 = x1 * lax.rsqrt(jnp.mean(x1 * x1, axis=-1, keepdims=True) + EPS) * fg_ref[...]
    for kk in range(TILE_ROWS):
        h2_ref[pl.ds(kk, tm, stride=TILE_ROWS), :] = h2[:, kk * LANES:(kk + 1) * LANES]

    h_hi = h2.astype(BF16)
    h_lo = (h2 - h_hi.astype(F32)).astype(BF16)
    logits = _dot(h_hi, rwh_ref[...]) + (_dot(h_hi, rwl_ref[...]) + _dot(h_lo, rwh_ref[...]))

    lane = lax.broadcasted_iota(jnp.int32, (tm, LANES), 1)
    lanef = lane.astype(F32)
    neg_inf = jnp.float32(-jnp.inf)
    big = jnp.float32(1e9)
    is_g = (lane >= N_EXPERTS) & (lane < N_EXPERTS + N_GROUPS)
    gl = jnp.where(is_g, logits, neg_inf)
    gmax = jnp.max(gl, axis=-1, keepdims=True)
    gidx = jnp.min(jnp.where(gl == gmax, lanef, big), axis=-1, keepdims=True) - N_EXPERTS
    g_w = 1.0 / jnp.sum(jnp.exp(gl - gmax), axis=-1, keepdims=True)
    lane_group = jnp.floor(lanef * (1.0 / EXPERTS_PER_GROUP))
    in_group = (lane < N_EXPERTS) & (lane_group == gidx)
    el = jnp.where(in_group, logits, neg_inf)
    m1 = jnp.max(el, axis=-1, keepdims=True)
    i1 = jnp.min(jnp.where(el == m1, lanef, big), axis=-1, keepdims=True)
    el2 = jnp.where(lanef == i1, neg_inf, el)
    m2 = jnp.max(el2, axis=-1, keepdims=True)
    i2 = jnp.min(jnp.where(el2 == m2, lanef, big), axis=-1, keepdims=True)
    tt = jnp.exp(m2 - m1)
    w1 = g_w / (1.0 + tt)
    w2 = g_w * tt / (1.0 + tt)

    oh1 = lanef == i1
    oh2 = lanef == i2
    e_cnt = jnp.where(oh1, 1.0, 0.0) + jnp.where(oh2, 1.0, 0.0)
    ri = lax.broadcasted_iota(jnp.int32, (tm, tm), 0)
    ci = lax.broadcasted_iota(jnp.int32, (tm, tm), 1)
    strict = jnp.where(ri > ci, 1.0, 0.0).astype(BF16)
    prefix = _dot(strict, e_cnt.astype(BF16)) + run_scr[0:1, :]
    r1 = jnp.sum(jnp.where(oh1, prefix, 0.0), axis=-1, keepdims=True)
    r2 = jnp.sum(jnp.where(oh2, prefix, 0.0), axis=-1, keepdims=True)
    run_scr[0:1, :] = run_scr[0:1, :] + jnp.sum(e_cnt, axis=0, keepdims=True)
    cnt_ref[...] = run_scr[...]

    route = jnp.where(lane == 0, i1, 0.0)
    route = jnp.where(lane == 1, i2, route)
    route = jnp.where(lane == 2, w1, route)
    route = jnp.where(lane == 3, w2, route)
    route = jnp.where(lane == 4, r1, route)
    route = jnp.where(lane == 5, r2, route)
    route_ref[...] = route


def _post(p, ro_p, ro_s, ho_p, ho_s, x, wr, wh, wo, fgain, rw_hi, rw_lo, precise=None):
    t = x.shape[0]
    tm = TOKEN_TILE
    assert ro_p.shape[0] % tm == 0 and ro_s.shape[0] % tm == 0
    npt = ro_p.shape[0] // tm
    prompt_idx = lambda i: (jnp.minimum(i, npt - 1), 0)
    sample_idx = lambda i: (jnp.maximum(i - npt, 0), 0)
    gate_block = C_GATES // (2 * D_MODEL)
    const2 = lambda i: (0, 0)
    in_specs = [
        pl.BlockSpec((tm, 2 * D_MODEL), lambda i: (i, gate_block)),
        pl.BlockSpec((tm, RET_HEADS * RET_DV), prompt_idx),
        pl.BlockSpec((tm, RET_HEADS * RET_DV), sample_idx),
        pl.BlockSpec((tm, HG_HEADS * HG_DV), prompt_idx),
        pl.BlockSpec((tm, HG_HEADS * HG_DV), sample_idx),
        pl.BlockSpec((tm, D_MODEL), lambda i: (i, 0)),
        pl.BlockSpec((RET_HEADS * RET_DV, D_MODEL), const2),
        pl.BlockSpec((HG_HEADS * HG_DV, D_MODEL), const2),
        pl.BlockSpec((D_MODEL, D_MODEL), const2),
    ]
    args = [p, ro_p, ro_s, ho_p, ho_s, x, wr, wh, wo]
    if precise is not None:
        in_specs += [
            pl.BlockSpec((tm, 2 * D_MODEL), lambda i: (jnp.maximum(i - npt, 0), gate_block)),
            pl.BlockSpec((RET_HEADS * RET_DV, D_MODEL), const2),
            pl.BlockSpec((HG_HEADS * HG_DV, D_MODEL), const2),
            pl.BlockSpec((D_MODEL, D_MODEL), const2),
        ]
        args += list(precise)
    in_specs += [
        pl.BlockSpec((1, D_MODEL), const2),
        pl.BlockSpec((D_MODEL, LANES), const2),
        pl.BlockSpec((D_MODEL, LANES), const2),
    ]
    args += [fgain, rw_hi, rw_lo]
    return pl.pallas_call(
        functools.partial(_post_kernel, npt, precise is not None),
        grid=(t // tm,),
        in_specs=in_specs,
        out_specs=[
            pl.BlockSpec((tm, D_MODEL), lambda i: (i, 0)),
            pl.BlockSpec((tm * TILE_ROWS, LANES), lambda i: (i, 0)),
            pl.BlockSpec((tm, LANES), lambda i: (i, 0)),
            pl.BlockSpec((8, LANES), const2),
        ],
        out_shape=[
            jax.ShapeDtypeStruct((t, D_MODEL), F32),
            jax.ShapeDtypeStruct((t * TILE_ROWS, LANES), F32),
            jax.ShapeDtypeStruct((t, LANES), F32),
            jax.ShapeDtypeStruct((8, LANES), F32),
        ],
        scratch_shapes=[pltpu.VMEM((8, LANES), F32)],
        compiler_params=pltpu.CompilerParams(
            dimension_semantics=("arbitrary",), vmem_limit_bytes=VMEM_LIMIT),
        name="post",
    )(*args)


def _plan_kernel(pairs_per_step, n_blocks,
                 dest_ref, cnt_ref, slot_ref, bexp_ref, nused_ref):
    step = pl.program_id(0)
    base = step * pairs_per_step

    @pl.when(step == 0)
    def _():
        def per_expert(e, carry):
            start, last_e = carry
            nb = (cnt_ref[e] + (ROW_BLOCK - 1)) // ROW_BLOCK
            first = start // ROW_BLOCK

            def fill(bb, _):
                bexp_ref[first + bb] = e
                return 0

            lax.fori_loop(0, nb, fill, 0)
            return start + nb * ROW_BLOCK, jnp.where(nb > 0, e, last_e)

        total, last_e = lax.fori_loop(0, N_EXPERTS, per_expert, (jnp.int32(0), jnp.int32(0)))
        n_used = total // ROW_BLOCK
        nused_ref[0] = n_used

        def fill_tail(bb, _):
            bexp_ref[bb] = last_e
            return 0

        lax.fori_loop(n_used, n_blocks, fill_tail, 0)

        def init(r, _):
            slot_ref[r] = 0
            return 0

        lax.fori_loop(0, n_blocks * ROW_BLOCK, init, 0, unroll=16)

    def place(i, _):
        slot_ref[dest_ref[base + i]] = (base + i) >> 1
        return 0

    lax.fori_loop(0, pairs_per_step, place, 0, unroll=16)


def _plan(dest, counts, n_blocks):
    n_pairs = dest.shape[0]
    pairs_per_step = 2 * TOKEN_TILE
    assert n_pairs % pairs_per_step == 0
    smem = pl.BlockSpec(memory_space=pltpu.SMEM)
    return pl.pallas_call(
        functools.partial(_plan_kernel, pairs_per_step, n_blocks),
        grid=(n_pairs // pairs_per_step,),
        in_specs=[smem, smem],
        out_specs=[smem, smem, smem],
        out_shape=[
            jax.ShapeDtypeStruct((n_blocks * ROW_BLOCK,), jnp.int32),
            jax.ShapeDtypeStruct((n_blocks,), jnp.int32),
            jax.ShapeDtypeStruct((1,), jnp.int32),
        ],
        compiler_params=pltpu.CompilerParams(dimension_semantics=("arbitrary",)),
        name="plan",
    )(dest, counts)


def _expert_kernel(bexp_ref, nused_ref, slot_ref, h2_hbm, wg_ref, wu_ref, wd_ref, yb_ref,
                   xbuf, sems, wg_s, wu_s, wd_s):
    b = pl.program_id(0)
    n_used = nused_ref[0]

    def row_copy(tok, buf_slot, r):
        return pltpu.make_async_copy(
            h2_hbm.at[pl.ds(pl.multiple_of(tok * TILE_ROWS, TILE_ROWS), TILE_ROWS)],
            xbuf.at[buf_slot, pl.ds(pl.multiple_of(r * TILE_ROWS, TILE_ROWS), TILE_ROWS)],
            sems.at[buf_slot])

    def wait_block(block):
        buf_slot = block % 2
        pltpu.make_async_copy(h2_hbm.at[pl.ds(0, ROW_BLOCK * TILE_ROWS)], xbuf.at[buf_slot],
                              sems.at[buf_slot]).wait()

    def gather(block):
        def issue(r, _):
            row_copy(slot_ref[block * ROW_BLOCK + r], block % 2, r).start()
            return 0

        lax.fori_loop(0, ROW_BLOCK, issue, 0, unroll=8)

    @pl.when(b == 0)
    def _():
        gather(b)

    prev = bexp_ref[jnp.maximum(b - 1, 0)]

    @pl.when((b == 0) | (bexp_ref[b] != prev))
    def _():
        wg_s[...] = wg_ref[...].astype(BF16)
        wu_s[...] = wu_ref[...].astype(BF16)
        wd_s[...] = wd_ref[...].astype(BF16)

    @pl.when(b < n_used)
    def _():
        buf_slot = b % 2
        wait_block(b)
        gather(b + 1)
        x = jnp.concatenate(
            [xbuf[buf_slot, pl.ds(kk, ROW_BLOCK, stride=TILE_ROWS), :] for kk in range(TILE_ROWS)],
            axis=1).astype(BF16)
        g = _dot(x, wg_s[...])
        u = _dot(x, wu_s[...])
        a = (g * _sigmoid(g) * u).astype(BF16)
        y = _dot(a, wd_s[...])
        for kk in range(TILE_ROWS):
            yb_ref[pl.ds(kk, ROW_BLOCK, stride=TILE_ROWS), :] = y[:, kk * LANES:(kk + 1) * LANES]

    @pl.when(b == n_used)
    def _():
        wait_block(b)

    @pl.when(b >= n_used)
    def _():
        yb_ref[...] = jnp.zeros_like(yb_ref)


def _experts(layer, bexp, nused, slot_tok, h2, w_gate, w_up, w_down):
    n_blocks = bexp.shape[0]
    assert slot_tok.shape[0] == n_blocks * ROW_BLOCK
    wmap = lambda b, be, nu, st: (layer, be[b], 0, 0)
    return pl.pallas_call(
        _expert_kernel,
        grid_spec=pltpu.PrefetchScalarGridSpec(
            num_scalar_prefetch=3,
            grid=(n_blocks,),
            in_specs=[
                pl.BlockSpec(memory_space=pl.ANY),
                pl.BlockSpec((None, None, D_MODEL, D_EXPERT), wmap),
                pl.BlockSpec((None, None, D_MODEL, D_EXPERT), wmap),
                pl.BlockSpec((None, None, D_EXPERT, D_MODEL), wmap),
            ],
            out_specs=pl.BlockSpec((ROW_BLOCK * TILE_ROWS, LANES), lambda b, be, nu, st: (b, 0)),
            scratch_shapes=[
                pltpu.VMEM((2, ROW_BLOCK * TILE_ROWS, LANES), F32),
                pltpu.SemaphoreType.DMA((2,)),
                pltpu.VMEM((D_MODEL, D_EXPERT), BF16),
                pltpu.VMEM((D_MODEL, D_EXPERT), BF16),
                pltpu.VMEM((D_EXPERT, D_MODEL), BF16),
            ],
        ),
        out_shape=jax.ShapeDtypeStruct((n_blocks * ROW_BLOCK * TILE_ROWS, LANES), F32),
        compiler_params=pltpu.CompilerParams(
            dimension_semantics=("arbitrary",), vmem_limit_bytes=VMEM_LIMIT),
        name="experts",
    )(bexp, nused, slot_tok, h2, w_gate, w_up, w_down)


def _combine_kernel(final_tiles, dest_ref, route_ref, x1_ref, yb_hbm, fg_ref, *rest):
    if final_tiles is not None:
        yp_ref, ys_ref, buf, sems = rest
    else:
        x2_ref, buf, sems = rest
    i = pl.program_id(0)
    tm = x1_ref.shape[0]

    def row_copy(d, buf_slot, r):
        return pltpu.make_async_copy(
            yb_hbm.at[pl.ds(pl.multiple_of(d * TILE_ROWS, TILE_ROWS), TILE_ROWS)],
            buf.at[buf_slot, pl.ds(pl.multiple_of(r * TILE_ROWS, TILE_ROWS), TILE_ROWS)],
            sems.at[buf_slot])

    def gather(tile):
        buf_slot = tile % 2
        base = tile * tm

        def issue(tt, _):
            row_copy(dest_ref[2 * (base + tt)], buf_slot, tt).start()
            row_copy(dest_ref[2 * (base + tt) + 1], buf_slot, tm + tt).start()
            return 0

        lax.fori_loop(0, tm, issue, 0, unroll=4)

    @pl.when(i == 0)
    def _():
        gather(i)

    @pl.when(i + 1 < pl.num_programs(0))
    def _():
        gather(i + 1)

    buf_slot = i % 2
    pltpu.make_async_copy(yb_hbm.at[pl.ds(0, 2 * tm * TILE_ROWS)], buf.at[buf_slot], sems.at[buf_slot]).wait()

    w1 = route_ref[:, 2:3]
    w2 = route_ref[:, 3:4]
    parts = []
    for kk in range(TILE_ROWS):
        y = (w1 * buf[buf_slot, pl.ds(kk, tm, stride=TILE_ROWS), :]
             + w2 * buf[buf_slot, pl.ds(tm * TILE_ROWS + kk, tm, stride=TILE_ROWS), :])
        parts.append(x1_ref[:, kk * LANES:(kk + 1) * LANES] + y)
    x2 = jnp.concatenate(parts, axis=1)
    if final_tiles is None:
        x2_ref[...] = x2
    else:
        first_real, n_prompt_tiles = final_tiles
        yn = x2 * lax.rsqrt(jnp.mean(x2 * x2, axis=-1, keepdims=True) + EPS) * fg_ref[...]

        @pl.when((i >= first_real) & (i < n_prompt_tiles))
        def _():
            yp_ref[...] = yn

        @pl.when(i >= n_prompt_tiles)
        def _():
            ys_ref[...] = yn


def _combine(dest, route, x1, yb, fgain, final_rows=None):
    t = x1.shape[0]
    tm = COMBINE_TILE
    row_spec = pl.BlockSpec((tm, D_MODEL), lambda i, dst: (i, 0))
    if final_rows is None:
        final_tiles = None
        out_specs = [row_spec]
        out_shape = [jax.ShapeDtypeStruct((t, D_MODEL), F32)]
    else:
        first_row, n_prompt = final_rows
        assert first_row % tm == 0 and n_prompt % tm == 0
        first_real, npt = first_row // tm, n_prompt // tm
        final_tiles = (first_real, npt)
        out_specs = [
            pl.BlockSpec((tm, D_MODEL), lambda i, dst: (jnp.clip(i - first_real, 0, npt - first_real - 1), 0)),
            pl.BlockSpec((tm, D_MODEL), lambda i, dst: (jnp.maximum(i - npt, 0), 0)),
        ]
        out_shape = [jax.ShapeDtypeStruct((n_prompt - first_row, D_MODEL), F32),
                     jax.ShapeDtypeStruct((t - n_prompt, D_MODEL), F32)]
    return pl.pallas_call(
        functools.partial(_combine_kernel, final_tiles),
        grid_spec=pltpu.PrefetchScalarGridSpec(
            num_scalar_prefetch=1,
            grid=(t // tm,),
            in_specs=[
                pl.BlockSpec((tm, LANES), lambda i, dst: (i, 0)),
                row_spec,
                pl.BlockSpec(memory_space=pl.ANY),
                pl.BlockSpec((1, D_MODEL), lambda i, dst: (0, 0)),
            ],
            out_specs=out_specs,
            scratch_shapes=[
                pltpu.VMEM((2, 2 * tm * TILE_ROWS, LANES), F32),
                pltpu.SemaphoreType.DMA((2,)),
            ],
        ),
        out_shape=out_shape,
        compiler_params=pltpu.CompilerParams(
            dimension_semantics=("arbitrary",), vmem_limit_bytes=VMEM_LIMIT),
        name="combine",
    )(dest, route, x1, yb, fgain)


def _rotary_tables(pos):
    half = RET_DK // 2
    inv_freq = 1.0 / (ROPE_BASE ** jnp.linspace(0.0, 1.0, half, dtype=F32))
    ang = pos.astype(F32)[:, None] * inv_freq[None, :]
    return jnp.cos(ang), jnp.sin(ang)


def _decay_tables(chunk):
    log_gamma = jnp.log1p(-jnp.exp2(-5.0 - jnp.arange(RET_HEADS, dtype=F32)))
    idx = jnp.arange(chunk, dtype=F32)
    rel = idx[:, None] - idx[None, :]
    causal = rel >= 0
    dint = jnp.where(causal, jnp.exp(log_gamma[:, None, None] * jnp.where(causal, rel, 0.0)), 0.0)
    qdec = jnp.exp(log_gamma[None, :] * (idx[:, None] + 1.0))
    kdec = jnp.exp(log_gamma[None, :] * (chunk - 1.0 - idx[:, None]))
    cdec = jnp.exp(log_gamma * chunk)
    qdec = jnp.repeat(qdec, RET_DK, axis=1)
    kdec = jnp.repeat(kdec, RET_DK, axis=1)
    return dint, qdec, kdec, cdec


def _split_param_kernel(w_ref, hi_ref, lo_ref):
    hi, lo = _split(w_ref[...])
    hi_ref[...] = hi
    lo_ref[...] = lo


def _split_param(w):
    rows, cols = w.shape
    tm = TOKEN_TILE
    assert rows % tm == 0 and cols % LANES == 0
    spec = pl.BlockSpec((tm, cols), lambda i: (i, 0))
    return pl.pallas_call(
        _split_param_kernel,
        grid=(rows // tm,),
        in_specs=[spec],
        out_specs=[spec, spec],
        out_shape=[jax.ShapeDtypeStruct(w.shape, BF16)] * 2,
        name="split_param",
    )(w)


def _router_split(router_group_l, router_expert_l):
    rw = jnp.concatenate(
        [router_expert_l, router_group_l,
         jnp.zeros((D_MODEL, LANES - N_EXPERTS - N_GROUPS), F32)], axis=1)
    return _split_param(rw)


def kernel(x_prompt, x_sample, state_ret, state_hgrn, meta_tokens, mix_norm, w_in, hg_lb_logits, hg_norm,
           w_ret_branch, w_hg_branch, w_out, ffn_norm, router_group, router_expert, w_gate, w_up, w_down,
           final_norm):
    depth = w_in.shape[0]
    bp, seq, d = x_prompt.shape
    bs, dec, _ = x_sample.shape
    assert bp == 1 and dec == CHUNK and d == D_MODEL and seq % CHUNK == 0

    n_real = N_META + seq
    n_sample = bs * dec
    n_pad = (-(n_real + n_sample)) % PROJ_TM
    while (n_pad + n_real) % PROMPT_CHUNK:
        n_pad += PROJ_TM
    n_prompt = n_pad + n_real
    assert n_pad > 0
    t = n_prompt + n_sample
    n_blocks = -(-(2 * t) // ROW_BLOCK) + N_EXPERTS + 1

    x = jnp.concatenate([
        jnp.zeros((n_pad, d), F32),
        meta_tokens.astype(F32),
        x_prompt[0],
        x_sample.reshape(n_sample, d),
    ], axis=0)

    rot_prompt = _rotary_tables(jnp.arange(n_prompt) - (n_pad + N_META))
    rot_sample = _rotary_tables(PAST_LEN + jnp.arange(dec))
    dec_prompt = _decay_tables(PROMPT_CHUNK)
    dec_sample = _decay_tables(dec)

    prob = jax.nn.softmax(hg_lb_logits.astype(F32), axis=0)
    cum = jnp.cumsum(prob, axis=0)
    lb_all = cum - cum[0:1]

    assert state_ret.dtype == F32 and state_hgrn.dtype == F32 and w_in.dtype == F32
    states = [jnp.zeros((depth, 1) + state_ret.shape[2:], F32), jnp.zeros((depth, 1) + state_hgrn.shape[2:], F32),
              jnp.zeros(state_ret.shape, F32), jnp.zeros(state_hgrn.shape, F32)]
    for l in range(depth):
        lb = lb_all[l]
        lbt = jnp.concatenate(
            [jnp.log(lb)[None], jnp.log1p(-lb)[None], (1.0 - lb)[None], jnp.zeros((5, d), F32)], axis=0)
        p = _proj(l, x, mix_norm[l][None], w_in)
        hgain = hg_norm[l][None].astype(F32)
        precise = l < depth - 1
        branch_w = [_split_param(w[l]) for w in (w_ret_branch, w_hg_branch, w_out)]
        if precise:
            n_plain = n_prompt - PRECISE_TAIL
            p_s = _proj_precise(l, x, n_plain, PRECISE_TAIL + n_sample, mix_norm[l][None], w_in)
            ro_p, ho_p, ret_p, hg_p = _mixer(l, p, PROMPT_CHUNK, 0, n_plain, rot_prompt, dec_prompt, lbt, hgain,
                                             states[:2])
            rot_tail = tuple(r[n_plain:] for r in rot_prompt)
            ro_t, ho_t, ret_p, hg_p = _mixer(l, p_s, PROMPT_CHUNK, 0, PRECISE_TAIL, rot_tail, dec_prompt, lbt, hgain,
                                             [ret_p, hg_p], states_in=(ret_p, hg_p), precise=True, mode="continue")
            ro_s, ho_s, ret_s, hg_s = _mixer(l, p_s, dec, PRECISE_TAIL, n_sample, rot_sample, dec_sample, lbt, hgain,
                                             states[2:], states_in=(state_ret, state_hgrn), precise=True)
            ro_s = jnp.concatenate([ro_t, ro_s], axis=0)
            ho_s = jnp.concatenate([ho_t, ho_s], axis=0)
            post_precise = (p_s,) + tuple(lo for _, lo in branch_w)
        else:
            ro_p, ho_p, ret_p, hg_p = _mixer(l, p, PROMPT_CHUNK, 0, n_prompt, rot_prompt, dec_prompt, lbt, hgain,
                                             states[:2])
            ro_s, ho_s, ret_s, hg_s = _mixer(l, p, dec, n_prompt, n_sample, rot_sample, dec_sample, lbt, hgain,
                                             states[2:], states_in=(state_ret, state_hgrn))
            post_precise = None
        states = [ret_p, hg_p, ret_s, hg_s]
        rw_hi, rw_lo = _router_split(router_group[l], router_expert[l])
        x1, h2, route, counts = _post(
            p, ro_p, ro_s, ho_p, ho_s, x, branch_w[0][0], branch_w[1][0], branch_w[2][0],
            ffn_norm[l][None], rw_hi, rw_lo, precise=post_precise)
        ids = route[:, 0:2].astype(jnp.int32)
        ranks = route[:, 4:6].astype(jnp.int32)
        cnt = counts[0, :N_EXPERTS].astype(jnp.int32)
        padded = (cnt + (ROW_BLOCK - 1)) // ROW_BLOCK * ROW_BLOCK
        pstart = jnp.cumsum(padded) - padded
        dest = (pstart[ids] + ranks).reshape(-1)
        slot_tok, bexp, nused = _plan(dest, cnt, n_blocks)
        yb = _experts(l, bexp, nused, slot_tok, h2, w_gate, w_up, w_down)
        if l < depth - 1:
            x, = _combine(dest, route, x1, yb, final_norm[None])
        else:
            y_prompt, y_sample = _combine(dest, route, x1, yb, final_norm[None],
                                          final_rows=(n_prompt - seq, n_prompt))

    ret_prompt, hgrn_prompt, ret_sample, hgrn_sample = states
    return (y_prompt.reshape(bp, seq, d), y_sample.reshape(bs, dec, d),
            ret_prompt, hgrn_prompt, ret_sample, hgrn_sample)
```
